```python
import math
import jax, jax.numpy as jnp
from jax import lax
import numpy as np

D_MODEL = 2048
BATCH = 4
SEQ = 2048
DEPTH = 2

N_EVEN = (DEPTH + 1) // 2
N_ODD = DEPTH // 2
EPS = 1e-6

POOL_WINDOWS = (2, 4, 8, 16)
N_POOL_GROUPS = 4
POOL_WIDTH = D_MODEL // 2
POOL_GROUP = POOL_WIDTH // N_POOL_GROUPS
FOX_HEADS = 8
FOX_HEAD_DIM = 128
FOX_WIDTH = FOX_HEADS * FOX_HEAD_DIM
Q_BLOCK = 128
MIX0_IN = POOL_WIDTH + 3 * FOX_WIDTH + FOX_HEADS
MIX0_OUT = POOL_WIDTH + FOX_WIDTH
SSD_INNER = 2 * D_MODEL
SSD_HEAD_DIM = 64
SSD_HEADS = SSD_INNER // SSD_HEAD_DIM
SSD_STATE = 128
SSD_GROUPS = 8
SSD_CONV = 4
SSD_CHUNK = 128
SSD_CONV_DIM = SSD_INNER + 2 * SSD_GROUPS * SSD_STATE
SSD_IN = SSD_INNER + SSD_CONV_DIM + SSD_HEADS
MOE_GROUPS = 4
MOE_EXPERTS_PER_GROUP = 8
MOE_EXPERTS = MOE_GROUPS * MOE_EXPERTS_PER_GROUP
MOE_TOP_K = 2
MOE_FF = D_MODEL // 4
MOE_BLOCK = 128

kernel_name = "hybrid_pool_fox_ssd_hmoe"


def rmsnorm(x, g):
    xf = x.astype(jnp.float32)
    y = xf * lax.rsqrt(jnp.mean(xf * xf, axis=-1, keepdims=True) + EPS)
    return (y * g.astype(jnp.float32)).astype(x.dtype)


def causal_pool_mixer(u, pool_w, pool_scale):
    B_, L, _ = u.shape
    uf = u.astype(jnp.float32)
    cs = jnp.cumsum(uf, axis=1)
    pos = jnp.arange(1, L + 1)
    outs = []
    for g, w in enumerate(POOL_WINDOWS):
        sl = slice(g * POOL_GROUP, (g + 1) * POOL_GROUP)
        cg = cs[:, :, sl]
        shifted = jnp.pad(cg, ((0, 0), (w, 0), (0, 0)))[:, :L]
        count = jnp.minimum(pos, w).astype(jnp.float32)[None, :, None]
        mixed = ((cg - shifted) / count - uf[:, :, sl]).astype(u.dtype)
        outs.append(mixed @ pool_w[g])
    return jnp.concatenate(outs, axis=-1) * pool_scale


def forgetting_attention(q, k, v, log_f):
    B_, L, H, Dh = q.shape
    nb = L // Q_BLOCK
    c = jnp.cumsum(log_f, axis=1)
    c_bh = c.transpose(0, 2, 1)
    qb = q.reshape(B_, nb, Q_BLOCK, H, Dh).transpose(1, 0, 2, 3, 4)
    cq = c.reshape(B_, nb, Q_BLOCK, H).transpose(1, 0, 2, 3)
    kpos = jnp.arange(L)
    scale = Dh ** -0.5

    def block(args):
        i, qi, ci = args
        s = jnp.einsum('bqhd,bkhd->bhqk', qi, k, preferred_element_type=jnp.float32) * scale
        s = s + ci.transpose(0, 2, 1)[..., None] - c_bh[:, :, None, :]
        qpos = i * Q_BLOCK + jnp.arange(Q_BLOCK)
        mask = kpos[None, :] <= qpos[:, None]
        s = jnp.where(mask[None, None], s, -jnp.inf)
        p = jax.nn.softmax(s, axis=-1)
        return jnp.einsum('bhqk,bkhd->bqhd', p.astype(v.dtype), v)

    out = lax.map(block, (jnp.arange(nb), qb, cq))
    return out.transpose(1, 0, 2, 3, 4).reshape(B_, L, H * Dh)


def pool_fox_mixer(u, w_in, b_forget, pool_w, pool_scale, w_out):
    B_, L, _ = u.shape
    proj = u @ w_in
    u_pool, q, k, v, f_logit = jnp.split(
        proj, [POOL_WIDTH, POOL_WIDTH + FOX_WIDTH, POOL_WIDTH + 2 * FOX_WIDTH,
               POOL_WIDTH + 3 * FOX_WIDTH], axis=-1)
    y_pool = causal_pool_mixer(u_pool, pool_w, pool_scale)
    log_f = jax.nn.log_sigmoid((f_logit + b_forget).astype(jnp.float32))
    hs = (B_, L, FOX_HEADS, FOX_HEAD_DIM)
    y_att = forgetting_attention(q.reshape(hs), k.reshape(hs), v.reshape(hs), log_f)
    return jnp.concatenate([y_pool, y_att.astype(u.dtype)], axis=-1) @ w_out


def causal_depthwise_conv(u, w, b):
    out = lax.conv_general_dilated(
        u, w[:, None, :], window_strides=(1,), padding=[(SSD_CONV - 1, 0)],
        dimension_numbers=('NWC', 'WIO', 'NWC'), feature_group_count=u.shape[-1])
    return out + b


def ssd_chunked(x, dt, a, bm, cm):
    B_, L, H, P = x.shape
    nc = L // SSD_CHUNK
    hg = H // SSD_GROUPS
    x = x.reshape(B_, nc, SSD_CHUNK, SSD_GROUPS, hg, P)
    dt = dt.reshape(B_, nc, SSD_CHUNK, SSD_GROUPS, hg)
    bm = bm.reshape(B_, nc, SSD_CHUNK, SSD_GROUPS, SSD_STATE)
    cm = cm.reshape(B_, nc, SSD_CHUNK, SSD_GROUPS, SSD_STATE)
    a_dt = dt * a.reshape(SSD_GROUPS, hg)
    x_dt = x * dt[..., None]
    acs = jnp.cumsum(a_dt, axis=2)
    idx = jnp.arange(SSD_CHUNK)
    causal = (idx[:, None] >= idx[None, :])[None, None, :, :, None, None]
    seg = acs[:, :, :, None] - acs[:, :, None, :]
    decay = jnp.exp(jnp.where(causal, seg, -jnp.inf))
    cb = jnp.einsum('bclgn,bcsgn->bclsg', cm, bm)
    y_diag = jnp.einsum('bclsgh,bcsghp->bclghp', cb[..., None] * decay, x_dt)
    decay_states = jnp.exp(acs[:, :, -1:] - acs)
    states = jnp.einsum('bcsgn,bcsghp->bcghpn', bm, x_dt * decay_states[..., None])
    chunk_decay = jnp.exp(acs[:, :, -1])

    def step(h, inp):
        s_c, d_c = inp
        return h * d_c[..., None, None] + s_c, h

    h0 = jnp.zeros((B_, SSD_GROUPS, hg, P, SSD_STATE), jnp.float32)
    _, prev = lax.scan(step, h0, (states.transpose(1, 0, 2, 3, 4, 5),
                                  chunk_decay.transpose(1, 0, 2, 3)))
    prev = prev.transpose(1, 0, 2, 3, 4, 5)
    y_off = jnp.einsum('bclgn,bcghpn->bclghp', cm, prev) * jnp.exp(acs)[..., None]
    return (y_diag + y_off).reshape(B_, L, H, P)


def ssd_mixer(u, w_in, conv_w, conv_b, dt_bias, a_log, d_skip, gnorm, w_out):
    B_, L, _ = u.shape
    proj = u @ w_in
    z, xbc, dt = jnp.split(proj, [SSD_INNER, SSD_INNER + SSD_CONV_DIM], axis=-1)
    xbc = jax.nn.silu(causal_depthwise_conv(xbc, conv_w, conv_b))
    xs, bm, cm = jnp.split(xbc, [SSD_INNER, SSD_INNER + SSD_GROUPS * SSD_STATE], axis=-1)
    dt = jax.nn.softplus((dt + dt_bias).astype(jnp.float32))
    a = -jnp.exp(a_log.astype(jnp.float32))
    xh = xs.reshape(B_, L, SSD_HEADS, SSD_HEAD_DIM).astype(jnp.float32)
    gs = (B_, L, SSD_GROUPS, SSD_STATE)
    y = ssd_chunked(xh, dt, a, bm.reshape(gs).astype(jnp.float32),
                    cm.reshape(gs).astype(jnp.float32))
    y = (y + xh * d_skip.astype(jnp.float32)[:, None]).reshape(B_, L, SSD_INNER)
    gated = (y * jax.nn.silu(z.astype(jnp.float32))).reshape(B_, L, SSD_GROUPS, -1)
    gated = gated * lax.rsqrt(jnp.mean(gated * gated, axis=-1, keepdims=True) + EPS)
    gated = gated.reshape(B_, L, SSD_INNER) * gnorm.astype(jnp.float32)
    return gated.astype(u.dtype) @ w_out


def hier_moe(h, router_g, router_g_b, router_e, router_e_b, w_gate, w_up, w_down):
    B_, L, D = h.shape
    T = B_ * L
    xt = h.reshape(T, D)
    g_logits = (xt @ router_g + router_g_b).astype(jnp.float32)
    g_prob = jax.nn.softmax(g_logits, axis=-1)
    g_sel = jnp.argmax(g_logits, axis=-1)
    g_w = jnp.take_along_axis(g_prob, g_sel[:, None], axis=1)[:, 0]
    e_logits = (xt @ router_e + router_e_b).astype(jnp.float32)
    e_logits = e_logits.reshape(T, MOE_GROUPS, MOE_EXPERTS_PER_GROUP)
    e_logits = jnp.take_along_axis(e_logits, g_sel[:, None, None], axis=1)[:, 0]
    e_prob = jax.nn.softmax(e_logits, axis=-1)
    top_w, top_i = lax.top_k(e_prob, MOE_TOP_K)
    top_w = top_w / jnp.sum(top_w, axis=-1, keepdims=True)
    weights = (g_w[:, None] * top_w).reshape(-1)
    expert_id = (g_sel[:, None] * MOE_EXPERTS_PER_GROUP + top_i).reshape(-1)
    token_id = jnp.repeat(jnp.arange(T), MOE_TOP_K)
    order = jnp.argsort(expert_id)
    se, st, sw = expert_id[order], token_id[order], weights[order]
    counts = jnp.zeros((MOE_EXPERTS,), jnp.int32).at[expert_id].add(1)
    padded = (counts + MOE_BLOCK - 1) // MOE_BLOCK * MOE_BLOCK
    start = jnp.cumsum(counts) - counts
    pstart = jnp.cumsum(padded) - padded
    pend = pstart + padded
    dest = pstart[se] + (jnp.arange(T * MOE_TOP_K) - start[se])
    n_rows = T * MOE_TOP_K + MOE_EXPERTS * MOE_BLOCK
    n_blocks = n_rows // MOE_BLOCK
    tok_buf = jnp.zeros((n_rows,), jnp.int32).at[dest].set(st)
    w_buf = jnp.zeros((n_rows,), jnp.float32).at[dest].set(sw)
    blk_e = jnp.clip(jnp.searchsorted(pend, jnp.arange(n_blocks) * MOE_BLOCK, side='right'),
                     0, MOE_EXPERTS - 1)
    xs = xt[tok_buf].reshape(n_blocks, MOE_BLOCK, D)

    def run_block(args):
        xb, e = args
        return (jax.nn.silu(xb @ w_gate[e]) * (xb @ w_up[e])) @ w_down[e]

    ys = lax.map(run_block, (xs, blk_e)).reshape(n_rows, D)
    out = jnp.zeros((T, D), h.dtype).at[tok_buf].add(ys * w_buf[:, None].astype(ys.dtype))
    return out.reshape(B_, L, D)


def setup_inputs(seed: int = 0) -> dict:
    key = jax.random.key(seed)
    ks = jax.random.split(key, 26)
    nrm = jax.random.normal
    f32 = jnp.float32
    x = nrm(ks[0], (BATCH, SEQ, D_MODEL), f32)
    norm_mix = 1.0 + 0.02 * nrm(ks[1], (DEPTH, D_MODEL), f32)
    norm_ffn = 1.0 + 0.02 * nrm(ks[2], (DEPTH, D_MODEL), f32)
    norm_final = 1.0 + 0.02 * nrm(ks[3], (D_MODEL,), f32)
    pf_w_in = nrm(ks[4], (N_EVEN, D_MODEL, MIX0_IN), f32) * D_MODEL ** -0.5
    pf_b_forget = 3.0 + 0.5 * nrm(ks[5], (N_EVEN, FOX_HEADS), f32)
    pf_pool_w = nrm(ks[6], (N_EVEN, N_POOL_GROUPS, POOL_GROUP, POOL_GROUP), f32) * POOL_GROUP ** -0.5
    pf_pool_scale = 1.0 + 0.02 * nrm(ks[7], (N_EVEN, POOL_WIDTH), f32)
    pf_w_out = nrm(ks[8], (N_EVEN, MIX0_OUT, D_MODEL), f32) * MIX0_OUT ** -0.5
    ssd_w_in = nrm(ks[9], (N_ODD, D_MODEL, SSD_IN), f32) * D_MODEL ** -0.5
    ssd_conv_w = nrm(ks[10], (N_ODD, SSD_CONV, SSD_CONV_DIM), f32) * SSD_CONV ** -0.5
    ssd_conv_b = 0.01 * nrm(ks[11], (N_ODD, SSD_CONV_DIM), f32)
    dt0 = jnp.exp(jax.random.uniform(ks[12], (N_ODD, SSD_HEADS), f32,
                                     math.log(1e-3), math.log(1e-1)))
    ssd_dt_bias = dt0 + jnp.log(-jnp.expm1(-dt0))
    ssd_a_log = jnp.log(jax.random.uniform(ks[13], (N_ODD, SSD_HEADS), f32, 1.0, 16.0))
    ssd_d_skip = 1.0 + 0.1 * nrm(ks[14], (N_ODD, SSD_HEADS), f32)
    ssd_gnorm = 1.0 + 0.02 * nrm(ks[15], (N_ODD, SSD_INNER), f32)
    ssd_w_out = nrm(ks[16], (N_ODD, SSD_INNER, D_MODEL), f32) * SSD_INNER ** -0.5
    moe_router_g = nrm(ks[17], (DEPTH, D_MODEL, MOE_GROUPS), f32) * D_MODEL ** -0.5
    moe_router_g_b = 0.01 * nrm(ks[18], (DEPTH, MOE_GROUPS), f32)
    moe_router_e = nrm(ks[19], (DEPTH, D_MODEL, MOE_EXPERTS), f32) * D_MODEL ** -0.5
    moe_router_e_b = 0.01 * nrm(ks[20], (DEPTH, MOE_EXPERTS), f32)
    moe_w_gate = nrm(ks[21], (DEPTH, MOE_EXPERTS, D_MODEL, MOE_FF), f32) * D_MODEL ** -0.5
    moe_w_up = nrm(ks[22], (DEPTH, MOE_EXPERTS, D_MODEL, MOE_FF), f32) * D_MODEL ** -0.5
    moe_w_down = nrm(ks[23], (DEPTH, MOE_EXPERTS, MOE_FF, D_MODEL), f32) * MOE_FF ** -0.5
    return {"x": x, "norm_mix": norm_mix, "norm_ffn": norm_ffn, "norm_final": norm_final,
            "pf_w_in": pf_w_in, "pf_b_forget": pf_b_forget, "pf_pool_w": pf_pool_w,
            "pf_pool_scale": pf_pool_scale, "pf_w_out": pf_w_out,
            "ssd_w_in": ssd_w_in, "ssd_conv_w": ssd_conv_w, "ssd_conv_b": ssd_conv_b,
            "ssd_dt_bias": ssd_dt_bias, "ssd_a_log": ssd_a_log, "ssd_d_skip": ssd_d_skip,
            "ssd_gnorm": ssd_gnorm, "ssd_w_out": ssd_w_out,
            "moe_router_g": moe_router_g, "moe_router_g_b": moe_router_g_b,
            "moe_router_e": moe_router_e, "moe_router_e_b": moe_router_e_b,
            "moe_w_gate": moe_w_gate, "moe_w_up": moe_w_up, "moe_w_down": moe_w_down}


def reference(x, norm_mix, norm_ffn, norm_final,
              pf_w_in, pf_b_forget, pf_pool_w, pf_pool_scale, pf_w_out,
              ssd_w_in, ssd_conv_w, ssd_conv_b, ssd_dt_bias, ssd_a_log, ssd_d_skip,
              ssd_gnorm, ssd_w_out,
              moe_router_g, moe_router_g_b, moe_router_e, moe_router_e_b,
              moe_w_gate, moe_w_up, moe_w_down):
    for i in range(DEPTH):
        j = i // 2
        h = rmsnorm(x, norm_mix[i])
        if i % 2 == 0:
            x = x + pool_fox_mixer(h, pf_w_in[j], pf_b_forget[j], pf_pool_w[j],
                                   pf_pool_scale[j], pf_w_out[j])
        else:
            x = x + ssd_mixer(h, ssd_w_in[j], ssd_conv_w[j], ssd_conv_b[j], ssd_dt_bias[j],
                              ssd_a_log[j], ssd_d_skip[j], ssd_gnorm[j], ssd_w_out[j])
        h = rmsnorm(x, norm_ffn[i])
        x = x + hier_moe(h, moe_router_g[i], moe_router_g_b[i], moe_router_e[i],
                         moe_router_e_b[i], moe_w_gate[i], moe_w_up[i], moe_w_down[i])
    return rmsnorm(x, norm_final)
```

```python
import functools

import jax
import jax.numpy as jnp
from jax import lax
from jax.experimental import pallas as pl
from jax.experimental.pallas import tpu as pltpu

F32 = jnp.float32
BF16 = jnp.bfloat16
I32 = jnp.int32
EPS = 1e-6

POOL_WINDOWS = (2, 4, 8, 16)
POOL_GROUP = 256
POOL_WIDTH = 1024
FOX_HEADS = 8
FOX_HEAD_DIM = 128
FOX_WIDTH = 1024
SSD_HEAD_DIM = 64
SSD_STATE = 128
SSD_GROUPS = 8
SSD_GROUP_HEADS = 8
SSD_GROUP_WIDTH = SSD_GROUP_HEADS * SSD_HEAD_DIM
SSD_CONV = 4
SSD_CHUNK = 128
MOE_GROUPS = 4
MOE_PER_GROUP = 8
MOE_EXPERTS = 32
MOE_BLOCK = 256
LANES = 128
MIB = 1 << 20


def _params(semantics, vmem_mib):
    return pltpu.CompilerParams(dimension_semantics=semantics,
                                vmem_limit_bytes=vmem_mib * MIB)


def _rmsnorm_body(x_ref, g_ref, o_ref):
    x = x_ref[...]
    inv = lax.rsqrt(jnp.mean(x * x, axis=-1, keepdims=True) + EPS)
    o_ref[...] = (x * inv * g_ref[...]).astype(o_ref.dtype)


def rmsnorm(x2d, g, out_dtype, tm=512):
    t, d = x2d.shape
    return pl.pallas_call(
        _rmsnorm_body,
        grid=(t // tm,),
        in_specs=[pl.BlockSpec((tm, d), lambda i: (i, 0)),
                  pl.BlockSpec((1, d), lambda i: (0, 0))],
        out_specs=pl.BlockSpec((tm, d), lambda i: (i, 0)),
        out_shape=jax.ShapeDtypeStruct((t, d), out_dtype),
        compiler_params=_params(("arbitrary",), 40),
        name="rmsnorm",
    )(x2d, g.reshape(1, d))


def _matmul_body(*refs, has_res):
    if has_res:
        a_ref, w_ref, r_ref, o_ref, wb_ref = refs
    else:
        a_ref, w_ref, o_ref, wb_ref = refs

    @pl.when(pl.program_id(1) == 0)
    def _():
        wb_ref[...] = w_ref[...].astype(BF16)

    acc = jnp.dot(a_ref[...], wb_ref[...], preferred_element_type=F32)
    if has_res:
        acc = acc + r_ref[...]
    o_ref[...] = acc.astype(o_ref.dtype)


def matmul(a, w, n_out, *, col0=0, res=None, out_dtype=BF16, tm=1024, tn=1024,
           vmem_mib=48, name="matmul"):
    t, k = a.shape
    tn = min(tn, n_out)
    tm = min(tm, t)
    assert t % tm == 0 and n_out % tn == 0 and col0 % tn == 0
    cb0 = col0 // tn
    in_specs = [pl.BlockSpec((tm, k), lambda j, i: (i, 0)),
                pl.BlockSpec((k, tn), lambda j, i: (0, j + cb0))]
    args = [a, w]
    if res is not None:
        in_specs.append(pl.BlockSpec((tm, tn), lambda j, i: (i, j)))
        args.append(res)
    return pl.pallas_call(
        functools.partial(_matmul_body, has_res=res is not None),
        grid=(n_out // tn, t // tm),
        in_specs=in_specs,
        out_specs=pl.BlockSpec((tm, tn), lambda j, i: (i, j)),
        out_shape=jax.ShapeDtypeStruct((t, n_out), out_dtype),
        scratch_shapes=[pltpu.VMEM((k, tn), BF16)],
        compiler_params=_params(("arbitrary", "arbitrary"), vmem_mib),
        name=name,
    )(*args)


def _forget_cumsum_body(f_ref, b_ref, c_ref):
    z = f_ref[0] + b_ref[...]
    x = jnp.minimum(z, 0.0) - jnp.log1p(jnp.exp(-jnp.abs(z)))
    n = x.shape[1]
    lane = lax.broadcasted_iota(I32, x.shape, 1)
    shift = 1
    while shift < n:
        x = x + jnp.where(lane >= shift, pltpu.roll(x, shift, axis=1), 0.0)
        shift *= 2
    c_ref[0] = x


def forget_cumsum(f_t, b_forget):
    b, h, l = f_t.shape
    return pl.pallas_call(
        _forget_cumsum_body,
        grid=(b,),
        in_specs=[pl.BlockSpec((1, h, l), lambda i: (i, 0, 0)),
                  pl.BlockSpec((h, 1), lambda i: (0, 0))],
        out_specs=pl.BlockSpec((1, h, l), lambda i: (i, 0, 0)),
        out_shape=jax.ShapeDtypeStruct((b, h, l), F32),
        compiler_params=_params(("arbitrary",), 16),
        name="forget_cumsum",
    )(f_t, b_forget.reshape(h, 1))


def _pool_body(u_ref, w_ref, s_ref, o_ref):
    g = pl.program_id(1)
    u = u_ref[0].astype(F32)
    row = lax.broadcasted_iota(I32, u.shape, 0)
    acc = u
    sums = []
    for shift in (1, 2, 4, 8):
        acc = acc + jnp.where(row >= shift, pltpu.roll(acc, shift, axis=0), 0.0)
        sums.append(acc)
    win_sum = jnp.where(g == 0, sums[0],
                        jnp.where(g == 1, sums[1],
                                  jnp.where(g == 2, sums[2], sums[3])))
    window = jnp.left_shift(jnp.int32(2), g)
    count = jnp.minimum(row + 1, window).astype(F32)
    mixed = win_sum / count - u
    y = jnp.dot(mixed.astype(BF16), w_ref[0].astype(BF16),
                preferred_element_type=F32)
    o_ref[0] = (y * s_ref[...]).astype(o_ref.dtype)


def pool_mixer(proj3, pool_w, pool_scale):
    b, l, _ = proj3.shape
    ng = len(POOL_WINDOWS)
    return pl.pallas_call(
        _pool_body,
        grid=(b, ng),
        in_specs=[pl.BlockSpec((1, l, POOL_GROUP), lambda i, g: (i, 0, g)),
                  pl.BlockSpec((1, POOL_GROUP, POOL_GROUP), lambda i, g: (g, 0, 0)),
                  pl.BlockSpec((1, POOL_GROUP), lambda i, g: (0, g))],
        out_specs=pl.BlockSpec((1, l, POOL_GROUP), lambda i, g: (i, 0, g)),
        out_shape=jax.ShapeDtypeStruct((b, l, POOL_WIDTH), BF16),
        compiler_params=_params(("arbitrary", "arbitrary"), 40),
        name="pool_mixer",
    )(proj3, pool_w, pool_scale.reshape(1, POOL_WIDTH))


def _fox_body(q_ref, k_ref, v_ref, cq_ref, ck_ref, o_ref, *, tile, scale):
    h = pl.program_id(1)
    qi = pl.program_id(2)
    q = q_ref[0]
    cq8 = cq_ref[0]
    head_lane = lax.broadcasted_iota(I32, cq8.shape, 1)
    cq = jnp.sum(jnp.where(head_lane == h, cq8, 0.0), axis=1, keepdims=True)

    def step(j, carry, masked):
        m, l, acc = carry
        start = pl.multiple_of(j * tile, tile)
        k = k_ref[0, pl.ds(start, tile), :]
        v = v_ref[0, pl.ds(start, tile), :]
        ck = ck_ref[0, :, pl.ds(start, tile)]
        s = lax.dot_general(q, k, (((1,), (1,)), ((), ())),
                            preferred_element_type=F32)
        s = s * scale + cq - ck
        if masked:
            r = lax.broadcasted_iota(I32, s.shape, 0)
            c = lax.broadcasted_iota(I32, s.shape, 1)
            s = jnp.where(c <= r, s, -jnp.inf)
        m_new = jnp.maximum(m, jnp.max(s, axis=1, keepdims=True))
        alpha = jnp.exp(m - m_new)
        p = jnp.exp(s - m_new)
        l_new = alpha * l + jnp.sum(p, axis=1, keepdims=True)
        acc_new = alpha * acc + jnp.dot(p.astype(BF16), v,
                                        preferred_element_type=F32)
        return m_new, l_new, acc_new

    init = (jnp.full((tile, 1), -jnp.inf, F32), jnp.zeros((tile, 1), F32),
            jnp.zeros((tile, q.shape[1]), F32))
    carry = step(qi, init, True)
    carry = lax.fori_loop(0, qi, lambda j, c: step(j, c, False), carry)
    _, l, acc = carry
    o_ref[0] = (acc / l).astype(o_ref.dtype)


def fox_attention(proj3, c_row, c_col, tile=256):
    b, l, _ = proj3.shape
    hd = FOX_HEAD_DIM
    q0 = POOL_WIDTH // hd
    k0 = q0 + FOX_HEADS
    v0 = k0 + FOX_HEADS
    return pl.pallas_call(
        functools.partial(_fox_body, tile=tile, scale=hd ** -0.5),
        grid=(b, FOX_HEADS, l // tile),
        in_specs=[
            pl.BlockSpec((1, tile, hd), lambda i, h, q: (i, q, q0 + h)),
            pl.BlockSpec((1, l, hd), lambda i, h, q: (i, 0, k0 + h)),
            pl.BlockSpec((1, l, hd), lambda i, h, q: (i, 0, v0 + h)),
            pl.BlockSpec((1, tile, FOX_HEADS), lambda i, h, q: (i, q, 0)),
            pl.BlockSpec((1, 1, l), lambda i, h, q: (i * FOX_HEADS + h, 0, 0)),
        ],
        out_specs=pl.BlockSpec((1, tile, hd), lambda i, h, q: (i, q, h)),
        out_shape=jax.ShapeDtypeStruct((b, l, FOX_WIDTH), BF16),
        compiler_params=_params(("arbitrary", "arbitrary", "arbitrary"), 32),
        name="fox_attention",
    )(proj3, proj3, proj3, c_col, c_row)


def _ssd_dt_body(raw_ref, bias_ref, alog_ref, dt_ref, acs_ref):
    z = raw_ref[...] + bias_ref[...]
    dt = jnp.maximum(z, 0.0) + jnp.log1p(jnp.exp(-jnp.abs(z)))
    a_dt = dt * (-jnp.exp(alog_ref[...]))
    n = z.shape[0]
    r = lax.broadcasted_iota(I32, (n, n), 0)
    c = lax.broadcasted_iota(I32, (n, n), 1)
    tri = (c <= r).astype(F32)
    dt_ref[...] = dt
    acs_ref[...] = jnp.dot(tri, a_dt, preferred_element_type=F32,
                           precision=lax.Precision.HIGHEST)


def ssd_dt(raw, bias_pad, alog_pad):
    t, n = raw.shape
    spec = pl.BlockSpec((SSD_CHUNK, n), lambda i: (i, 0))
    vec = pl.BlockSpec((1, n), lambda i: (0, 0))
    return pl.pallas_call(
        _ssd_dt_body,
        grid=(t // SSD_CHUNK,),
        in_specs=[spec, vec, vec],
        out_specs=[spec, spec],
        out_shape=[jax.ShapeDtypeStruct((t, n), F32)] * 2,
        compiler_params=_params(("arbitrary",), 16),
        name="ssd_dt",
    )(raw, bias_pad.reshape(1, n), alog_pad.reshape(1, n))


def _silu(x):
    return x / (1.0 + jnp.exp(-x))


def _causal_conv_silu(u_ref, w_ref, b_ref):
    u = u_ref[0].astype(F32)
    row = lax.broadcasted_iota(I32, u.shape, 0)
    w = w_ref[...]
    out = b_ref[...] + u * w[SSD_CONV - 1:SSD_CONV, :]
    for shift in range(1, SSD_CONV):
        prev = jnp.where(row >= shift, pltpu.roll(u, shift, axis=0), 0.0)
        out = out + prev * w[SSD_CONV - 1 - shift:SSD_CONV - shift, :]
    return _silu(out)


def _pair_lanes(cols, j):
    rows = cols.shape[0]
    lane = lax.broadcasted_iota(I32, (rows, LANES), 1)
    lo = jnp.broadcast_to(cols[:, 2 * j:2 * j + 1], (rows, LANES))
    hi = jnp.broadcast_to(cols[:, 2 * j + 1:2 * j + 2], (rows, LANES))
    return jnp.where(lane < SSD_HEAD_DIM, lo, hi)


def _ssd_body(z_ref, x_ref, b_ref, c_ref, wx_ref, wb_ref, wc_ref,
              bx_ref, bb_ref, bc_ref, dt_ref, acs_ref, acst_ref,
              dskip_ref, gn_ref, o_ref, xs_ref, bs_ref, cs_ref, st_ref):
    q = SSD_CHUNK
    xs_ref[...] = _causal_conv_silu(x_ref, wx_ref, bx_ref)
    bs_ref[...] = _causal_conv_silu(b_ref, wb_ref, bb_ref).astype(BF16)
    cs_ref[...] = _causal_conv_silu(c_ref, wc_ref, bc_ref).astype(BF16)
    st_ref[...] = jnp.zeros_like(st_ref)

    r = lax.broadcasted_iota(I32, (q, q), 0)
    c = lax.broadcasted_iota(I32, (q, q), 1)
    causal = c <= r
    lane = lax.broadcasted_iota(I32, (q, LANES), 1)
    first_head = lane < SSD_HEAD_DIM

    def chunk(ci, _):
        r0 = pl.multiple_of(ci * q, q)
        xc = xs_ref[pl.ds(r0, q), :]
        bm = bs_ref[pl.ds(r0, q), :]
        cm = cs_ref[pl.ds(r0, q), :]
        zc = z_ref[0, pl.ds(r0, q), :].astype(F32)
        dt = dt_ref[0, 0, pl.ds(r0, q), :]
        acs = acs_ref[0, 0, pl.ds(r0, q), :]
        acs_t = acst_ref[0, 0, :, pl.ds(r0, q)]
        acs_last = acs[q - 1:q, :]
        exp_acs = jnp.exp(acs)
        decay_to_end = jnp.exp(acs_last - acs)
        chunk_decay = jnp.exp(acs_last)
        cb = lax.dot_general(cm, bm, (((1,), (1,)), ((), ())),
                             preferred_element_type=F32)
        bm_t = bm.astype(F32).T.astype(BF16)
        gated = []
        for j in range(SSD_GROUP_HEADS // 2):
            lanes = slice(j * LANES, (j + 1) * LANES)
            xp = xc[:, lanes]
            x_dt = xp * _pair_lanes(dt, j)
            x_dt_b = x_dt.astype(BF16)
            y_heads = []
            for hh in range(2):
                hd = 2 * j + hh
                seg = jnp.broadcast_to(acs[:, hd:hd + 1], (q, q)) - acs_t[hd:hd + 1, :]
                decay = jnp.exp(jnp.where(causal, seg, -jnp.inf))
                y_heads.append(jnp.dot((cb * decay).astype(BF16), x_dt_b,
                                       preferred_element_type=F32))
            y = jnp.where(first_head, y_heads[0], y_heads[1])
            state = st_ref[:, lanes]
            y = y + jnp.dot(cm, state.astype(BF16),
                            preferred_element_type=F32) * _pair_lanes(exp_acs, j)
            y = y + xp * dskip_ref[:, lanes]
            x_end = (x_dt * _pair_lanes(decay_to_end, j)).astype(BF16)
            st_ref[:, lanes] = state * _pair_lanes(chunk_decay, j) + jnp.dot(
                bm_t, x_end, preferred_element_type=F32)
            gated.append(y * _silu(zc[:, lanes]))
        gated = jnp.concatenate(gated, axis=1)
        inv = lax.rsqrt(jnp.mean(gated * gated, axis=-1, keepdims=True) + EPS)
        o_ref[0, pl.ds(r0, q), :] = (gated * inv * gn_ref[...]).astype(o_ref.dtype)
        return 0

    lax.fori_loop(0, x_ref.shape[1] // q, chunk, 0)


def ssd_core(proj3, conv_w, conv_b, dt_g, acs_g, acs_gt, d_skip_exp, gnorm):
    b, l, _ = proj3.shape
    gw = SSD_GROUP_WIDTH
    n = SSD_STATE
    inner = SSD_GROUPS * gw
    x_blk0 = inner // gw
    bm_blk0 = 2 * inner // n
    cm_blk0 = bm_blk0 + SSD_GROUPS
    wb_blk0 = inner // n
    wc_blk0 = wb_blk0 + SSD_GROUPS
    hpg = SSD_GROUP_HEADS
    conv_b2 = conv_b.reshape(1, -1)
    return pl.pallas_call(
        _ssd_body,
        grid=(b, SSD_GROUPS),
        in_specs=[
            pl.BlockSpec((1, l, gw), lambda i, g: (i, 0, g)),
            pl.BlockSpec((1, l, gw), lambda i, g: (i, 0, x_blk0 + g)),
            pl.BlockSpec((1, l, n), lambda i, g: (i, 0, bm_blk0 + g)),
            pl.BlockSpec((1, l, n), lambda i, g: (i, 0, cm_blk0 + g)),
            pl.BlockSpec((SSD_CONV, gw), lambda i, g: (0, g)),
            pl.BlockSpec((SSD_CONV, n), lambda i, g: (0, wb_blk0 + g)),
            pl.BlockSpec((SSD_CONV, n), lambda i, g: (0, wc_blk0 + g)),
            pl.BlockSpec((1, gw), lambda i, g: (0, g)),
            pl.BlockSpec((1, n), lambda i, g: (0, wb_blk0 + g)),
            pl.BlockSpec((1, n), lambda i, g: (0, wc_blk0 + g)),
            pl.BlockSpec((1, 1, l, hpg), lambda i, g: (i, g, 0, 0)),
            pl.BlockSpec((1, 1, l, hpg), lambda i, g: (i, g, 0, 0)),
            pl.BlockSpec((1, 1, hpg, l), lambda i, g: (i, g, 0, 0)),
            pl.BlockSpec((1, gw), lambda i, g: (0, g)),
            pl.BlockSpec((1, gw), lambda i, g: (0, g)),
        ],
        out_specs=pl.BlockSpec((1, l, gw), lambda i, g: (i, 0, g)),
        out_shape=jax.ShapeDtypeStruct((b, l, inner), BF16),
        scratch_shapes=[pltpu.VMEM((l, gw), F32), pltpu.VMEM((l, n), BF16),
                        pltpu.VMEM((l, n), BF16), pltpu.VMEM((n, gw), F32)],
        compiler_params=_params(("arbitrary", "arbitrary"), 48),
        name="ssd_core",
    )(proj3, proj3, proj3, proj3, conv_w, conv_w, conv_w, conv_b2, conv_b2,
      conv_b2, dt_g, acs_g, acs_gt, d_skip_exp, gnorm.reshape(1, inner))


def _first_argmax(vals, nrows):
    row = lax.broadcasted_iota(I32, vals.shape, 0)
    top = jnp.max(vals, axis=0, keepdims=True)
    idx = jnp.min(jnp.where(vals == top, row, nrows), axis=0, keepdims=True)
    return top, idx, row


def _router_body(x_ref, g_ref, rt_ref, rb_ref, eid_ref, wt_ref, rank_ref,
                 cnt_ref, carry_ref):
    step = pl.program_id(0)

    @pl.when(step == 0)
    def _():
        carry_ref[...] = jnp.zeros_like(carry_ref)

    x = x_ref[...]
    tm = x.shape[0]
    h = x * lax.rsqrt(jnp.mean(x * x, axis=-1, keepdims=True) + EPS) * g_ref[...]
    logits = lax.dot_general(rt_ref[...], h, (((1,), (1,)), ((), ())),
                             preferred_element_type=F32,
                             precision=lax.Precision.HIGHEST) + rb_ref[...]
    e_logits = logits[:MOE_EXPERTS]
    g_logits = logits[MOE_EXPERTS:MOE_EXPERTS + MOE_GROUPS]
    g_max, g_sel, _ = _first_argmax(g_logits, MOE_GROUPS)
    g_w = 1.0 / jnp.sum(jnp.exp(g_logits - g_max), axis=0, keepdims=True)
    sel = jnp.zeros((MOE_PER_GROUP, tm), F32)
    for grp in range(MOE_GROUPS):
        sel = jnp.where(g_sel == grp,
                        e_logits[grp * MOE_PER_GROUP:(grp + 1) * MOE_PER_GROUP], sel)
    m1, i1, row8 = _first_argmax(sel, MOE_PER_GROUP)
    rest = jnp.where(row8 == i1, -jnp.inf, sel)
    m2, i2, _ = _first_argmax(rest, MOE_PER_GROUP)
    p2 = jnp.exp(m2 - m1)
    w1 = g_w / (1.0 + p2)
    w2 = g_w * p2 / (1.0 + p2)
    e1 = g_sel * MOE_PER_GROUP + i1
    e2 = g_sel * MOE_PER_GROUP + i2

    r = lax.broadcasted_iota(I32, (tm, tm), 0)
    c = lax.broadcasted_iota(I32, (tm, tm), 1)
    upper = jnp.where(r <= c, 1.0, 0.0).astype(BF16)
    row32 = lax.broadcasted_iota(I32, (MOE_EXPERTS, tm), 0)
    hit1 = row32 == e1
    hit2 = row32 == e2
    cum1 = jnp.dot(jnp.where(hit1, 1.0, 0.0).astype(BF16), upper,
                   preferred_element_type=F32)
    cum2 = jnp.dot(jnp.where(hit2, 1.0, 0.0).astype(BF16), upper,
                   preferred_element_type=F32)
    carry = carry_ref[...]
    tot1 = cum1[:, tm - 1:tm]
    tot2 = cum2[:, tm - 1:tm]
    rank1 = jnp.sum(jnp.where(hit1, carry + cum1 - 1.0, 0.0), axis=0, keepdims=True)
    rank2 = jnp.sum(jnp.where(hit2, carry + tot1 + cum2 - 1.0, 0.0), axis=0,
                    keepdims=True)
    new_carry = carry + tot1 + tot2
    carry_ref[...] = new_carry

    eid_ref[...] = jnp.concatenate([e1, e2], axis=0)
    wt_ref[...] = jnp.concatenate([w1, w2], axis=0)
    rank_ref[...] = jnp.concatenate([rank1, rank2], axis=0).astype(I32)
    cnt_ref[...] = jnp.broadcast_to(new_carry, cnt_ref.shape).astype(I32)


def moe_router(x2d, g, router_g, router_g_b, router_e, router_e_b, tm=512):
    t, d = x2d.shape
    pad = 8 - MOE_GROUPS
    rt = jnp.concatenate([router_e.T, router_g.T, jnp.zeros((pad, d), F32)], axis=0)
    rb = jnp.concatenate([router_e_b, router_g_b, jnp.zeros((pad,), F32)]).reshape(-1, 1)
    nr = rt.shape[0]
    tok = pl.BlockSpec((2, tm), lambda i: (0, i))
    return pl.pallas_call(
        _router_body,
        grid=(t // tm,),
        in_specs=[pl.BlockSpec((tm, d), lambda i: (i, 0)),
                  pl.BlockSpec((1, d), lambda i: (0, 0)),
                  pl.BlockSpec((nr, d), lambda i: (0, 0)),
                  pl.BlockSpec((nr, 1), lambda i: (0, 0))],
        out_specs=[tok, tok, tok,
                   pl.BlockSpec((MOE_EXPERTS, LANES), lambda i: (0, 0))],
        out_shape=[jax.ShapeDtypeStruct((2, t), I32),
                   jax.ShapeDtypeStruct((2, t), F32),
                   jax.ShapeDtypeStruct((2, t), I32),
                   jax.ShapeDtypeStruct((MOE_EXPERTS, LANES), I32)],
        scratch_shapes=[pltpu.VMEM((MOE_EXPERTS, 1), F32)],
        compiler_params=_params(("arbitrary",), 40),
        name="moe_router",
    )(x2d, g.reshape(1, d), rt, rb)


DISPATCH_BATCH = 256


def _dispatch_body(dest_ref, x_ref, init_ref, xs_ref, sem):
    del init_ref
    t = x_ref.shape[0]
    nb = t // DISPATCH_BATCH

    def row_copy(tok, k):
        d = dest_ref[k * t + tok]
        return pltpu.make_async_copy(x_ref.at[pl.ds(tok, 1)],
                                     xs_ref.at[pl.ds(d, 1)], sem)

    def wait_batch():
        for _ in range(2):
            pltpu.make_async_copy(x_ref.at[pl.ds(0, DISPATCH_BATCH)],
                                  xs_ref.at[pl.ds(0, DISPATCH_BATCH)], sem).wait()

    def batch(bi, _):
        def issue(r, _):
            tok = bi * DISPATCH_BATCH + r
            row_copy(tok, 0).start()
            row_copy(tok, 1).start()
            return 0
        lax.fori_loop(0, DISPATCH_BATCH, issue, 0)

        @pl.when(bi > 0)
        def _():
            wait_batch()
        return 0

    lax.fori_loop(0, nb, batch, 0)
    wait_batch()


def moe_dispatch(dest_flat, x2d, n_rows):
    t, d = x2d.shape
    zeros = jnp.zeros((n_rows, d), x2d.dtype)
    return pl.pallas_call(
        _dispatch_body,
        grid_spec=pltpu.PrefetchScalarGridSpec(
            num_scalar_prefetch=1,
            grid=(1,),
            in_specs=[pl.BlockSpec(memory_space=pl.ANY),
                      pl.BlockSpec(memory_space=pl.ANY)],
            out_specs=pl.BlockSpec(memory_space=pl.ANY),
            scratch_shapes=[pltpu.SemaphoreType.DMA(())],
        ),
        out_shape=jax.ShapeDtypeStruct((n_rows, d), x2d.dtype),
        input_output_aliases={2: 0},
        compiler_params=pltpu.CompilerParams(
            dimension_semantics=("arbitrary",), has_side_effects=True),
        name="moe_dispatch",
    )(dest_flat, x2d, zeros)


def _experts_body(blk_e_ref, n_used_ref, x_ref, g_ref, wg_ref, wu_ref, wd_ref,
                  o_ref, wgb_ref, wub_ref, wdb_ref):
    i = pl.program_id(0)
    used = i < n_used_ref[0]
    prev = blk_e_ref[jnp.maximum(i - 1, 0)]
    fresh = jnp.logical_or(i == 0, blk_e_ref[i] != prev)

    @pl.when(jnp.logical_and(used, fresh))
    def _():
        wgb_ref[...] = wg_ref[0, 0].astype(BF16)
        wub_ref[...] = wu_ref[0, 0].astype(BF16)
        wdb_ref[...] = wd_ref[0, 0].astype(BF16)

    @pl.when(used)
    def _():
        x = x_ref[...]
        inv = lax.rsqrt(jnp.mean(x * x, axis=-1, keepdims=True) + EPS)
        h = (x * inv * g_ref[...]).astype(BF16)
        gate = jnp.dot(h, wgb_ref[...], preferred_element_type=F32)
        up = jnp.dot(h, wub_ref[...], preferred_element_type=F32)
        act = (_silu(gate) * up).astype(BF16)
        o_ref[...] = jnp.dot(act, wdb_ref[...], preferred_element_type=F32)

    @pl.when(jnp.logical_not(used))
    def _():
        o_ref[...] = jnp.zeros_like(o_ref)


def moe_experts(blk_e, n_used, xs, g, w_gate, w_up, w_down, layer):
    n_rows, d = xs.shape
    ff = w_gate.shape[3]
    nblk = n_rows // MOE_BLOCK

    def row_map(i, be, nu):
        return (jnp.minimum(i, nu[0] - 1), 0)

    def w_map(i, be, nu):
        return (layer, be[i], 0, 0)

    return pl.pallas_call(
        _experts_body,
        grid_spec=pltpu.PrefetchScalarGridSpec(
            num_scalar_prefetch=2,
            grid=(nblk,),
            in_specs=[pl.BlockSpec((MOE_BLOCK, d), row_map),
                      pl.BlockSpec((1, d), lambda i, be, nu: (0, 0)),
                      pl.BlockSpec((1, 1, d, ff), w_map),
                      pl.BlockSpec((1, 1, d, ff), w_map),
                      pl.BlockSpec((1, 1, ff, d), w_map)],
            out_specs=pl.BlockSpec((MOE_BLOCK, d), lambda i, be, nu: (i, 0)),
            scratch_shapes=[pltpu.VMEM((d, ff), BF16), pltpu.VMEM((d, ff), BF16),
                            pltpu.VMEM((ff, d), BF16)],
        ),
        out_shape=jax.ShapeDtypeStruct((n_rows, d), F32),
        compiler_params=_params(("arbitrary",), 56),
        name="moe_experts",
    )(blk_e, n_used, xs, g.reshape(1, d), w_gate, w_up, w_down)


def _combine_body(dest_ref, x_ref, wt_ref, g_ref, ys_ref, o_ref, buf_ref, sem,
                  *, final_norm):
    i = pl.program_id(0)
    n = pl.num_programs(0)
    tm = x_ref.shape[0]
    t = tm * n

    def issue(tile, slot):
        def body(r, _):
            for k in range(2):
                d = dest_ref[k * t + tile * tm + r]
                pltpu.make_async_copy(ys_ref.at[pl.ds(d, 1)],
                                      buf_ref.at[slot, k, pl.ds(r, 1)],
                                      sem.at[slot]).start()
            return 0
        lax.fori_loop(0, tm, body, 0)

    @pl.when(i == 0)
    def _():
        issue(0, 0)

    @pl.when(i + 1 < n)
    def _():
        issue(i + 1, (i + 1) % 2)

    slot = i % 2
    for k in range(2):
        pltpu.make_async_copy(ys_ref.at[pl.ds(0, tm)], buf_ref.at[slot, k],
                              sem.at[slot]).wait()
    w = wt_ref[...]
    y = x_ref[...] + w[:, 0:1] * buf_ref[slot, 0] + w[:, 1:2] * buf_ref[slot, 1]
    if final_norm:
        y = y * lax.rsqrt(jnp.mean(y * y, axis=-1, keepdims=True) + EPS) * g_ref[...]
    o_ref[...] = y


def moe_combine(dest_flat, x2d, wt_t, ys, g_final, final_norm, tm=128):
    t, d = x2d.shape
    return pl.pallas_call(
        functools.partial(_combine_body, final_norm=final_norm),
        grid_spec=pltpu.PrefetchScalarGridSpec(
            num_scalar_prefetch=1,
            grid=(t // tm,),
            in_specs=[pl.BlockSpec((tm, d), lambda i, ds: (i, 0)),
                      pl.BlockSpec((tm, 2), lambda i, ds: (i, 0)),
                      pl.BlockSpec((1, d), lambda i, ds: (0, 0)),
                      pl.BlockSpec(memory_space=pl.ANY)],
            out_specs=pl.BlockSpec((tm, d), lambda i, ds: (i, 0)),
            scratch_shapes=[pltpu.VMEM((2, 2, tm, d), F32),
                            pltpu.SemaphoreType.DMA((2,))],
        ),
        out_shape=jax.ShapeDtypeStruct((t, d), F32),
        compiler_params=_params(("arbitrary",), 32),
        name="moe_combine",
    )(dest_flat, x2d, wt_t, g_final.reshape(1, d), ys)


def hier_moe(x2d, g_ffn, router_g, router_g_b, router_e, router_e_b,
             w_gate, w_up, w_down, layer, g_final, final_norm):
    t, d = x2d.shape
    eid, wts, rank, cnt = moe_router(x2d, g_ffn, router_g, router_g_b,
                                     router_e, router_e_b)
    counts = cnt[:, 0]
    padded = (counts + MOE_BLOCK - 1) // MOE_BLOCK * MOE_BLOCK
    pend = jnp.cumsum(padded)
    pstart = pend - padded
    n_rows = 2 * t + MOE_EXPERTS * MOE_BLOCK
    nblk = n_rows // MOE_BLOCK
    dest = (pstart[eid] + rank).reshape(-1)
    blk_start = jnp.arange(nblk, dtype=I32) * MOE_BLOCK
    blk_e = jnp.sum(blk_start[:, None] >= pend[None, :], axis=1).astype(I32)
    blk_e = jnp.minimum(blk_e, MOE_EXPERTS - 1)
    n_used = (pend[-1:] // MOE_BLOCK).astype(I32)
    last_e = blk_e[jnp.maximum(n_used[0] - 1, 0)]
    blk_e = jnp.where(jnp.arange(nblk) < n_used[0], blk_e, last_e)
    xs = moe_dispatch(dest, x2d, n_rows)
    ys = moe_experts(blk_e, n_used, xs, g_ffn, w_gate, w_up, w_down, layer)
    return moe_combine(dest, x2d, wts.T, ys, g_final, final_norm)


def _pad_cols(w, n):
    return jnp.pad(w, ((0, 0), (0, n - w.shape[1])))


def pool_fox_layer(x2d, b, l, g_mix, w_in, b_forget, pool_w, pool_scale, w_out):
    h = rmsnorm(x2d, g_mix, BF16)
    main = POOL_WIDTH + 3 * FOX_WIDTH
    proj = matmul(h, w_in, main, name="pf_in_proj")
    f = matmul(h, _pad_cols(w_in[:, main:], LANES), LANES, out_dtype=F32,
               name="pf_forget_proj")
    f_t = f[:, :FOX_HEADS].reshape(b, l, FOX_HEADS).transpose(0, 2, 1)
    c = forget_cumsum(f_t, b_forget)
    proj3 = proj.reshape(b, l, main)
    y_pool = pool_mixer(proj3, pool_w, pool_scale)
    y_att = fox_attention(proj3, c.reshape(b * FOX_HEADS, 1, l),
                          c.transpose(0, 2, 1))
    y = jnp.concatenate([y_pool, y_att], axis=-1).reshape(b * l, -1)
    return matmul(y, w_out, w_out.shape[1], res=x2d, out_dtype=F32, tm=512,
                  name="pf_out_proj")


def ssd_layer(x2d, b, l, g_mix, w_in, conv_w, conv_b, dt_bias, a_log, d_skip,
              gnorm, w_out):
    h = rmsnorm(x2d, g_mix, BF16)
    heads = dt_bias.shape[0]
    inner = heads * SSD_HEAD_DIM
    main = 2 * inner + 2 * SSD_GROUPS * SSD_STATE
    proj = matmul(h, w_in, main, name="ssd_in_proj")
    raw = matmul(h, _pad_cols(w_in[:, main:], LANES), LANES, out_dtype=F32,
                 name="ssd_dt_proj")
    dt, acs = ssd_dt(raw, jnp.pad(dt_bias, (0, LANES - heads)),
                     jnp.pad(a_log, (0, LANES - heads)))
    hpg = SSD_GROUP_HEADS

    def by_group(v):
        return v[:, :heads].reshape(b, l, SSD_GROUPS, hpg).transpose(0, 2, 1, 3)

    dt_g = by_group(dt)
    acs_g = by_group(acs)
    acs_gt = acs_g.transpose(0, 1, 3, 2)
    d_skip_exp = jnp.repeat(d_skip, SSD_HEAD_DIM).reshape(1, inner)
    y = ssd_core(proj.reshape(b, l, main), conv_w, conv_b, dt_g, acs_g, acs_gt,
                 d_skip_exp, gnorm)
    return matmul(y.reshape(b * l, inner), w_out, w_out.shape[1], res=x2d,
                  out_dtype=F32, tm=512, tn=512, name="ssd_out_proj")


def kernel(x, norm_mix, norm_ffn, norm_final, pf_w_in, pf_b_forget, pf_pool_w, pf_pool_scale, pf_w_out, ssd_w_in, ssd_conv_w, ssd_conv_b, ssd_dt_bias, ssd_a_log, ssd_d_skip, ssd_gnorm, ssd_w_out, moe_router_g, moe_router_g_b, moe_router_e, moe_router_e_b, moe_w_gate, moe_w_up, moe_w_down):
    b, l, d = x.shape
    depth = norm_mix.shape[0]
    x2d = x.reshape(b * l, d)
    for i in range(depth):
        j = i // 2
        if i % 2 == 0:
            x2d = pool_fox_layer(x2d, b, l, norm_mix[i], pf_w_in[j], pf_b_forget[j],
                                 pf_pool_w[j], pf_pool_scale[j], pf_w_out[j])
        else:
            x2d = ssd_layer(x2d, b, l, norm_mix[i], ssd_w_in[j], ssd_conv_w[j],
                            ssd_conv_b[j], ssd_dt_bias[j], ssd_a_log[j],
                            ssd_d_skip[j], ssd_gnorm[j], ssd_w_out[j])
        last = i == depth - 1
        x2d = hier_moe(x2d, norm_ffn[i], moe_router_g[i], moe_router_g_b[i],
                       moe_router_e[i], moe_router_e_b[i], moe_w_gate,
                       moe_w_up, moe_w_down, i, norm_final, last)
    return x2d.reshape(b, l, d)
```

```python
import functools

import jax
import jax.numpy as jnp
from jax import lax
from jax.experimental import pallas as pl
from jax.experimental.pallas import tpu as pltpu

F32 = jnp.float32
BF16 = jnp.bfloat16
I32 = jnp.int32
EPS = 1e-6

POOL_WINDOWS = (2, 4, 8, 16)
POOL_GROUP = 256
POOL_WIDTH = 1024
FOX_HEADS = 8
FOX_HEAD_DIM = 128
FOX_WIDTH = 1024
SSD_HEAD_DIM = 64
SSD_STATE = 128
SSD_GROUPS = 8
SSD_GROUP_HEADS = 8
SSD_GROUP_WIDTH = SSD_GROUP_HEADS * SSD_HEAD_DIM
SSD_CONV = 4
SSD_CHUNK = 128
MOE_GROUPS = 4
MOE_PER_GROUP = 8
MOE_EXPERTS = 32
MOE_BLOCK = 256
LANES = 128
MIB = 1 << 20


def _params(semantics, vmem_mib):
    return pltpu.CompilerParams(dimension_semantics=semantics,
                                vmem_limit_bytes=vmem_mib * MIB)


def _rmsnorm_body(x_ref, g_ref, o_ref):
    x = x_ref[...]
    inv = lax.rsqrt(jnp.mean(x * x, axis=-1, keepdims=True) + EPS)
    o_ref[...] = (x * inv * g_ref[...]).astype(o_ref.dtype)


def rmsnorm(x2d, g, out_dtype, tm=512):
    t, d = x2d.shape
    return pl.pallas_call(
        _rmsnorm_body,
        grid=(t // tm,),
        in_specs=[pl.BlockSpec((tm, d), lambda i: (i, 0)),
                  pl.BlockSpec((1, d), lambda i: (0, 0))],
        out_specs=pl.BlockSpec((tm, d), lambda i: (i, 0)),
        out_shape=jax.ShapeDtypeStruct((t, d), out_dtype),
        compiler_params=_params(("arbitrary",), 40),
        name="rmsnorm",
    )(x2d, g.reshape(1, d))


def _matmul_body(*refs, has_res):
    if has_res:
        a_ref, w_ref, r_ref, o_ref, wb_ref = refs
    else:
        a_ref, w_ref, o_ref, wb_ref = refs

    @pl.when(pl.program_id(1) == 0)
    def _():
        wb_ref[...] = w_ref[...].astype(BF16)

    acc = jnp.dot(a_ref[...], wb_ref[...], preferred_element_type=F32)
    if has_res:
        acc = acc + r_ref[...]
    o_ref[...] = acc.astype(o_ref.dtype)


def matmul(a, w, n_out, *, col0=0, res=None, out_dtype=BF16, tm=1024, tn=1024,
           vmem_mib=48, name="matmul"):
    t, k = a.shape
    tn = min(tn, n_out)
    tm = min(tm, t)
    assert t % tm == 0 and n_out % tn == 0 and col0 % tn == 0
    cb0 = col0 // tn
    in_specs = [pl.BlockSpec((tm, k), lambda j, i: (i, 0)),
                pl.BlockSpec((k, tn), lambda j, i: (0, j + cb0))]
    args = [a, w]
    if res is not None:
        in_specs.append(pl.BlockSpec((tm, tn), lambda j, i: (i, j)))
        args.append(res)
    return pl.pallas_call(
        functools.partial(_matmul_body, has_res=res is not None),
        grid=(n_out // tn, t // tm),
        in_specs=in_specs,
        out_specs=pl.BlockSpec((tm, tn), lambda j, i: (i, j)),
        out_shape=jax.ShapeDtypeStruct((t, n_out), out_dtype),
        scratch_shapes=[pltpu.VMEM((k, tn), BF16)],
        compiler_params=_params(("arbitrary", "arbitrary"), vmem_mib),
        name=name,
    )(*args)


def _forget_cumsum_body(f_ref, b_ref, c_ref):
    z = f_ref[0] + b_ref[...]
    x = jnp.minimum(z, 0.0) - jnp.log1p(jnp.exp(-jnp.abs(z)))
    n = x.shape[1]
    lane = lax.broadcasted_iota(I32, x.shape, 1)
    shift = 1
    while shift < n:
        x = x + jnp.where(lane >= shift, pltpu.roll(x, shift, axis=1), 0.0)
        shift *= 2
    c_ref[0] = x


def forget_cumsum(f_t, b_forget):
    b, h, l = f_t.shape
    return pl.pallas_call(
        _forget_cumsum_body,
        grid=(b,),
        in_specs=[pl.BlockSpec((1, h, l), lambda i: (i, 0, 0)),
                  pl.BlockSpec((h, 1), lambda i: (0, 0))],
        out_specs=pl.BlockSpec((1, h, l), lambda i: (i, 0, 0)),
        out_shape=jax.ShapeDtypeStruct((b, h, l), F32),
        compiler_params=_params(("arbitrary",), 16),
        name="forget_cumsum",
    )(f_t, b_forget.reshape(h, 1))


def _pool_body(u_ref, w_ref, s_ref, o_ref):
    g = pl.program_id(1)
    u = u_ref[0].astype(F32)
    row = lax.broadcasted_iota(I32, u.shape, 0)
    acc = u
    sums = []
    for shift in (1, 2, 4, 8):
        acc = acc + jnp.where(row >= shift, pltpu.roll(acc, shift, axis=0), 0.0)
        sums.append(acc)
    win_sum = jnp.where(g == 0, sums[0],
                        jnp.where(g == 1, sums[1],
                                  jnp.where(g == 2, sums[2], sums[3])))
    window = jnp.left_shift(jnp.int32(2), g)
    count = jnp.minimum(row + 1, window).astype(F32)
    mixed = win_sum / count - u
    y = jnp.dot(mixed.astype(BF16), w_ref[0].astype(BF16),
                preferred_element_type=F32)
    o_ref[0] = (y * s_ref[...]).astype(o_ref.dtype)


def pool_mixer(proj3, pool_w, pool_scale):
    b, l, _ = proj3.shape
    ng = len(POOL_WINDOWS)
    return pl.pallas_call(
        _pool_body,
        grid=(b, ng),
        in_specs=[pl.BlockSpec((1, l, POOL_GROUP), lambda i, g: (i, 0, g)),
                  pl.BlockSpec((1, POOL_GROUP, POOL_GROUP), lambda i, g: (g, 0, 0)),
                  pl.BlockSpec((1, POOL_GROUP), lambda i, g: (0, g))],
        out_specs=pl.BlockSpec((1, l, POOL_GROUP), lambda i, g: (i, 0, g)),
        out_shape=jax.ShapeDtypeStruct((b, l, POOL_WIDTH), BF16),
        compiler_params=_params(("arbitrary", "arbitrary"), 40),
        name="pool_mixer",
    )(proj3, pool_w, pool_scale.reshape(1, POOL_WIDTH))


def _fox_body(q_ref, k_ref, v_ref, cq_ref, ck_ref, o_ref, *, tile, scale):
    h = pl.program_id(1)
    qi = pl.program_id(2)
    q = q_ref[0]
    cq8 = cq_ref[0]
    head_lane = lax.broadcasted_iota(I32, cq8.shape, 1)
    cq = jnp.sum(jnp.where(head_lane == h, cq8, 0.0), axis=1, keepdims=True)

    def step(j, carry, masked):
        m, l, acc = carry
        start = pl.multiple_of(j * tile, tile)
        k = k_ref[0, pl.ds(start, tile), :]
        v = v_ref[0, pl.ds(start, tile), :]
        ck = ck_ref[0, :, pl.ds(start, tile)]
        s = lax.dot_general(q, k, (((1,), (1,)), ((), ())),
                            preferred_element_type=F32)
        s = s * scale + cq - ck
        if masked:
            r = lax.broadcasted_iota(I32, s.shape, 0)
            c = lax.broadcasted_iota(I32, s.shape, 1)
            s = jnp.where(c <= r, s, -jnp.inf)
        m_new = jnp.maximum(m, jnp.max(s, axis=1, keepdims=True))
        alpha = jnp.exp(m - m_new)
        p = jnp.exp(s - m_new)
        l_new = alpha * l + jnp.sum(p, axis=1, keepdims=True)
        acc_new = alpha * acc + jnp.dot(p.astype(BF16), v,
                                        preferred_element_type=F32)
        return m_new, l_new, acc_new

    init = (jnp.full((tile, 1), -jnp.inf, F32), jnp.zeros((tile, 1), F32),
            jnp.zeros((tile, q.shape[1]), F32))
    carry = step(qi, init, True)
    carry = lax.fori_loop(0, qi, lambda j, c: step(j, c, False), carry)
    _, l, acc = carry
    o_ref[0] = (acc / l).astype(o_ref.dtype)


def fox_attention(proj3, c_row, c_col, tile=256):
    b, l, _ = proj3.shape
    hd = FOX_HEAD_DIM
    q0 = POOL_WIDTH // hd
    k0 = q0 + FOX_HEADS
    v0 = k0 + FOX_HEADS
    return pl.pallas_call(
        functools.partial(_fox_body, tile=tile, scale=hd ** -0.5),
        grid=(b, FOX_HEADS, l // tile),
        in_specs=[
            pl.BlockSpec((1, tile, hd), lambda i, h, q: (i, q, q0 + h)),
            pl.BlockSpec((1, l, hd), lambda i, h, q: (i, 0, k0 + h)),
            pl.BlockSpec((1, l, hd), lambda i, h, q: (i, 0, v0 + h)),
            pl.BlockSpec((1, tile, FOX_HEADS), lambda i, h, q: (i, q, 0)),
            pl.BlockSpec((1, 1, l), lambda i, h, q: (i * FOX_HEADS + h, 0, 0)),
        ],
        out_specs=pl.BlockSpec((1, tile, hd), lambda i, h, q: (i, q, h)),
        out_shape=jax.ShapeDtypeStruct((b, l, FOX_WIDTH), BF16),
        compiler_params=_params(("arbitrary", "arbitrary", "arbitrary"), 32),
        name="fox_attention",
    )(proj3, proj3, proj3, c_col, c_row)


def _ssd_dt_body(raw_ref, bias_ref, alog_ref, dt_ref, acs_ref):
    z = raw_ref[...] + bias_ref[...]
    dt = jnp.maximum(z, 0.0) + jnp.log1p(jnp.exp(-jnp.abs(z)))
    a_dt = dt * (-jnp.exp(alog_ref[...]))
    n = z.shape[0]
    r = lax.broadcasted_iota(I32, (n, n), 0)
    c = lax.broadcasted_iota(I32, (n, n), 1)
    tri = (c <= r).astype(F32)
    dt_ref[...] = dt
    acs_ref[...] = jnp.dot(tri, a_dt, preferred_element_type=F32,
                           precision=lax.Precision.HIGHEST)


def ssd_dt(raw, bias_pad, alog_pad):
    t, n = raw.shape
    spec = pl.BlockSpec((SSD_CHUNK, n), lambda i: (i, 0))
    vec = pl.BlockSpec((1, n), lambda i: (0, 0))
    return pl.pallas_call(
        _ssd_dt_body,
        grid=(t // SSD_CHUNK,),
        in_specs=[spec, vec, vec],
        out_specs=[spec, spec],
        out_shape=[jax.ShapeDtypeStruct((t, n), F32)] * 2,
        compiler_params=_params(("arbitrary",), 16),
        name="ssd_dt",
    )(raw, bias_pad.reshape(1, n), alog_pad.reshape(1, n))


def _silu(x):
    return x / (1.0 + jnp.exp(-x))


def _causal_conv_silu(u_ref, w_ref, b_ref):
    u = u_ref[0].astype(F32)
    row = lax.broadcasted_iota(I32, u.shape, 0)
    w = w_ref[...]
    out = b_ref[...] + u * w[SSD_CONV - 1:SSD_CONV, :]
    for shift in range(1, SSD_CONV):
        prev = jnp.where(row >= shift, pltpu.roll(u, shift, axis=0), 0.0)
        out = out + prev * w[SSD_CONV - 1 - shift:SSD_CONV - shift, :]
    return _silu(out)


def _pair_lanes(cols, j):
    rows = cols.shape[0]
    lane = lax.broadcasted_iota(I32, (rows, LANES), 1)
    lo = jnp.broadcast_to(cols[:, 2 * j:2 * j + 1], (rows, LANES))
    hi = jnp.broadcast_to(cols[:, 2 * j + 1:2 * j + 2], (rows, LANES))
    return jnp.where(lane < SSD_HEAD_DIM, lo, hi)


def _ssd_body(z_ref, x_ref, b_ref, c_ref, wx_ref, wb_ref, wc_ref,
              bx_ref, bb_ref, bc_ref, dt_ref, acs_ref, acst_ref,
              dskip_ref, gn_ref, o_ref, xs_ref, bs_ref, cs_ref, st_ref):
    q = SSD_CHUNK
    xs_ref[...] = _causal_conv_silu(x_ref, wx_ref, bx_ref)
    bs_ref[...] = _causal_conv_silu(b_ref, wb_ref, bb_ref).astype(BF16)
    cs_ref[...] = _causal_conv_silu(c_ref, wc_ref, bc_ref).astype(BF16)
    st_ref[...] = jnp.zeros_like(st_ref)

    r = lax.broadcasted_iota(I32, (q, q), 0)
    c = lax.broadcasted_iota(I32, (q, q), 1)
    causal = c <= r
    lane = lax.broadcasted_iota(I32, (q, LANES), 1)
    first_head = lane < SSD_HEAD_DIM

    def chunk(ci, _):
        r0 = pl.multiple_of(ci * q, q)
        xc = xs_ref[pl.ds(r0, q), :]
        bm = bs_ref[pl.ds(r0, q), :]
        cm = cs_ref[pl.ds(r0, q), :]
        zc = z_ref[0, pl.ds(r0, q), :].astype(F32)
        dt = dt_ref[0, 0, pl.ds(r0, q), :]
        acs = acs_ref[0, 0, pl.ds(r0, q), :]
        acs_t = acst_ref[0, 0, :, pl.ds(r0, q)]
        acs_last = acs[q - 1:q, :]
        exp_acs = jnp.exp(acs)
        decay_to_end = jnp.exp(acs_last - acs)
        chunk_decay = jnp.exp(acs_last)
        cb = lax.dot_general(cm, bm, (((1,), (1,)), ((), ())),
                             preferred_element_type=F32)
        bm_t = bm.astype(F32).T.astype(BF16)
        gated = []
        for j in range(SSD_GROUP_HEADS // 2):
            lanes = slice(j * LANES, (j + 1) * LANES)
            xp = xc[:, lanes]
            x_dt = xp * _pair_lanes(dt, j)
            x_dt_b = x_dt.astype(BF16)
            y_heads = []
            for hh in range(2):
                hd = 2 * j + hh
                seg = jnp.broadcast_to(acs[:, hd:hd + 1], (q, q)) - acs_t[hd:hd + 1, :]
                decay = jnp.exp(jnp.where(causal, seg, -jnp.inf))
                y_heads.append(jnp.dot((cb * decay).astype(BF16), x_dt_b,
                                       preferred_element_type=F32))
            y = jnp.where(first_head, y_heads[0], y_heads[1])
            state = st_ref[:, lanes]
            y = y + jnp.dot(cm, state.astype(BF16),
                            preferred_element_type=F32) * _pair_lanes(exp_acs, j)
            y = y + xp * dskip_ref[:, lanes]
            x_end = (x_dt * _pair_lanes(decay_to_end, j)).astype(BF16)
            st_ref[:, lanes] = state * _pair_lanes(chunk_decay, j) + jnp.dot(
                bm_t, x_end, preferred_element_type=F32)
            gated.append(y * _silu(zc[:, lanes]))
        gated = jnp.concatenate(gated, axis=1)
        inv = lax.rsqrt(jnp.mean(gated * gated, axis=-1, keepdims=True) + EPS)
        o_ref[0, pl.ds(r0, q), :] = (gated * inv * gn_ref[...]).astype(o_ref.dtype)
        return 0

    lax.fori_loop(0, x_ref.shape[1] // q, chunk, 0)


def ssd_core(proj3, conv_w, conv_b, dt_g, acs_g, acs_gt, d_skip_exp, gnorm):
    b, l, _ = proj3.shape
    gw = SSD_GROUP_WIDTH
    n = SSD_STATE
    inner = SSD_GROUPS * gw
    x_blk0 = inner // gw
    bm_blk0 = 2 * inner // n
    cm_blk0 = bm_blk0 + SSD_GROUPS
    wb_blk0 = inner // n
    wc_blk0 = wb_blk0 + SSD_GROUPS
    hpg = SSD_GROUP_HEADS
    conv_b2 = conv_b.reshape(1, -1)
    return pl.pallas_call(
        _ssd_body,
        grid=(b, SSD_GROUPS),
        in_specs=[
            pl.BlockSpec((1, l, gw), lambda i, g: (i, 0, g)),
            pl.BlockSpec((1, l, gw), lambda i, g: (i, 0, x_blk0 + g)),
            pl.BlockSpec((1, l, n), lambda i, g: (i, 0, bm_blk0 + g)),
            pl.BlockSpec((1, l, n), lambda i, g: (i, 0, cm_blk0 + g)),
            pl.BlockSpec((SSD_CONV, gw), lambda i, g: (0, g)),
            pl.BlockSpec((SSD_CONV, n), lambda i, g: (0, wb_blk0 + g)),
            pl.BlockSpec((SSD_CONV, n), lambda i, g: (0, wc_blk0 + g)),
            pl.BlockSpec((1, gw), lambda i, g: (0, g)),
            pl.BlockSpec((1, n), lambda i, g: (0, wb_blk0 + g)),
            pl.BlockSpec((1, n), lambda i, g: (0, wc_blk0 + g)),
            pl.BlockSpec((1, 1, l, hpg), lambda i, g: (i, g, 0, 0)),
            pl.BlockSpec((1, 1, l, hpg), lambda i, g: (i, g, 0, 0)),
            pl.BlockSpec((1, 1, hpg, l), lambda i, g: (i, g, 0, 0)),
            pl.BlockSpec((1, gw), lambda i, g: (0, g)),
            pl.BlockSpec((1, gw), lambda i, g: (0, g)),
        ],
        out_specs=pl.BlockSpec((1, l, gw), lambda i, g: (i, 0, g)),
        out_shape=jax.ShapeDtypeStruct((b, l, inner), BF16),
        scratch_shapes=[pltpu.VMEM((l, gw), F32), pltpu.VMEM((l, n), BF16),
                        pltpu.VMEM((l, n), BF16), pltpu.VMEM((n, gw), F32)],
        compiler_params=_params(("arbitrary", "arbitrary"), 48),
        name="ssd_core",
    )(proj3, proj3, proj3, proj3, conv_w, conv_w, conv_w, conv_b2, conv_b2,
      conv_b2, dt_g, acs_g, acs_gt, d_skip_exp, gnorm.reshape(1, inner))


def _first_argmax(vals, nrows):
    row = lax.broadcasted_iota(I32, vals.shape, 0)
    top = jnp.max(vals, axis=0, keepdims=True)
    idx = jnp.min(jnp.where(vals == top, row, nrows), axis=0, keepdims=True)
    return top, idx, row


def _router_body(x_ref, g_ref, rt_ref, rb_ref, eid_ref, wt_ref, rank_ref,
                 cnt_ref, carry_ref):
    step = pl.program_id(0)

    @pl.when(step == 0)
    def _():
        carry_ref[...] = jnp.zeros_like(carry_ref)

    x = x_ref[...]
    tm = x.shape[0]
    h = x * lax.rsqrt(jnp.mean(x * x, axis=-1, keepdims=True) + EPS) * g_ref[...]
    logits = lax.dot_general(rt_ref[...], h, (((1,), (1,)), ((), ())),
                             preferred_element_type=F32,
                             precision=lax.Precision.HIGHEST) + rb_ref[...]
    e_logits = logits[:MOE_EXPERTS]
    g_logits = logits[MOE_EXPERTS:MOE_EXPERTS + MOE_GROUPS]
    g_max, g_sel, _ = _first_argmax(g_logits, MOE_GROUPS)
    g_w = 1.0 / jnp.sum(jnp.exp(g_logits - g_max), axis=0, keepdims=True)
    sel = jnp.zeros((MOE_PER_GROUP, tm), F32)
    for grp in range(MOE_GROUPS):
        sel = jnp.where(g_sel == grp,
                        e_logits[grp * MOE_PER_GROUP:(grp + 1) * MOE_PER_GROUP], sel)
    m1, i1, row8 = _first_argmax(sel, MOE_PER_GROUP)
    rest = jnp.where(row8 == i1, -jnp.inf, sel)
    m2, i2, _ = _first_argmax(rest, MOE_PER_GROUP)
    p2 = jnp.exp(m2 - m1)
    w1 = g_w / (1.0 + p2)
    w2 = g_w * p2 / (1.0 + p2)
    e1 = g_sel * MOE_PER_GROUP + i1
    e2 = g_sel * MOE_PER_GROUP + i2

    r = lax.broadcasted_iota(I32, (tm, tm), 0)
    c = lax.broadcasted_iota(I32, (tm, tm), 1)
    upper = jnp.where(r <= c, 1.0, 0.0).astype(BF16)
    row32 = lax.broadcasted_iota(I32, (MOE_EXPERTS, tm), 0)
    hit1 = row32 == e1
    hit2 = row32 == e2
    cum1 = jnp.dot(jnp.where(hit1, 1.0, 0.0).astype(BF16), upper,
                   preferred_element_type=F32)
    cum2 = jnp.dot(jnp.where(hit2, 1.0, 0.0).astype(BF16), upper,
                   preferred_element_type=F32)
    carry = carry_ref[...]
    tot1 = cum1[:, tm - 1:tm]
    tot2 = cum2[:, tm - 1:tm]
    rank1 = jnp.sum(jnp.where(hit1, carry + cum1 - 1.0, 0.0), axis=0, keepdims=True)
    rank2 = jnp.sum(jnp.where(hit2, carry + tot1 + cum2 - 1.0, 0.0), axis=0,
                    keepdims=True)
    new_carry = carry + tot1 + tot2
    carry_ref[...] = new_carry

    eid_ref[...] = jnp.concatenate([e1, e2], axis=0)
    wt_ref[...] = jnp.concatenate([w1, w2], axis=0)
    rank_ref[...] = jnp.concatenate([rank1, rank2], axis=0).astype(I32)
    cnt_ref[...] = jnp.broadcast_to(new_carry, cnt_ref.shape).astype(I32)


def moe_router(x2d, g, router_g, router_g_b, router_e, router_e_b, tm=512):
    t, d = x2d.shape
    pad = 8 - MOE_GROUPS
    rt = jnp.concatenate([router_e.T, router_g.T, jnp.zeros((pad, d), F32)], axis=0)
    rb = jnp.concatenate([router_e_b, router_g_b, jnp.zeros((pad,), F32)]).reshape(-1, 1)
    nr = rt.shape[0]
    tok = pl.BlockSpec((2, tm), lambda i: (0, i))
    return pl.pallas_call(
        _router_body,
        grid=(t // tm,),
        in_specs=[pl.BlockSpec((tm, d), lambda i: (i, 0)),
                  pl.BlockSpec((1, d), lambda i: (0, 0)),
                  pl.BlockSpec((nr, d), lambda i: (0, 0)),
                  pl.BlockSpec((nr, 1), lambda i: (0, 0))],
        out_specs=[tok, tok, tok,
                   pl.BlockSpec((MOE_EXPERTS, LANES), lambda i: (0, 0))],
        out_shape=[jax.ShapeDtypeStruct((2, t), I32),
                   jax.ShapeDtypeStruct((2, t), F32),
                   jax.ShapeDtypeStruct((2, t), I32),
                   jax.ShapeDtypeStruct((MOE_EXPERTS, LANES), I32)],
        scratch_shapes=[pltpu.VMEM((MOE_EXPERTS, 1), F32)],
        compiler_params=_params(("arbitrary",), 40),
        name="moe_router",
    )(x2d, g.reshape(1, d), rt, rb)


def _row_tokens_body(dest_ref, tok_ref):
    n_rows = tok_ref.shape[0]
    t = dest_ref.shape[0] // 2

    def clear(r, _):
        tok_ref[r] = 0
        return 0

    lax.fori_loop(0, n_rows, clear, 0, unroll=8)

    def fill(tok, _):
        tok_ref[dest_ref[tok]] = tok
        tok_ref[dest_ref[t + tok]] = tok
        return 0

    lax.fori_loop(0, t, fill, 0, unroll=8)


def moe_row_tokens(dest_flat, n_rows):
    return pl.pallas_call(
        _row_tokens_body,
        grid_spec=pltpu.PrefetchScalarGridSpec(
            num_scalar_prefetch=1,
            grid=(1,),
            in_specs=[],
            out_specs=pl.BlockSpec(memory_space=pltpu.SMEM),
        ),
        out_shape=jax.ShapeDtypeStruct((n_rows,), I32),
        compiler_params=pltpu.CompilerParams(dimension_semantics=("arbitrary",)),
        name="moe_row_tokens",
    )(dest_flat)


def _experts_body(blk_e_ref, n_used_ref, tok_ref, x_hbm, g_ref, wg_ref, wu_ref,
                  wd_ref, o_ref, xbuf_ref, sem, wgb_ref, wub_ref, wdb_ref):
    i = pl.program_id(0)
    n_used = n_used_ref[0]
    used = i < n_used
    prev = blk_e_ref[jnp.maximum(i - 1, 0)]
    fresh = jnp.logical_or(i == 0, blk_e_ref[i] != prev)
    rows = xbuf_ref.shape[1]

    def issue(blk, slot):
        def body(r, _):
            tok = tok_ref[blk * rows + r]
            pltpu.make_async_copy(x_hbm.at[pl.ds(tok, 1)],
                                  xbuf_ref.at[slot, pl.ds(r, 1)],
                                  sem.at[slot]).start()
            return 0
        lax.fori_loop(0, rows, body, 0, unroll=8)

    @pl.when(i == 0)
    def _():
        issue(0, 0)

    @pl.when(i + 1 < n_used)
    def _():
        issue(i + 1, (i + 1) % 2)

    @pl.when(jnp.logical_and(used, fresh))
    def _():
        wgb_ref[...] = wg_ref[0, 0].astype(BF16)
        wub_ref[...] = wu_ref[0, 0].astype(BF16)
        wdb_ref[...] = wd_ref[0, 0].astype(BF16)

    @pl.when(used)
    def _():
        slot = i % 2
        pltpu.make_async_copy(x_hbm.at[pl.ds(0, rows)], xbuf_ref.at[slot],
                              sem.at[slot]).wait()
        x = xbuf_ref[slot]
        inv = lax.rsqrt(jnp.mean(x * x, axis=-1, keepdims=True) + EPS)
        h = (x * inv * g_ref[...]).astype(BF16)
        gate = jnp.dot(h, wgb_ref[...], preferred_element_type=F32)
        up = jnp.dot(h, wub_ref[...], preferred_element_type=F32)
        act = (_silu(gate) * up).astype(BF16)
        o_ref[...] = jnp.dot(act, wdb_ref[...], preferred_element_type=F32)

    @pl.when(jnp.logical_not(used))
    def _():
        o_ref[...] = jnp.zeros_like(o_ref)


def moe_experts(blk_e, n_used, row_tok, x2d, g, w_gate, w_up, w_down, layer):
    n_rows = row_tok.shape[0]
    d = x2d.shape[1]
    ff = w_gate.shape[3]
    nblk = n_rows // MOE_BLOCK

    def w_map(i, be, nu, rt):
        return (layer, be[i], 0, 0)

    return pl.pallas_call(
        _experts_body,
        grid_spec=pltpu.PrefetchScalarGridSpec(
            num_scalar_prefetch=3,
            grid=(nblk,),
            in_specs=[pl.BlockSpec(memory_space=pl.ANY),
                      pl.BlockSpec((1, d), lambda i, be, nu, rt: (0, 0)),
                      pl.BlockSpec((1, 1, d, ff), w_map),
                      pl.BlockSpec((1, 1, d, ff), w_map),
                      pl.BlockSpec((1, 1, ff, d), w_map)],
            out_specs=pl.BlockSpec((MOE_BLOCK, d), lambda i, be, nu, rt: (i, 0)),
            scratch_shapes=[pltpu.VMEM((2, MOE_BLOCK, d), F32),
                            pltpu.SemaphoreType.DMA((2,)),
                            pltpu.VMEM((d, ff), BF16), pltpu.VMEM((d, ff), BF16),
                            pltpu.VMEM((ff, d), BF16)],
        ),
        out_shape=jax.ShapeDtypeStruct((n_rows, d), F32),
        compiler_params=_params(("arbitrary",), 56),
        name="moe_experts",
    )(blk_e, n_used, row_tok, x2d, g.reshape(1, d), w_gate, w_up, w_down)


def _combine_body(dest_ref, x_ref, wt_ref, g_ref, ys_ref, o_ref, buf_ref, sem,
                  *, final_norm):
    i = pl.program_id(0)
    n = pl.num_programs(0)
    tm = x_ref.shape[0]
    t = tm * n

    def issue(tile, slot):
        def body(r, _):
            for k in range(2):
                d = dest_ref[k * t + tile * tm + r]
                pltpu.make_async_copy(ys_ref.at[pl.ds(d, 1)],
                                      buf_ref.at[slot, k, pl.ds(r, 1)],
                                      sem.at[slot]).start()
            return 0
        lax.fori_loop(0, tm, body, 0, unroll=8)

    @pl.when(i == 0)
    def _():
        issue(0, 0)

    @pl.when(i + 1 < n)
    def _():
        issue(i + 1, (i + 1) % 2)

    slot = i % 2
    for k in range(2):
        pltpu.make_async_copy(ys_ref.at[pl.ds(0, tm)], buf_ref.at[slot, k],
                              sem.at[slot]).wait()
    w = wt_ref[...]
    y = x_ref[...] + w[:, 0:1] * buf_ref[slot, 0] + w[:, 1:2] * buf_ref[slot, 1]
    if final_norm:
        y = y * lax.rsqrt(jnp.mean(y * y, axis=-1, keepdims=True) + EPS) * g_ref[...]
    o_ref[...] = y


def moe_combine(dest_flat, x2d, wt_t, ys, g_final, final_norm, tm=128):
    t, d = x2d.shape
    return pl.pallas_call(
        functools.partial(_combine_body, final_norm=final_norm),
        grid_spec=pltpu.PrefetchScalarGridSpec(
            num_scalar_prefetch=1,
            grid=(t // tm,),
            in_specs=[pl.BlockSpec((tm, d), lambda i, ds: (i, 0)),
                      pl.BlockSpec((tm, 2), lambda i, ds: (i, 0)),
                      pl.BlockSpec((1, d), lambda i, ds: (0, 0)),
                      pl.BlockSpec(memory_space=pl.ANY)],
            out_specs=pl.BlockSpec((tm, d), lambda i, ds: (i, 0)),
            scratch_shapes=[pltpu.VMEM((2, 2, tm, d), F32),
                            pltpu.SemaphoreType.DMA((2,))],
        ),
        out_shape=jax.ShapeDtypeStruct((t, d), F32),
        compiler_params=_params(("arbitrary",), 32),
        name="moe_combine",
    )(dest_flat, x2d, wt_t, g_final.reshape(1, d), ys)


def hier_moe(x2d, g_ffn, router_g, router_g_b, router_e, router_e_b,
             w_gate, w_up, w_down, layer, g_final, final_norm):
    t, d = x2d.shape
    eid, wts, rank, cnt = moe_router(x2d, g_ffn, router_g, router_g_b,
                                     router_e, router_e_b)
    counts = cnt[:, 0]
    padded = (counts + MOE_BLOCK - 1) // MOE_BLOCK * MOE_BLOCK
    pend = jnp.cumsum(padded)
    pstart = pend - padded
    n_rows = 2 * t + MOE_EXPERTS * MOE_BLOCK
    nblk = n_rows // MOE_BLOCK
    expert_ids = jnp.arange(MOE_EXPERTS, dtype=I32)
    start_of = jnp.sum(jnp.where(eid[..., None] == expert_ids, pstart, 0), axis=-1)
    dest = (start_of + rank).reshape(-1)
    blk_start = jnp.arange(nblk, dtype=I32) * MOE_BLOCK
    blk_e = jnp.sum(blk_start[:, None] >= pend[None, :], axis=1).astype(I32)
    blk_e = jnp.minimum(blk_e, MOE_EXPERTS - 1)
    n_used = (pend[-1:] // MOE_BLOCK).astype(I32)
    last_e = blk_e[jnp.maximum(n_used[0] - 1, 0)]
    blk_e = jnp.where(jnp.arange(nblk) < n_used[0], blk_e, last_e)
    row_tok = moe_row_tokens(dest, n_rows)
    ys = moe_experts(blk_e, n_used, row_tok, x2d, g_ffn, w_gate, w_up, w_down,
                     layer)
    return moe_combine(dest, x2d, wts.T, ys, g_final, final_norm)


def _pad_cols(w, n):
    return jnp.pad(w, ((0, 0), (0, n - w.shape[1])))


def pool_fox_layer(x2d, b, l, g_mix, w_in, b_forget, pool_w, pool_scale, w_out):
    h = rmsnorm(x2d, g_mix, BF16)
    main = POOL_WIDTH + 3 * FOX_WIDTH
    proj = matmul(h, w_in, main, name="pf_in_proj")
    f = matmul(h, _pad_cols(w_in[:, main:], LANES), LANES, out_dtype=F32,
               name="pf_forget_proj")
    f_t = f[:, :FOX_HEADS].reshape(b, l, FOX_HEADS).transpose(0, 2, 1)
    c = forget_cumsum(f_t, b_forget)
    proj3 = proj.reshape(b, l, main)
    y_pool = pool_mixer(proj3, pool_w, pool_scale)
    y_att = fox_attention(proj3, c.reshape(b * FOX_HEADS, 1, l),
                          c.transpose(0, 2, 1))
    y = jnp.concatenate([y_pool, y_att], axis=-1).reshape(b * l, -1)
    return matmul(y, w_out, w_out.shape[1], res=x2d, out_dtype=F32, tm=512,
                  name="pf_out_proj")


def ssd_layer(x2d, b, l, g_mix, w_in, conv_w, conv_b, dt_bias, a_log, d_skip,
              gnorm, w_out):
    h = rmsnorm(x2d, g_mix, BF16)
    heads = dt_bias.shape[0]
    inner = heads * SSD_HEAD_DIM
    main = 2 * inner + 2 * SSD_GROUPS * SSD_STATE
    proj = matmul(h, w_in, main, name="ssd_in_proj")
    raw = matmul(h, _pad_cols(w_in[:, main:], LANES), LANES, out_dtype=F32,
                 name="ssd_dt_proj")
    dt, acs = ssd_dt(raw, jnp.pad(dt_bias, (0, LANES - heads)),
                     jnp.pad(a_log, (0, LANES - heads)))
    hpg = SSD_GROUP_HEADS

    def by_group(v):
        return v[:, :heads].reshape(b, l, SSD_GROUPS, hpg).transpose(0, 2, 1, 3)

    dt_g = by_group(dt)
    acs_g = by_group(acs)
    acs_gt = acs_g.transpose(0, 1, 3, 2)
    d_skip_exp = jnp.repeat(d_skip, SSD_HEAD_DIM).reshape(1, inner)
    y = ssd_core(proj.reshape(b, l, main), conv_w, conv_b, dt_g, acs_g, acs_gt,
                 d_skip_exp, gnorm)
    return matmul(y.reshape(b * l, inner), w_out, w_out.shape[1], res=x2d,
                  out_dtype=F32, tm=512, tn=512, name="ssd_out_proj")


def kernel(x, norm_mix, norm_ffn, norm_final, pf_w_in, pf_b_forget, pf_pool_w, pf_pool_scale, pf_w_out, ssd_w_in, ssd_conv_w, ssd_conv_b, ssd_dt_bias, ssd_a_log, ssd_d_skip, ssd_gnorm, ssd_w_out, moe_router_g, moe_router_g_b, moe_router_e, moe_router_e_b, moe_w_gate, moe_w_up, moe_w_down):
    b, l, d = x.shape
    depth = norm_mix.shape[0]
    x2d = x.reshape(b * l, d)
    for i in range(depth):
        j = i // 2
        if i % 2 == 0:
            x2d = pool_fox_layer(x2d, b, l, norm_mix[i], pf_w_in[j], pf_b_forget[j],
                                 pf_pool_w[j], pf_pool_scale[j], pf_w_out[j])
        else:
            x2d = ssd_layer(x2d, b, l, norm_mix[i], ssd_w_in[j], ssd_conv_w[j],
                            ssd_conv_b[j], ssd_dt_bias[j], ssd_a_log[j],
                            ssd_d_skip[j], ssd_gnorm[j], ssd_w_out[j])
        last = i == depth - 1
        x2d = hier_moe(x2d, norm_ffn[i], moe_router_g[i], moe_router_g_b[i],
                       moe_router_e[i], moe_router_e_b[i], moe_w_gate,
                       moe_w_up, moe_w_down, i, norm_final, last)
    return x2d.reshape(b, l, d)
```

```python
import functools

import jax
import jax.numpy as jnp
from jax import lax
from jax.experimental import pallas as pl
from jax.experimental.pallas import tpu as pltpu

F32 = jnp.float32
BF16 = jnp.bfloat16
I32 = jnp.int32
U32 = jnp.uint32
EPS = 1e-6

POOL_WINDOWS = (2, 4, 8, 16)
POOL_GROUP = 256
POOL_WIDTH = 1024
FOX_HEADS = 8
FOX_HEAD_DIM = 128
FOX_WIDTH = 1024
SSD_HEAD_DIM = 64
SSD_STATE = 128
SSD_GROUPS = 8
SSD_GROUP_HEADS = 8
SSD_GROUP_WIDTH = SSD_GROUP_HEADS * SSD_HEAD_DIM
SSD_CONV = 4
SSD_CHUNK = 128
MOE_GROUPS = 4
MOE_PER_GROUP = 8
MOE_EXPERTS = 32
MOE_BLOCK = 256
LANES = 128
MIB = 1 << 20


def _params(semantics, vmem_mib):
    return pltpu.CompilerParams(dimension_semantics=semantics,
                                vmem_limit_bytes=vmem_mib * MIB)


def _rmsnorm_body(x_ref, g_ref, o_ref):
    x = x_ref[...]
    inv = lax.rsqrt(jnp.mean(x * x, axis=-1, keepdims=True) + EPS)
    o_ref[...] = (x * inv * g_ref[...]).astype(o_ref.dtype)


def rmsnorm(x2d, g, out_dtype, tm=512):
    t, d = x2d.shape
    return pl.pallas_call(
        _rmsnorm_body,
        grid=(t // tm,),
        in_specs=[pl.BlockSpec((tm, d), lambda i: (i, 0)),
                  pl.BlockSpec((1, d), lambda i: (0, 0))],
        out_specs=pl.BlockSpec((tm, d), lambda i: (i, 0)),
        out_shape=jax.ShapeDtypeStruct((t, d), out_dtype),
        compiler_params=_params(("arbitrary",), 40),
        name="rmsnorm",
    )(x2d, g.reshape(1, d))


def _matmul_body(*refs, has_res, has_scale):
    refs = list(refs)
    a_ref, w_ref = refs[:2]
    o_ref, wb_ref = refs[-2:]
    s_ref = refs[2] if has_scale else None
    r_ref = refs[-3] if has_res else None

    @pl.when(pl.program_id(1) == 0)
    def _():
        w = w_ref[...]
        if has_scale:
            w = w * s_ref[...]
        wb_ref[...] = w.astype(BF16)

    acc = jnp.dot(a_ref[...], wb_ref[...], preferred_element_type=F32)
    if has_res:
        acc = acc + r_ref[...]
    o_ref[...] = acc.astype(o_ref.dtype)


def matmul(a, w, n_out, *, col0=0, res=None, col_scale=None, out_dtype=BF16,
           tm=1024, tn=1024, vmem_mib=48, name="matmul"):
    t, k = a.shape
    tn = min(tn, n_out)
    tm = min(tm, t)
    assert t % tm == 0 and n_out % tn == 0 and col0 % tn == 0
    cb0 = col0 // tn
    in_specs = [pl.BlockSpec((tm, k), lambda j, i: (i, 0)),
                pl.BlockSpec((k, tn), lambda j, i: (0, j + cb0))]
    args = [a, w]
    if col_scale is not None:
        in_specs.append(pl.BlockSpec((1, tn), lambda j, i: (0, j)))
        args.append(col_scale.reshape(1, n_out))
    if res is not None:
        in_specs.append(pl.BlockSpec((tm, tn), lambda j, i: (i, j)))
        args.append(res)
    return pl.pallas_call(
        functools.partial(_matmul_body, has_res=res is not None,
                          has_scale=col_scale is not None),
        grid=(n_out // tn, t // tm),
        in_specs=in_specs,
        out_specs=pl.BlockSpec((tm, tn), lambda j, i: (i, j)),
        out_shape=jax.ShapeDtypeStruct((t, n_out), out_dtype),
        scratch_shapes=[pltpu.VMEM((k, tn), BF16)],
        compiler_params=_params(("arbitrary", "arbitrary"), vmem_mib),
        name=name,
    )(*args)


def _forget_cumsum_body(f_ref, b_ref, c_ref):
    z = f_ref[0] + b_ref[...]
    x = jnp.minimum(z, 0.0) - jnp.log1p(jnp.exp(-jnp.abs(z)))
    n = x.shape[1]
    lane = lax.broadcasted_iota(I32, x.shape, 1)
    shift = 1
    while shift < n:
        x = x + jnp.where(lane >= shift, pltpu.roll(x, shift, axis=1), 0.0)
        shift *= 2
    c_ref[0] = x


def forget_cumsum(f_t, b_forget):
    b, h, l = f_t.shape
    return pl.pallas_call(
        _forget_cumsum_body,
        grid=(b,),
        in_specs=[pl.BlockSpec((1, h, l), lambda i: (i, 0, 0)),
                  pl.BlockSpec((h, 1), lambda i: (0, 0))],
        out_specs=pl.BlockSpec((1, h, l), lambda i: (i, 0, 0)),
        out_shape=jax.ShapeDtypeStruct((b, h, l), F32),
        compiler_params=_params(("arbitrary",), 16),
        name="forget_cumsum",
    )(f_t, b_forget.reshape(h, 1))


def _pool_body(u_ref, w_ref, s_ref, o_ref):
    g = pl.program_id(1)
    u = u_ref[0].astype(F32)
    row = lax.broadcasted_iota(I32, u.shape, 0)
    acc = u
    sums = []
    for shift in (1, 2, 4, 8):
        acc = acc + jnp.where(row >= shift, pltpu.roll(acc, shift, axis=0), 0.0)
        sums.append(acc)
    win_sum = jnp.where(g == 0, sums[0],
                        jnp.where(g == 1, sums[1],
                                  jnp.where(g == 2, sums[2], sums[3])))
    window = jnp.left_shift(jnp.int32(2), g)
    count = jnp.minimum(row + 1, window).astype(F32)
    mixed = win_sum / count - u
    y = jnp.dot(mixed.astype(BF16), w_ref[0].astype(BF16),
                preferred_element_type=F32)
    o_ref[0] = (y * s_ref[...]).astype(o_ref.dtype)


def pool_mixer(proj3, pool_w, pool_scale):
    b, l, _ = proj3.shape
    ng = len(POOL_WINDOWS)
    return pl.pallas_call(
        _pool_body,
        grid=(b, ng),
        in_specs=[pl.BlockSpec((1, l, POOL_GROUP), lambda i, g: (i, 0, g)),
                  pl.BlockSpec((1, POOL_GROUP, POOL_GROUP), lambda i, g: (g, 0, 0)),
                  pl.BlockSpec((1, POOL_GROUP), lambda i, g: (0, g))],
        out_specs=pl.BlockSpec((1, l, POOL_GROUP), lambda i, g: (i, 0, g)),
        out_shape=jax.ShapeDtypeStruct((b, l, POOL_WIDTH), BF16),
        compiler_params=_params(("arbitrary", "arbitrary"), 40),
        name="pool_mixer",
    )(proj3, pool_w, pool_scale.reshape(1, POOL_WIDTH))


LOG2E = 1.4426950408889634
FOX_Q_SCALE = FOX_HEAD_DIM ** -0.5 * LOG2E


FOX_HEADS_PER_STEP = 2


def _fox_body(q_ref, k_ref, v_ref, cq_ref, ck_ref, o_ref, m_ref, l_ref, acc_ref,
              *, tile):
    hp = pl.program_id(1)
    qi = pl.program_id(2)
    hd = FOX_HEAD_DIM
    cq8 = cq_ref[0]
    head_lane = lax.broadcasted_iota(I32, cq8.shape, 1)
    heads = []
    for hh in range(FOX_HEADS_PER_STEP):
        head = hp * FOX_HEADS_PER_STEP + hh
        cq = jnp.sum(jnp.where(head_lane == head, cq8, 0.0), axis=1,
                     keepdims=True) * LOG2E
        heads.append((hh, head, slice(hh * hd, (hh + 1) * hd), cq))

    def logits(j, head, cols):
        start = pl.multiple_of(j * tile, tile)
        k = k_ref[0, pl.ds(start, tile), cols]
        ck = ck_ref[0, pl.ds(head, 1), pl.ds(start, tile)] * LOG2E
        s = lax.dot_general(q_ref[0, :, cols], k, (((1,), (1,)), ((), ())),
                            preferred_element_type=F32)
        return s - ck, start

    for hh, head, cols, cq in heads:
        s, start = logits(qi, head, cols)
        r = lax.broadcasted_iota(I32, s.shape, 0)
        c = lax.broadcasted_iota(I32, s.shape, 1)
        s = jnp.where(c <= r, s, -jnp.inf)
        m0 = jnp.max(s, axis=1, keepdims=True) + cq
        p = jnp.exp2(s + (cq - m0))
        m_ref[hh] = m0
        l_ref[hh] = jnp.sum(p, axis=1, keepdims=True)
        acc_ref[hh] = jnp.dot(p.astype(BF16), v_ref[0, pl.ds(start, tile), cols],
                              preferred_element_type=F32)

    def step(j, _):
        for hh, head, cols, cq in heads:
            s, start = logits(j, head, cols)
            m_old = m_ref[hh]
            m_new = jnp.maximum(m_old, jnp.max(s, axis=1, keepdims=True) + cq)
            alpha = jnp.exp2(m_old - m_new)
            p = jnp.exp2(s + (cq - m_new))
            l_ref[hh] = alpha * l_ref[hh] + jnp.sum(p, axis=1, keepdims=True)
            acc_ref[hh] = alpha * acc_ref[hh] + jnp.dot(
                p.astype(BF16), v_ref[0, pl.ds(start, tile), cols],
                preferred_element_type=F32)
            m_ref[hh] = m_new
        return 0

    lax.fori_loop(0, qi, step, 0)
    for hh, head, cols, cq in heads:
        o_ref[0, :, cols] = (acc_ref[hh] / l_ref[hh]).astype(o_ref.dtype)


def fox_attention(proj3, c_row, c_col, tile=512):
    b, l, _ = proj3.shape
    hps = FOX_HEADS_PER_STEP
    w = hps * FOX_HEAD_DIM
    q0 = POOL_WIDTH // w
    k0 = q0 + FOX_HEADS // hps
    v0 = k0 + FOX_HEADS // hps
    return pl.pallas_call(
        functools.partial(_fox_body, tile=tile),
        grid=(b, FOX_HEADS // hps, l // tile),
        in_specs=[
            pl.BlockSpec((1, tile, w), lambda i, h, q: (i, q, q0 + h)),
            pl.BlockSpec((1, l, w), lambda i, h, q: (i, 0, k0 + h)),
            pl.BlockSpec((1, l, w), lambda i, h, q: (i, 0, v0 + h)),
            pl.BlockSpec((1, tile, FOX_HEADS), lambda i, h, q: (i, q, 0)),
            pl.BlockSpec((1, FOX_HEADS, l), lambda i, h, q: (i, 0, 0)),
        ],
        out_specs=pl.BlockSpec((1, tile, w), lambda i, h, q: (i, q, h)),
        out_shape=jax.ShapeDtypeStruct((b, l, FOX_WIDTH), BF16),
        scratch_shapes=[pltpu.VMEM((hps, tile, 1), F32),
                        pltpu.VMEM((hps, tile, 1), F32),
                        pltpu.VMEM((hps, tile, FOX_HEAD_DIM), F32)],
        compiler_params=_params(("arbitrary", "arbitrary", "arbitrary"), 32),
        name="fox_attention",
    )(proj3, proj3, proj3, c_col, c_row)


def _ssd_dt_body(raw_ref, bias_ref, alog_ref, dt_ref, acs_ref):
    z = raw_ref[...] + bias_ref[...]
    dt = jnp.maximum(z, 0.0) + jnp.log1p(jnp.exp(-jnp.abs(z)))
    a_dt = dt * (-jnp.exp(alog_ref[...]))
    n = z.shape[0]
    r = lax.broadcasted_iota(I32, (n, n), 0)
    c = lax.broadcasted_iota(I32, (n, n), 1)
    tri = (c <= r).astype(F32)
    dt_ref[...] = dt
    acs_ref[...] = jnp.dot(tri, a_dt, preferred_element_type=F32,
                           precision=lax.Precision.HIGHEST)


def ssd_dt(raw, bias_pad, alog_pad):
    t, n = raw.shape
    spec = pl.BlockSpec((SSD_CHUNK, n), lambda i: (i, 0))
    vec = pl.BlockSpec((1, n), lambda i: (0, 0))
    return pl.pallas_call(
        _ssd_dt_body,
        grid=(t // SSD_CHUNK,),
        in_specs=[spec, vec, vec],
        out_specs=[spec, spec],
        out_shape=[jax.ShapeDtypeStruct((t, n), F32)] * 2,
        compiler_params=_params(("arbitrary",), 16),
        name="ssd_dt",
    )(raw, bias_pad.reshape(1, n), alog_pad.reshape(1, n))


def _silu(x):
    return x / (1.0 + jnp.exp(-x))


def _causal_conv_silu(u_ref, w_ref, b_ref):
    u = u_ref[0].astype(F32)
    row = lax.broadcasted_iota(I32, u.shape, 0)
    w = w_ref[...]
    out = b_ref[...] + u * w[SSD_CONV - 1:SSD_CONV, :]
    for shift in range(1, SSD_CONV):
        prev = jnp.where(row >= shift, pltpu.roll(u, shift, axis=0), 0.0)
        out = out + prev * w[SSD_CONV - 1 - shift:SSD_CONV - shift, :]
    return _silu(out)


def _pair_lanes(cols, j):
    rows = cols.shape[0]
    lane = lax.broadcasted_iota(I32, (rows, LANES), 1)
    lo = jnp.broadcast_to(cols[:, 2 * j:2 * j + 1], (rows, LANES))
    hi = jnp.broadcast_to(cols[:, 2 * j + 1:2 * j + 2], (rows, LANES))
    return jnp.where(lane < SSD_HEAD_DIM, lo, hi)


def _ssd_body(z_ref, x_ref, b_ref, c_ref, wx_ref, wb_ref, wc_ref,
              bx_ref, bb_ref, bc_ref, dt_ref, acs_ref, acst_ref,
              dskip_ref, gn_ref, o_ref, xs_ref, bs_ref, cs_ref, st_ref):
    q = SSD_CHUNK
    xs_ref[...] = _causal_conv_silu(x_ref, wx_ref, bx_ref)
    bs_ref[...] = _causal_conv_silu(b_ref, wb_ref, bb_ref).astype(BF16)
    cs_ref[...] = _causal_conv_silu(c_ref, wc_ref, bc_ref).astype(BF16)
    st_ref[...] = jnp.zeros_like(st_ref)

    r = lax.broadcasted_iota(I32, (q, q), 0)
    c = lax.broadcasted_iota(I32, (q, q), 1)
    causal = c <= r
    lane = lax.broadcasted_iota(I32, (q, LANES), 1)
    first_head = lane < SSD_HEAD_DIM

    def chunk(ci, _):
        r0 = pl.multiple_of(ci * q, q)
        xc = xs_ref[pl.ds(r0, q), :]
        bm = bs_ref[pl.ds(r0, q), :]
        cm = cs_ref[pl.ds(r0, q), :]
        zc = z_ref[0, pl.ds(r0, q), :].astype(F32)
        dt = dt_ref[0, 0, pl.ds(r0, q), :]
        acs = acs_ref[0, 0, pl.ds(r0, q), :]
        acs_t = acst_ref[0, 0, :, pl.ds(r0, q)]
        acs_last = acs[q - 1:q, :]
        exp_acs = jnp.exp(acs)
        decay_to_end = jnp.exp(acs_last - acs)
        chunk_decay = jnp.exp(acs_last)
        cb = lax.dot_general(cm, bm, (((1,), (1,)), ((), ())),
                             preferred_element_type=F32)
        bm_t = bm.astype(F32).T.astype(BF16)
        gated = []
        for j in range(SSD_GROUP_HEADS // 2):
            lanes = slice(j * LANES, (j + 1) * LANES)
            xp = xc[:, lanes]
            x_dt = xp * _pair_lanes(dt, j)
            x_dt_b = x_dt.astype(BF16)
            y_heads = []
            for hh in range(2):
                hd = 2 * j + hh
                seg = jnp.broadcast_to(acs[:, hd:hd + 1], (q, q)) - acs_t[hd:hd + 1, :]
                decay = jnp.exp(jnp.where(causal, seg, -jnp.inf))
                y_heads.append(jnp.dot((cb * decay).astype(BF16), x_dt_b,
                                       preferred_element_type=F32))
            y = jnp.where(first_head, y_heads[0], y_heads[1])
            state = st_ref[:, lanes]
            y = y + jnp.dot(cm, state.astype(BF16),
                            preferred_element_type=F32) * _pair_lanes(exp_acs, j)
            y = y + xp * dskip_ref[:, lanes]
            x_end = (x_dt * _pair_lanes(decay_to_end, j)).astype(BF16)
            st_ref[:, lanes] = state * _pair_lanes(chunk_decay, j) + jnp.dot(
                bm_t, x_end, preferred_element_type=F32)
            gated.append(y * _silu(zc[:, lanes]))
        gated = jnp.concatenate(gated, axis=1)
        inv = lax.rsqrt(jnp.mean(gated * gated, axis=-1, keepdims=True) + EPS)
        o_ref[0, pl.ds(r0, q), :] = (gated * inv * gn_ref[...]).astype(o_ref.dtype)
        return 0

    lax.fori_loop(0, x_ref.shape[1] // q, chunk, 0)


def ssd_core(proj3, conv_w, conv_b, dt_g, acs_g, acs_gt, d_skip_exp, gnorm):
    b, l, _ = proj3.shape
    gw = SSD_GROUP_WIDTH
    n = SSD_STATE
    inner = SSD_GROUPS * gw
    x_blk0 = inner // gw
    bm_blk0 = 2 * inner // n
    cm_blk0 = bm_blk0 + SSD_GROUPS
    wb_blk0 = inner // n
    wc_blk0 = wb_blk0 + SSD_GROUPS
    hpg = SSD_GROUP_HEADS
    conv_b2 = conv_b.reshape(1, -1)
    return pl.pallas_call(
        _ssd_body,
        grid=(b, SSD_GROUPS),
        in_specs=[
            pl.BlockSpec((1, l, gw), lambda i, g: (i, 0, g)),
            pl.BlockSpec((1, l, gw), lambda i, g: (i, 0, x_blk0 + g)),
            pl.BlockSpec((1, l, n), lambda i, g: (i, 0, bm_blk0 + g)),
            pl.BlockSpec((1, l, n), lambda i, g: (i, 0, cm_blk0 + g)),
            pl.BlockSpec((SSD_CONV, gw), lambda i, g: (0, g)),
            pl.BlockSpec((SSD_CONV, n), lambda i, g: (0, wb_blk0 + g)),
            pl.BlockSpec((SSD_CONV, n), lambda i, g: (0, wc_blk0 + g)),
            pl.BlockSpec((1, gw), lambda i, g: (0, g)),
            pl.BlockSpec((1, n), lambda i, g: (0, wb_blk0 + g)),
            pl.BlockSpec((1, n), lambda i, g: (0, wc_blk0 + g)),
            pl.BlockSpec((1, 1, l, hpg), lambda i, g: (i, g, 0, 0)),
            pl.BlockSpec((1, 1, l, hpg), lambda i, g: (i, g, 0, 0)),
            pl.BlockSpec((1, 1, hpg, l), lambda i, g: (i, g, 0, 0)),
            pl.BlockSpec((1, gw), lambda i, g: (0, g)),
            pl.BlockSpec((1, gw), lambda i, g: (0, g)),
        ],
        out_specs=pl.BlockSpec((1, l, gw), lambda i, g: (i, 0, g)),
        out_shape=jax.ShapeDtypeStruct((b, l, inner), BF16),
        scratch_shapes=[pltpu.VMEM((l, gw), F32), pltpu.VMEM((l, n), BF16),
                        pltpu.VMEM((l, n), BF16), pltpu.VMEM((n, gw), F32)],
        compiler_params=_params(("arbitrary", "arbitrary"), 48),
        name="ssd_core",
    )(proj3, proj3, proj3, proj3, conv_w, conv_w, conv_w, conv_b2, conv_b2,
      conv_b2, dt_g, acs_g, acs_gt, d_skip_exp, gnorm.reshape(1, inner))


def _first_argmax(vals, nrows):
    row = lax.broadcasted_iota(I32, vals.shape, 0)
    top = jnp.max(vals, axis=0, keepdims=True)
    idx = jnp.min(jnp.where(vals == top, row, nrows), axis=0, keepdims=True)
    return top, idx, row


def _pack_bf16_halves(y):
    half = y.shape[1] // 2

    def bits(v):
        return lax.bitcast_convert_type(v.astype(BF16).astype(F32), U32)

    return bits(y[:, half:]) | (bits(y[:, :half]) >> 16)


def _unpack_bf16_halves(w):
    lo = lax.bitcast_convert_type(w << 16, F32)
    hi = lax.bitcast_convert_type(w & jnp.uint32(0xFFFF0000), F32)
    return lo, hi


def _router_body(x_ref, g_ref, rt_ref, rb_ref, eid_ref, wt_ref, rank_ref,
                 cnt_ref, hp_ref, carry_ref):
    step = pl.program_id(0)

    @pl.when(step == 0)
    def _():
        carry_ref[...] = jnp.zeros_like(carry_ref)

    x = x_ref[...]
    tm = x.shape[0]
    h = x * lax.rsqrt(jnp.mean(x * x, axis=-1, keepdims=True) + EPS) * g_ref[...]
    hp_ref[...] = _pack_bf16_halves(h)
    logits = lax.dot_general(rt_ref[...], h, (((1,), (1,)), ((), ())),
                             preferred_element_type=F32,
                             precision=lax.Precision.HIGHEST) + rb_ref[...]
    e_logits = logits[:MOE_EXPERTS]
    g_logits = logits[MOE_EXPERTS:MOE_EXPERTS + MOE_GROUPS]
    g_max, g_sel, _ = _first_argmax(g_logits, MOE_GROUPS)
    g_w = 1.0 / jnp.sum(jnp.exp(g_logits - g_max), axis=0, keepdims=True)
    sel = jnp.zeros((MOE_PER_GROUP, tm), F32)
    for grp in range(MOE_GROUPS):
        sel = jnp.where(g_sel == grp,
                        e_logits[grp * MOE_PER_GROUP:(grp + 1) * MOE_PER_GROUP], sel)
    m1, i1, row8 = _first_argmax(sel, MOE_PER_GROUP)
    rest = jnp.where(row8 == i1, -jnp.inf, sel)
    m2, i2, _ = _first_argmax(rest, MOE_PER_GROUP)
    p2 = jnp.exp(m2 - m1)
    w1 = g_w / (1.0 + p2)
    w2 = g_w * p2 / (1.0 + p2)
    e1 = g_sel * MOE_PER_GROUP + i1
    e2 = g_sel * MOE_PER_GROUP + i2

    r = lax.broadcasted_iota(I32, (tm, tm), 0)
    c = lax.broadcasted_iota(I32, (tm, tm), 1)
    upper = jnp.where(r <= c, 1.0, 0.0).astype(BF16)
    row32 = lax.broadcasted_iota(I32, (MOE_EXPERTS, tm), 0)
    hit1 = row32 == e1
    hit2 = row32 == e2
    cum1 = jnp.dot(jnp.where(hit1, 1.0, 0.0).astype(BF16), upper,
                   preferred_element_type=F32)
    cum2 = jnp.dot(jnp.where(hit2, 1.0, 0.0).astype(BF16), upper,
                   preferred_element_type=F32)
    carry = carry_ref[...]
    tot1 = cum1[:, tm - 1:tm]
    tot2 = cum2[:, tm - 1:tm]
    rank1 = jnp.sum(jnp.where(hit1, carry + cum1 - 1.0, 0.0), axis=0, keepdims=True)
    rank2 = jnp.sum(jnp.where(hit2, carry + tot1 + cum2 - 1.0, 0.0), axis=0,
                    keepdims=True)
    new_carry = carry + tot1 + tot2
    carry_ref[...] = new_carry

    eid_ref[...] = jnp.concatenate([e1, e2], axis=0)
    wt_ref[...] = jnp.concatenate([w1, w2], axis=0)
    rank_ref[...] = jnp.concatenate([rank1, rank2], axis=0).astype(I32)
    cnt_ref[...] = jnp.broadcast_to(new_carry, cnt_ref.shape).astype(I32)


def moe_router(x2d, g, router_g, router_g_b, router_e, router_e_b, tm=512):
    t, d = x2d.shape
    pad = 8 - MOE_GROUPS
    rt = jnp.concatenate([router_e.T, router_g.T, jnp.zeros((pad, d), F32)], axis=0)
    rb = jnp.concatenate([router_e_b, router_g_b, jnp.zeros((pad,), F32)]).reshape(-1, 1)
    nr = rt.shape[0]
    tok = pl.BlockSpec((2, tm), lambda i: (0, i))
    return pl.pallas_call(
        _router_body,
        grid=(t // tm,),
        in_specs=[pl.BlockSpec((tm, d), lambda i: (i, 0)),
                  pl.BlockSpec((1, d), lambda i: (0, 0)),
                  pl.BlockSpec((nr, d), lambda i: (0, 0)),
                  pl.BlockSpec((nr, 1), lambda i: (0, 0))],
        out_specs=[tok, tok, tok,
                   pl.BlockSpec((MOE_EXPERTS, LANES), lambda i: (0, 0)),
                   pl.BlockSpec((tm, d // 2), lambda i: (i, 0))],
        out_shape=[jax.ShapeDtypeStruct((2, t), I32),
                   jax.ShapeDtypeStruct((2, t), F32),
                   jax.ShapeDtypeStruct((2, t), I32),
                   jax.ShapeDtypeStruct((MOE_EXPERTS, LANES), I32),
                   jax.ShapeDtypeStruct((t, d // 2), U32)],
        scratch_shapes=[pltpu.VMEM((MOE_EXPERTS, 1), F32)],
        compiler_params=_params(("arbitrary",), 40),
        name="moe_router",
    )(x2d, g.reshape(1, d), rt, rb)


def _row_tokens_body(dest_ref, tok_ref):
    n_rows = tok_ref.shape[0]
    t = dest_ref.shape[0] // 2

    def clear(r, _):
        tok_ref[r] = 0
        return 0

    lax.fori_loop(0, n_rows, clear, 0, unroll=8)

    def fill(tok, _):
        tok_ref[dest_ref[tok]] = tok
        tok_ref[dest_ref[t + tok]] = tok
        return 0

    lax.fori_loop(0, t, fill, 0, unroll=8)


def moe_row_tokens(dest_flat, n_rows):
    return pl.pallas_call(
        _row_tokens_body,
        grid_spec=pltpu.PrefetchScalarGridSpec(
            num_scalar_prefetch=1,
            grid=(1,),
            in_specs=[],
            out_specs=pl.BlockSpec(memory_space=pltpu.SMEM),
        ),
        out_shape=jax.ShapeDtypeStruct((n_rows,), I32),
        compiler_params=pltpu.CompilerParams(dimension_semantics=("arbitrary",)),
        name="moe_row_tokens",
    )(dest_flat)


def _experts_body(blk_e_ref, n_used_ref, next_e_ref, grp_ref, tok_ref,
                  hp_hbm, wg_hbm, wu_hbm, wd_hbm, o_ref,
                  xbuf_ref, xsem, wg_st, wu_st, wd_st, wsem,
                  wgb_ref, wub_ref, wdb_ref, *, layer):
    i = pl.program_id(0)
    n_used = n_used_ref[0]
    used = i < n_used
    e_cur = blk_e_ref[i]
    prev = blk_e_ref[jnp.maximum(i - 1, 0)]
    fresh = jnp.logical_and(used, jnp.logical_or(i == 0, e_cur != prev))
    rows = xbuf_ref.shape[1]

    def weight_copies(e, slot):
        return (pltpu.make_async_copy(wg_hbm.at[layer, e], wg_st.at[slot],
                                      wsem.at[0, slot]),
                pltpu.make_async_copy(wu_hbm.at[layer, e], wu_st.at[slot],
                                      wsem.at[1, slot]),
                pltpu.make_async_copy(wd_hbm.at[layer, e], wd_st.at[slot],
                                      wsem.at[2, slot]))

    def row_copy(blk, slot, r):
        tok = tok_ref[blk * rows + r]
        return pltpu.make_async_copy(hp_hbm.at[pl.ds(tok, 1)],
                                     xbuf_ref.at[slot, pl.ds(r, 1)],
                                     xsem.at[slot])

    @pl.when(i == 0)
    def _():
        for cp in weight_copies(e_cur, 0):
            cp.start()

        def body(r, _):
            row_copy(0, 0, r).start()
            return 0
        lax.fori_loop(0, rows, body, 0, unroll=8)

    wslot = grp_ref[i] % 2
    nxt = next_e_ref[i]

    @pl.when(jnp.logical_and(fresh, nxt >= 0))
    def _():
        for cp in weight_copies(nxt, 1 - wslot):
            cp.start()

    @pl.when(fresh)
    def _():
        for cp in weight_copies(e_cur, wslot):
            cp.wait()
        wgb_ref[...] = wg_st[wslot].astype(BF16)
        wub_ref[...] = wu_st[wslot].astype(BF16)
        wdb_ref[...] = wd_st[wslot].astype(BF16)

    def compute(gather_next):
        slot = i % 2
        pltpu.make_async_copy(hp_hbm.at[pl.ds(0, rows)], xbuf_ref.at[slot],
                              xsem.at[slot]).wait()
        if gather_next:
            for r in range(rows):
                row_copy(i + 1, 1 - slot, r).start()
        lo, hi = _unpack_bf16_halves(xbuf_ref[slot])
        h = jnp.concatenate([lo.astype(BF16), hi.astype(BF16)], axis=1)
        gate = jnp.dot(h, wgb_ref[...], preferred_element_type=F32)
        up = jnp.dot(h, wub_ref[...], preferred_element_type=F32)
        act = (_silu(gate) * up).astype(BF16)
        y = jnp.dot(act, wdb_ref[...], preferred_element_type=F32)
        o_ref[...] = _pack_bf16_halves(y)

    has_next = i + 1 < n_used

    @pl.when(has_next)
    def _():
        compute(True)

    @pl.when(jnp.logical_and(used, jnp.logical_not(has_next)))
    def _():
        compute(False)

    @pl.when(jnp.logical_not(used))
    def _():
        o_ref[...] = jnp.zeros_like(o_ref)


def moe_experts(blk_e, n_used, next_e, grp, row_tok, hp, w_gate, w_up, w_down,
                layer):
    n_rows = row_tok.shape[0]
    half = hp.shape[1]
    d = 2 * half
    ff = w_gate.shape[3]
    nblk = n_rows // MOE_BLOCK
    hbm = pl.BlockSpec(memory_space=pl.ANY)
    return pl.pallas_call(
        functools.partial(_experts_body, layer=layer),
        grid_spec=pltpu.PrefetchScalarGridSpec(
            num_scalar_prefetch=5,
            grid=(nblk,),
            in_specs=[hbm, hbm, hbm, hbm],
            out_specs=pl.BlockSpec((MOE_BLOCK, half), lambda i, *_: (i, 0)),
            scratch_shapes=[pltpu.VMEM((2, MOE_BLOCK, half), U32),
                            pltpu.SemaphoreType.DMA((2,)),
                            pltpu.VMEM((2, d, ff), F32), pltpu.VMEM((2, d, ff), F32),
                            pltpu.VMEM((2, ff, d), F32),
                            pltpu.SemaphoreType.DMA((3, 2)),
                            pltpu.VMEM((d, ff), BF16), pltpu.VMEM((d, ff), BF16),
                            pltpu.VMEM((ff, d), BF16)],
        ),
        out_shape=jax.ShapeDtypeStruct((n_rows, half), U32),
        compiler_params=_params(("arbitrary",), 56),
        name="moe_experts",
    )(blk_e, n_used, next_e, grp, row_tok, hp, w_gate, w_up, w_down)


def _combine_body(dest_ref, x_ref, wt_ref, g_ref, ys_ref, o_ref, buf_ref, sem,
                  *, final_norm):
    i = pl.program_id(0)
    n = pl.num_programs(0)
    tm = x_ref.shape[0]
    t = tm * n

    def issue(tile, slot):
        def body(r, _):
            for k in range(2):
                d = dest_ref[k * t + tile * tm + r]
                pltpu.make_async_copy(ys_ref.at[pl.ds(d, 1)],
                                      buf_ref.at[slot, k, pl.ds(r, 1)],
                                      sem.at[slot]).start()
            return 0
        lax.fori_loop(0, tm, body, 0, unroll=8)

    @pl.when(i == 0)
    def _():
        issue(0, 0)

    @pl.when(i + 1 < n)
    def _():
        issue(i + 1, (i + 1) % 2)

    slot = i % 2
    for k in range(2):
        pltpu.make_async_copy(ys_ref.at[pl.ds(0, tm)], buf_ref.at[slot, k],
                              sem.at[slot]).wait()
    w = wt_ref[...]
    lo1, hi1 = _unpack_bf16_halves(buf_ref[slot, 0])
    lo2, hi2 = _unpack_bf16_halves(buf_ref[slot, 1])
    half = lo1.shape[1]
    x = x_ref[...]
    y_lo = x[:, :half] + w[:, 0:1] * lo1 + w[:, 1:2] * lo2
    y_hi = x[:, half:] + w[:, 0:1] * hi1 + w[:, 1:2] * hi2
    if final_norm:
        ssq = (jnp.sum(y_lo * y_lo, axis=-1, keepdims=True)
               + jnp.sum(y_hi * y_hi, axis=-1, keepdims=True))
        inv = lax.rsqrt(ssq / (2 * half) + EPS)
        g = g_ref[...]
        y_lo = y_lo * inv * g[:, :half]
        y_hi = y_hi * inv * g[:, half:]
    o_ref[:, :half] = y_lo
    o_ref[:, half:] = y_hi


def moe_combine(dest_flat, x2d, wt_t, ys, g_final, final_norm, tm=128):
    t, d = x2d.shape
    return pl.pallas_call(
        functools.partial(_combine_body, final_norm=final_norm),
        grid_spec=pltpu.PrefetchScalarGridSpec(
            num_scalar_prefetch=1,
            grid=(t // tm,),
            in_specs=[pl.BlockSpec((tm, d), lambda i, ds: (i, 0)),
                      pl.BlockSpec((tm, 2), lambda i, ds: (i, 0)),
                      pl.BlockSpec((1, d), lambda i, ds: (0, 0)),
                      pl.BlockSpec(memory_space=pl.ANY)],
            out_specs=pl.BlockSpec((tm, d), lambda i, ds: (i, 0)),
            scratch_shapes=[pltpu.VMEM((2, 2, tm, d // 2), U32),
                            pltpu.SemaphoreType.DMA((2,))],
        ),
        out_shape=jax.ShapeDtypeStruct((t, d), F32),
        compiler_params=_params(("arbitrary",), 32),
        name="moe_combine",
    )(dest_flat, x2d, wt_t, g_final.reshape(1, d), ys)


def hier_moe(x2d, g_ffn, router_g, router_g_b, router_e, router_e_b,
             w_gate, w_up, w_down, layer, g_final, final_norm):
    t, d = x2d.shape
    eid, wts, rank, cnt, hp = moe_router(x2d, g_ffn, router_g, router_g_b,
                                         router_e, router_e_b)
    counts = cnt[:, 0]
    padded = (counts + MOE_BLOCK - 1) // MOE_BLOCK * MOE_BLOCK
    pend = jnp.cumsum(padded)
    pstart = pend - padded
    n_rows = 2 * t + MOE_EXPERTS * MOE_BLOCK
    nblk = n_rows // MOE_BLOCK
    expert_ids = jnp.arange(MOE_EXPERTS, dtype=I32)
    start_of = jnp.sum(jnp.where(eid[..., None] == expert_ids, pstart, 0), axis=-1)
    dest = (start_of + rank).reshape(-1)
    blk_start = jnp.arange(nblk, dtype=I32) * MOE_BLOCK
    blk_e = jnp.sum(blk_start[:, None] >= pend[None, :], axis=1).astype(I32)
    blk_e = jnp.minimum(blk_e, MOE_EXPERTS - 1)
    n_used = (pend[-1:] // MOE_BLOCK).astype(I32)
    last_e = blk_e[jnp.maximum(n_used[0] - 1, 0)]
    blk_e = jnp.where(jnp.arange(nblk) < n_used[0], blk_e, last_e)
    changed = jnp.concatenate([jnp.ones((1,), I32),
                               (blk_e[1:] != blk_e[:-1]).astype(I32)])
    grp = jnp.cumsum(changed) - 1
    grp_end = jnp.sum(jnp.where(blk_e[:, None] == expert_ids, pend, 0), axis=-1)
    nxt_blk = grp_end // MOE_BLOCK
    nxt_e = jnp.sum(jnp.where(nxt_blk[:, None] == jnp.arange(nblk), blk_e, 0), axis=-1)
    next_e = jnp.where(nxt_blk < n_used[0], nxt_e, -1).astype(I32)
    row_tok = moe_row_tokens(dest, n_rows)
    ys = moe_experts(blk_e, n_used, next_e, grp.astype(I32), row_tok, hp,
                     w_gate, w_up, w_down, layer)
    return moe_combine(dest, x2d, wts.T, ys, g_final, final_norm)


def _pad_cols(w, n):
    return jnp.pad(w, ((0, 0), (0, n - w.shape[1])))


def pool_fox_layer(x2d, b, l, g_mix, w_in, b_forget, pool_w, pool_scale, w_out):
    h = rmsnorm(x2d, g_mix, BF16)
    main = POOL_WIDTH + 3 * FOX_WIDTH
    col = jnp.arange(main)
    is_q = jnp.logical_and(col >= POOL_WIDTH, col < POOL_WIDTH + FOX_WIDTH)
    q_scale = jnp.where(is_q, FOX_Q_SCALE, 1.0).astype(F32)
    proj = matmul(h, w_in, main, col_scale=q_scale, name="pf_in_proj")
    f = matmul(h, _pad_cols(w_in[:, main:], LANES), LANES, out_dtype=F32,
               name="pf_forget_proj")
    f_t = f[:, :FOX_HEADS].reshape(b, l, FOX_HEADS).transpose(0, 2, 1)
    c = forget_cumsum(f_t, b_forget)
    proj3 = proj.reshape(b, l, main)
    y_pool = pool_mixer(proj3, pool_w, pool_scale)
    y_att = fox_attention(proj3, c, c.transpose(0, 2, 1))
    y = jnp.concatenate([y_pool, y_att], axis=-1).reshape(b * l, -1)
    return matmul(y, w_out, w_out.shape[1], res=x2d, out_dtype=F32, tm=512,
                  name="pf_out_proj")


def ssd_layer(x2d, b, l, g_mix, w_in, conv_w, conv_b, dt_bias, a_log, d_skip,
              gnorm, w_out):
    h = rmsnorm(x2d, g_mix, BF16)
    heads = dt_bias.shape[0]
    inner = heads * SSD_HEAD_DIM
    main = 2 * inner + 2 * SSD_GROUPS * SSD_STATE
    proj = matmul(h, w_in, main, name="ssd_in_proj")
    raw = matmul(h, _pad_cols(w_in[:, main:], LANES), LANES, out_dtype=F32,
                 name="ssd_dt_proj")
    dt, acs = ssd_dt(raw, jnp.pad(dt_bias, (0, LANES - heads)),
                     jnp.pad(a_log, (0, LANES - heads)))
    hpg = SSD_GROUP_HEADS

    def by_group(v):
        return v[:, :heads].reshape(b, l, SSD_GROUPS, hpg).transpose(0, 2, 1, 3)

    dt_g = by_group(dt)
    acs_g = by_group(acs)
    acs_gt = acs_g.transpose(0, 1, 3, 2)
    d_skip_exp = jnp.repeat(d_skip, SSD_HEAD_DIM).reshape(1, inner)
    y = ssd_core(proj.reshape(b, l, main), conv_w, conv_b, dt_g, acs_g, acs_gt,
                 d_skip_exp, gnorm)
    return matmul(y.reshape(b * l, inner), w_out, w_out.shape[1], res=x2d,
                  out_dtype=F32, tm=512, tn=512, name="ssd_out_proj")


def kernel(x, norm_mix, norm_ffn, norm_final, pf_w_in, pf_b_forget, pf_pool_w, pf_pool_scale, pf_w_out, ssd_w_in, ssd_conv_w, ssd_conv_b, ssd_dt_bias, ssd_a_log, ssd_d_skip, ssd_gnorm, ssd_w_out, moe_router_g, moe_router_g_b, moe_router_e, moe_router_e_b, moe_w_gate, moe_w_up, moe_w_down):
    b, l, d = x.shape
    depth = norm_mix.shape[0]
    x2d = x.reshape(b * l, d)
    for i in range(depth):
        j = i // 2
        if i % 2 == 0:
            x2d = pool_fox_layer(x2d, b, l, norm_mix[i], pf_w_in[j], pf_b_forget[j],
                                 pf_pool_w[j], pf_pool_scale[j], pf_w_out[j])
        else:
            x2d = ssd_layer(x2d, b, l, norm_mix[i], ssd_w_in[j], ssd_conv_w[j],
                            ssd_conv_b[j], ssd_dt_bias[j], ssd_a_log[j],
                            ssd_d_skip[j], ssd_gnorm[j], ssd_w_out[j])
        last = i == depth - 1
        x2d = hier_moe(x2d, norm_ffn[i], moe_router_g[i], moe_router_g_b[i],
                       moe_router_e[i], moe_router_e_b[i], moe_w_gate,
                       moe_w_up, moe_w_down, i, norm_final, last)
    return x2d.reshape(b, l, d)
```

```python
import functools

import jax
import jax.numpy as jnp
from jax import lax
from jax.experimental import pallas as pl
from jax.experimental.pallas import tpu as pltpu

F32 = jnp.float32
BF16 = jnp.bfloat16
I32 = jnp.int32
U32 = jnp.uint32
EPS = 1e-6

POOL_WINDOWS = (2, 4, 8, 16)
POOL_GROUP = 256
POOL_WIDTH = 1024
FOX_HEADS = 8
FOX_HEAD_DIM = 128
FOX_WIDTH = 1024
SSD_HEAD_DIM = 64
SSD_STATE = 128
SSD_GROUPS = 8
SSD_GROUP_HEADS = 8
SSD_GROUP_WIDTH = SSD_GROUP_HEADS * SSD_HEAD_DIM
SSD_CONV = 4
SSD_CHUNK = 128
MOE_GROUPS = 4
MOE_PER_GROUP = 8
MOE_EXPERTS = 32
MOE_BLOCK = 256
WEIGHT_DMA_PRIORITY = 1
LANES = 128
SUBLANES = 8
MIB = 1 << 20


def _params(semantics, vmem_mib):
    return pltpu.CompilerParams(dimension_semantics=semantics,
                                vmem_limit_bytes=vmem_mib * MIB)


def _rmsnorm_body(x_ref, g_ref, o_ref):
    x = x_ref[...]
    inv = lax.rsqrt(jnp.mean(x * x, axis=-1, keepdims=True) + EPS)
    o_ref[...] = (x * inv * g_ref[...]).astype(o_ref.dtype)


def rmsnorm(x2d, g, out_dtype, tm=512):
    t, d = x2d.shape
    return pl.pallas_call(
        _rmsnorm_body,
        grid=(t // tm,),
        in_specs=[pl.BlockSpec((tm, d), lambda i: (i, 0)),
                  pl.BlockSpec((1, d), lambda i: (0, 0))],
        out_specs=pl.BlockSpec((tm, d), lambda i: (i, 0)),
        out_shape=jax.ShapeDtypeStruct((t, d), out_dtype),
        compiler_params=_params(("arbitrary",), 40),
        name="rmsnorm",
    )(x2d, g.reshape(1, d))


def _matmul_body(*refs, has_res, has_scale):
    refs = list(refs)
    a_ref, w_ref = refs[:2]
    o_ref, wb_ref = refs[-2:]
    s_ref = refs[2] if has_scale else None
    r_ref = refs[-3] if has_res else None

    @pl.when(pl.program_id(1) == 0)
    def _():
        w = w_ref[...]
        if has_scale:
            w = w * s_ref[...]
        wb_ref[...] = w.astype(BF16)

    acc = jnp.dot(a_ref[...], wb_ref[...], preferred_element_type=F32)
    if has_res:
        acc = acc + r_ref[...]
    o_ref[...] = acc.astype(o_ref.dtype)


def matmul(a, w, n_out, *, col0=0, res=None, col_scale=None, out_dtype=BF16,
           tm=1024, tn=1024, vmem_mib=48, name="matmul"):
    t, k = a.shape
    tn = min(tn, n_out)
    tm = min(tm, t)
    assert t % tm == 0 and n_out % tn == 0 and col0 % tn == 0
    cb0 = col0 // tn
    in_specs = [pl.BlockSpec((tm, k), lambda j, i: (i, 0)),
                pl.BlockSpec((k, tn), lambda j, i: (0, j + cb0))]
    args = [a, w]
    if col_scale is not None:
        in_specs.append(pl.BlockSpec((1, tn), lambda j, i: (0, j)))
        args.append(col_scale.reshape(1, n_out))
    if res is not None:
        in_specs.append(pl.BlockSpec((tm, tn), lambda j, i: (i, j)))
        args.append(res)
    return pl.pallas_call(
        functools.partial(_matmul_body, has_res=res is not None,
                          has_scale=col_scale is not None),
        grid=(n_out // tn, t // tm),
        in_specs=in_specs,
        out_specs=pl.BlockSpec((tm, tn), lambda j, i: (i, j)),
        out_shape=jax.ShapeDtypeStruct((t, n_out), out_dtype),
        scratch_shapes=[pltpu.VMEM((k, tn), BF16)],
        compiler_params=_params(("arbitrary", "arbitrary"), vmem_mib),
        name=name,
    )(*args)


def _forget_cumsum_body(f_ref, b_ref, c_ref):
    z = f_ref[0] + b_ref[...]
    x = jnp.minimum(z, 0.0) - jnp.log1p(jnp.exp(-jnp.abs(z)))
    n = x.shape[1]
    lane = lax.broadcasted_iota(I32, x.shape, 1)
    shift = 1
    while shift < n:
        x = x + jnp.where(lane >= shift, pltpu.roll(x, shift, axis=1), 0.0)
        shift *= 2
    c_ref[0] = x


def forget_cumsum(f_t, b_forget):
    b, h, l = f_t.shape
    return pl.pallas_call(
        _forget_cumsum_body,
        grid=(b,),
        in_specs=[pl.BlockSpec((1, h, l), lambda i: (i, 0, 0)),
                  pl.BlockSpec((h, 1), lambda i: (0, 0))],
        out_specs=pl.BlockSpec((1, h, l), lambda i: (i, 0, 0)),
        out_shape=jax.ShapeDtypeStruct((b, h, l), F32),
        compiler_params=_params(("arbitrary",), 16),
        name="forget_cumsum",
    )(f_t, b_forget.reshape(h, 1))


def _pool_body(u_ref, w_ref, s_ref, o_ref):
    g = pl.program_id(1)
    u = u_ref[0].astype(F32)
    row = lax.broadcasted_iota(I32, u.shape, 0)
    acc = u
    sums = []
    for shift in (1, 2, 4, 8):
        acc = acc + jnp.where(row >= shift, pltpu.roll(acc, shift, axis=0), 0.0)
        sums.append(acc)
    win_sum = jnp.where(g == 0, sums[0],
                        jnp.where(g == 1, sums[1],
                                  jnp.where(g == 2, sums[2], sums[3])))
    window = jnp.left_shift(jnp.int32(2), g)
    count = jnp.minimum(row + 1, window).astype(F32)
    mixed = win_sum / count - u
    y = jnp.dot(mixed.astype(BF16), w_ref[0].astype(BF16),
                preferred_element_type=F32)
    o_ref[0] = (y * s_ref[...]).astype(o_ref.dtype)


def pool_mixer(proj3, pool_w, pool_scale):
    b, l, _ = proj3.shape
    ng = len(POOL_WINDOWS)
    return pl.pallas_call(
        _pool_body,
        grid=(b, ng),
        in_specs=[pl.BlockSpec((1, l, POOL_GROUP), lambda i, g: (i, 0, g)),
                  pl.BlockSpec((1, POOL_GROUP, POOL_GROUP), lambda i, g: (g, 0, 0)),
                  pl.BlockSpec((1, POOL_GROUP), lambda i, g: (0, g))],
        out_specs=pl.BlockSpec((1, l, POOL_GROUP), lambda i, g: (i, 0, g)),
        out_shape=jax.ShapeDtypeStruct((b, l, POOL_WIDTH), BF16),
        compiler_params=_params(("arbitrary", "arbitrary"), 40),
        name="pool_mixer",
    )(proj3, pool_w, pool_scale.reshape(1, POOL_WIDTH))


LOG2E = 1.4426950408889634
FOX_Q_SCALE = FOX_HEAD_DIM ** -0.5 * LOG2E


FOX_HEADS_PER_STEP = 2


def _fox_body(q_ref, k_ref, v_ref, cq_ref, ck_ref, o_ref, m_ref, l_ref, acc_ref,
              *, tile):
    hp = pl.program_id(1)
    qi = pl.program_id(2)
    hd = FOX_HEAD_DIM
    cq8 = cq_ref[0]
    head_lane = lax.broadcasted_iota(I32, cq8.shape, 1)
    heads = []
    for hh in range(FOX_HEADS_PER_STEP):
        head = hp * FOX_HEADS_PER_STEP + hh
        cq = jnp.sum(jnp.where(head_lane == head, cq8, 0.0), axis=1,
                     keepdims=True) * LOG2E
        heads.append((hh, head, slice(hh * hd, (hh + 1) * hd), cq))

    def logits(j, head, cols):
        start = pl.multiple_of(j * tile, tile)
        k = k_ref[0, pl.ds(start, tile), cols]
        ck = ck_ref[0, pl.ds(head, 1), pl.ds(start, tile)] * LOG2E
        s = lax.dot_general(q_ref[0, :, cols], k, (((1,), (1,)), ((), ())),
                            preferred_element_type=F32)
        return s - ck, start

    for hh, head, cols, cq in heads:
        s, start = logits(qi, head, cols)
        r = lax.broadcasted_iota(I32, s.shape, 0)
        c = lax.broadcasted_iota(I32, s.shape, 1)
        s = jnp.where(c <= r, s, -jnp.inf)
        m0 = jnp.max(s, axis=1, keepdims=True) + cq
        p = jnp.exp2(s + (cq - m0))
        m_ref[hh] = m0
        l_ref[hh] = jnp.sum(p, axis=1, keepdims=True)
        acc_ref[hh] = jnp.dot(p.astype(BF16), v_ref[0, pl.ds(start, tile), cols],
                              preferred_element_type=F32)

    def step(j, _):
        for hh, head, cols, cq in heads:
            s, start = logits(j, head, cols)
            m_old = m_ref[hh]
            m_new = jnp.maximum(m_old, jnp.max(s, axis=1, keepdims=True) + cq)
            alpha = jnp.exp2(m_old - m_new)
            p = jnp.exp2(s + (cq - m_new))
            l_ref[hh] = alpha * l_ref[hh] + jnp.sum(p, axis=1, keepdims=True)
            acc_ref[hh] = alpha * acc_ref[hh] + jnp.dot(
                p.astype(BF16), v_ref[0, pl.ds(start, tile), cols],
                preferred_element_type=F32)
            m_ref[hh] = m_new
        return 0

    lax.fori_loop(0, qi, step, 0)
    for hh, head, cols, cq in heads:
        o_ref[0, :, cols] = (acc_ref[hh] / l_ref[hh]).astype(o_ref.dtype)


def fox_attention(proj3, c_row, c_col, tile=512):
    b, l, _ = proj3.shape
    hps = FOX_HEADS_PER_STEP
    w = hps * FOX_HEAD_DIM
    q0 = POOL_WIDTH // w
    k0 = q0 + FOX_HEADS // hps
    v0 = k0 + FOX_HEADS // hps
    return pl.pallas_call(
        functools.partial(_fox_body, tile=tile),
        grid=(b, FOX_HEADS // hps, l // tile),
        in_specs=[
            pl.BlockSpec((1, tile, w), lambda i, h, q: (i, q, q0 + h)),
            pl.BlockSpec((1, l, w), lambda i, h, q: (i, 0, k0 + h)),
            pl.BlockSpec((1, l, w), lambda i, h, q: (i, 0, v0 + h)),
            pl.BlockSpec((1, tile, FOX_HEADS), lambda i, h, q: (i, q, 0)),
            pl.BlockSpec((1, FOX_HEADS, l), lambda i, h, q: (i, 0, 0)),
        ],
        out_specs=pl.BlockSpec((1, tile, w), lambda i, h, q: (i, q, h)),
        out_shape=jax.ShapeDtypeStruct((b, l, FOX_WIDTH), BF16),
        scratch_shapes=[pltpu.VMEM((hps, tile, 1), F32),
                        pltpu.VMEM((hps, tile, 1), F32),
                        pltpu.VMEM((hps, tile, FOX_HEAD_DIM), F32)],
        compiler_params=_params(("arbitrary", "arbitrary", "arbitrary"), 32),
        name="fox_attention",
    )(proj3, proj3, proj3, c_col, c_row)


def _ssd_dt_body(raw_ref, bias_ref, alog_ref, dt_ref, acs_ref):
    z = raw_ref[...] + bias_ref[...]
    dt = jnp.maximum(z, 0.0) + jnp.log1p(jnp.exp(-jnp.abs(z)))
    a_dt = dt * (-jnp.exp(alog_ref[...]))
    n = z.shape[0]
    r = lax.broadcasted_iota(I32, (n, n), 0)
    c = lax.broadcasted_iota(I32, (n, n), 1)
    tri = (c <= r).astype(F32)
    dt_ref[...] = dt
    acs_ref[...] = jnp.dot(tri, a_dt, preferred_element_type=F32,
                           precision=lax.Precision.HIGHEST)


def ssd_dt(raw, bias_pad, alog_pad):
    t, n = raw.shape
    spec = pl.BlockSpec((SSD_CHUNK, n), lambda i: (i, 0))
    vec = pl.BlockSpec((1, n), lambda i: (0, 0))
    return pl.pallas_call(
        _ssd_dt_body,
        grid=(t // SSD_CHUNK,),
        in_specs=[spec, vec, vec],
        out_specs=[spec, spec],
        out_shape=[jax.ShapeDtypeStruct((t, n), F32)] * 2,
        compiler_params=_params(("arbitrary",), 16),
        name="ssd_dt",
    )(raw, bias_pad.reshape(1, n), alog_pad.reshape(1, n))


def _silu(x):
    return x / (1.0 + jnp.exp(-x))


def _causal_conv_silu(u_ref, w_ref, b_ref):
    u = u_ref[0].astype(F32)
    w = w_ref[...]

    def conv(v, causal_rows):
        out = b_ref[...] + v * w[SSD_CONV - 1:SSD_CONV, :]
        for shift in range(1, SSD_CONV):
            prev = pltpu.roll(v, shift, axis=0)
            if causal_rows is not None:
                prev = jnp.where(causal_rows >= shift, prev, 0.0)
            out = out + prev * w[SSD_CONV - 1 - shift:SSD_CONV - shift, :]
        return out

    head = u[:SUBLANES]
    head_rows = lax.broadcasted_iota(I32, head.shape, 0)
    out = jnp.concatenate([conv(head, head_rows), conv(u, None)[SUBLANES:]], axis=0)
    return _silu(out)


def _pair_lanes(cols, j):
    rows = cols.shape[0]
    lane = lax.broadcasted_iota(I32, (rows, LANES), 1)
    lo = jnp.broadcast_to(cols[:, 2 * j:2 * j + 1], (rows, LANES))
    hi = jnp.broadcast_to(cols[:, 2 * j + 1:2 * j + 2], (rows, LANES))
    return jnp.where(lane < SSD_HEAD_DIM, lo, hi)


def _ssd_body(z_ref, x_ref, b_ref, c_ref, wx_ref, wb_ref, wc_ref,
              bx_ref, bb_ref, bc_ref, dtt_ref, acs_ref, acst_ref,
              dskip_ref, gn_ref, o_ref, xs_ref, bs_ref, cs_ref, st_ref):
    q = SSD_CHUNK
    xs_ref[...] = _causal_conv_silu(x_ref, wx_ref, bx_ref)
    bs_ref[...] = _causal_conv_silu(b_ref, wb_ref, bb_ref).astype(BF16)
    cs_ref[...] = _causal_conv_silu(c_ref, wc_ref, bc_ref).astype(BF16)
    st_ref[...] = jnp.zeros_like(st_ref)

    r = lax.broadcasted_iota(I32, (q, q), 0)
    c = lax.broadcasted_iota(I32, (q, q), 1)
    causal = c <= r
    lane = lax.broadcasted_iota(I32, (q, LANES), 1)
    first_head = lane < SSD_HEAD_DIM

    def chunk(ci, _):
        r0 = pl.multiple_of(ci * q, q)
        xc = xs_ref[pl.ds(r0, q), :]
        bm = bs_ref[pl.ds(r0, q), :]
        cm = cs_ref[pl.ds(r0, q), :]
        zc = z_ref[0, pl.ds(r0, q), :].astype(F32)
        dt_t = dtt_ref[0, 0, :, pl.ds(r0, q)]
        acs = acs_ref[0, 0, pl.ds(r0, q), :]
        acs_t = acst_ref[0, 0, :, pl.ds(r0, q)]
        exp_acs = jnp.exp(acs)
        chunk_decay = jnp.exp(acs[q - 1:q, :])
        w_diag_t = dt_t
        w_end_t = dt_t * jnp.exp(acs_t[:, q - 1:q] - acs_t)
        cb = lax.dot_general(cm, bm, (((1,), (1,)), ((), ())),
                             preferred_element_type=F32)
        bm_t = bm.astype(F32).T
        gated = []
        for j in range(SSD_GROUP_HEADS // 2):
            lanes = slice(j * LANES, (j + 1) * LANES)
            xp = xc[:, lanes]
            xp_b = xp.astype(BF16)
            y_heads = []
            s_heads = []
            for hh in range(2):
                hd = 2 * j + hh
                seg = jnp.broadcast_to(acs[:, hd:hd + 1], (q, q)) - acs_t[hd:hd + 1, :]
                decay = jnp.exp(jnp.where(causal, seg, -jnp.inf))
                mix = cb * decay * w_diag_t[hd:hd + 1, :]
                y_heads.append(jnp.dot(mix.astype(BF16), xp_b,
                                       preferred_element_type=F32))
                b_end = (bm_t * w_end_t[hd:hd + 1, :]).astype(BF16)
                s_heads.append(jnp.dot(b_end, xp_b, preferred_element_type=F32))
            y = jnp.where(first_head, y_heads[0], y_heads[1])
            state = st_ref[:, lanes]
            y = y + jnp.dot(cm, state.astype(BF16),
                            preferred_element_type=F32) * _pair_lanes(exp_acs, j)
            y = y + xp * dskip_ref[:, lanes]
            st_ref[:, lanes] = state * _pair_lanes(chunk_decay, j) + jnp.where(
                first_head, s_heads[0], s_heads[1])
            gated.append(y * _silu(zc[:, lanes]))
        gated = jnp.concatenate(gated, axis=1)
        inv = lax.rsqrt(jnp.mean(gated * gated, axis=-1, keepdims=True) + EPS)
        o_ref[0, pl.ds(r0, q), :] = (gated * inv * gn_ref[...]).astype(o_ref.dtype)
        return 0

    lax.fori_loop(0, x_ref.shape[1] // q, chunk, 0)


def ssd_core(proj3, conv_w, conv_b, dt_gt, acs_g, acs_gt, d_skip_exp, gnorm):
    b, l, _ = proj3.shape
    gw = SSD_GROUP_WIDTH
    n = SSD_STATE
    inner = SSD_GROUPS * gw
    x_blk0 = inner // gw
    bm_blk0 = 2 * inner // n
    cm_blk0 = bm_blk0 + SSD_GROUPS
    wb_blk0 = inner // n
    wc_blk0 = wb_blk0 + SSD_GROUPS
    hpg = SSD_GROUP_HEADS
    conv_b2 = conv_b.reshape(1, -1)
    return pl.pallas_call(
        _ssd_body,
        grid=(b, SSD_GROUPS),
        in_specs=[
            pl.BlockSpec((1, l, gw), lambda i, g: (i, 0, g)),
            pl.BlockSpec((1, l, gw), lambda i, g: (i, 0, x_blk0 + g)),
            pl.BlockSpec((1, l, n), lambda i, g: (i, 0, bm_blk0 + g)),
            pl.BlockSpec((1, l, n), lambda i, g: (i, 0, cm_blk0 + g)),
            pl.BlockSpec((SSD_CONV, gw), lambda i, g: (0, g)),
            pl.BlockSpec((SSD_CONV, n), lambda i, g: (0, wb_blk0 + g)),
            pl.BlockSpec((SSD_CONV, n), lambda i, g: (0, wc_blk0 + g)),
            pl.BlockSpec((1, gw), lambda i, g: (0, g)),
            pl.BlockSpec((1, n), lambda i, g: (0, wb_blk0 + g)),
            pl.BlockSpec((1, n), lambda i, g: (0, wc_blk0 + g)),
            pl.BlockSpec((1, 1, hpg, l), lambda i, g: (i, g, 0, 0)),
            pl.BlockSpec((1, 1, l, hpg), lambda i, g: (i, g, 0, 0)),
            pl.BlockSpec((1, 1, hpg, l), lambda i, g: (i, g, 0, 0)),
            pl.BlockSpec((1, gw), lambda i, g: (0, g)),
            pl.BlockSpec((1, gw), lambda i, g: (0, g)),
        ],
        out_specs=pl.BlockSpec((1, l, gw), lambda i, g: (i, 0, g)),
        out_shape=jax.ShapeDtypeStruct((b, l, inner), BF16),
        scratch_shapes=[pltpu.VMEM((l, gw), F32), pltpu.VMEM((l, n), BF16),
                        pltpu.VMEM((l, n), BF16), pltpu.VMEM((n, gw), F32)],
        compiler_params=_params(("arbitrary", "arbitrary"), 48),
        name="ssd_core",
    )(proj3, proj3, proj3, proj3, conv_w, conv_w, conv_w, conv_b2, conv_b2,
      conv_b2, dt_gt, acs_g, acs_gt, d_skip_exp, gnorm.reshape(1, inner))


def _first_argmax(vals, nrows):
    row = lax.broadcasted_iota(I32, vals.shape, 0)
    top = jnp.max(vals, axis=0, keepdims=True)
    idx = jnp.min(jnp.where(vals == top, row, nrows), axis=0, keepdims=True)
    return top, idx, row


def _pack_bf16_halves(y):
    half = y.shape[1] // 2

    def bits(v):
        return lax.bitcast_convert_type(v.astype(BF16).astype(F32), U32)

    return bits(y[:, half:]) | (bits(y[:, :half]) >> 16)


def _unpack_bf16_halves(w):
    lo = lax.bitcast_convert_type(w << 16, F32)
    hi = lax.bitcast_convert_type(w & jnp.uint32(0xFFFF0000), F32)
    return lo, hi


def _router_body(x_ref, g_ref, rt_ref, rb_ref, eid_ref, wt_ref, rank_ref,
                 cnt_ref, hp_ref, carry_ref):
    step = pl.program_id(0)

    @pl.when(step == 0)
    def _():
        carry_ref[...] = jnp.zeros_like(carry_ref)

    x = x_ref[...]
    tm = x.shape[0]
    h = x * lax.rsqrt(jnp.mean(x * x, axis=-1, keepdims=True) + EPS) * g_ref[...]
    hp_ref[...] = _pack_bf16_halves(h)
    logits = lax.dot_general(rt_ref[...], h, (((1,), (1,)), ((), ())),
                             preferred_element_type=F32,
                             precision=lax.Precision.HIGHEST) + rb_ref[...]
    e_logits = logits[:MOE_EXPERTS]
    g_logits = logits[MOE_EXPERTS:MOE_EXPERTS + MOE_GROUPS]
    g_max, g_sel, _ = _first_argmax(g_logits, MOE_GROUPS)
    g_w = 1.0 / jnp.sum(jnp.exp(g_logits - g_max), axis=0, keepdims=True)
    sel = jnp.zeros((MOE_PER_GROUP, tm), F32)
    for grp in range(MOE_GROUPS):
        sel = jnp.where(g_sel == grp,
                        e_logits[grp * MOE_PER_GROUP:(grp + 1) * MOE_PER_GROUP], sel)
    m1, i1, row8 = _first_argmax(sel, MOE_PER_GROUP)
    rest = jnp.where(row8 == i1, -jnp.inf, sel)
    m2, i2, _ = _first_argmax(rest, MOE_PER_GROUP)
    p2 = jnp.exp(m2 - m1)
    w1 = g_w / (1.0 + p2)
    w2 = g_w * p2 / (1.0 + p2)
    e1 = g_sel * MOE_PER_GROUP + i1
    e2 = g_sel * MOE_PER_GROUP + i2

    r = lax.broadcasted_iota(I32, (tm, tm), 0)
    c = lax.broadcasted_iota(I32, (tm, tm), 1)
    upper = jnp.where(r <= c, 1.0, 0.0).astype(BF16)
    row32 = lax.broadcasted_iota(I32, (MOE_EXPERTS, tm), 0)
    hit1 = row32 == e1
    hit2 = row32 == e2
    cum1 = jnp.dot(jnp.where(hit1, 1.0, 0.0).astype(BF16), upper,
                   preferred_element_type=F32)
    cum2 = jnp.dot(jnp.where(hit2, 1.0, 0.0).astype(BF16), upper,
                   preferred_element_type=F32)
    carry = carry_ref[...]
    tot1 = cum1[:, tm - 1:tm]
    tot2 = cum2[:, tm - 1:tm]
    rank1 = jnp.sum(jnp.where(hit1, carry + cum1 - 1.0, 0.0), axis=0, keepdims=True)
    rank2 = jnp.sum(jnp.where(hit2, carry + tot1 + cum2 - 1.0, 0.0), axis=0,
                    keepdims=True)
    new_carry = carry + tot1 + tot2
    carry_ref[...] = new_carry

    eid_ref[...] = jnp.concatenate([e1, e2], axis=0)
    wt_ref[...] = jnp.concatenate([w1, w2], axis=0)
    rank_ref[...] = jnp.concatenate([rank1, rank2], axis=0).astype(I32)
    cnt_ref[...] = jnp.broadcast_to(new_carry, cnt_ref.shape).astype(I32)


def moe_router(x2d, g, router_g, router_g_b, router_e, router_e_b, tm=512):
    t, d = x2d.shape
    pad = 8 - MOE_GROUPS
    rt = jnp.concatenate([router_e.T, router_g.T, jnp.zeros((pad, d), F32)], axis=0)
    rb = jnp.concatenate([router_e_b, router_g_b, jnp.zeros((pad,), F32)]).reshape(-1, 1)
    nr = rt.shape[0]
    tok = pl.BlockSpec((2, tm), lambda i: (0, i))
    return pl.pallas_call(
        _router_body,
        grid=(t // tm,),
        in_specs=[pl.BlockSpec((tm, d), lambda i: (i, 0)),
                  pl.BlockSpec((1, d), lambda i: (0, 0)),
                  pl.BlockSpec((nr, d), lambda i: (0, 0)),
                  pl.BlockSpec((nr, 1), lambda i: (0, 0))],
        out_specs=[tok, tok, tok,
                   pl.BlockSpec((MOE_EXPERTS, LANES), lambda i: (0, 0)),
                   pl.BlockSpec((tm, d // 2), lambda i: (i, 0))],
        out_shape=[jax.ShapeDtypeStruct((2, t), I32),
                   jax.ShapeDtypeStruct((2, t), F32),
                   jax.ShapeDtypeStruct((2, t), I32),
                   jax.ShapeDtypeStruct((MOE_EXPERTS, LANES), I32),
                   jax.ShapeDtypeStruct((t, d // 2), U32)],
        scratch_shapes=[pltpu.VMEM((MOE_EXPERTS, 1), F32)],
        compiler_params=_params(("arbitrary",), 40),
        name="moe_router",
    )(x2d, g.reshape(1, d), rt, rb)


def _row_tokens_body(dest_ref, tok_ref):
    n_rows = tok_ref.shape[0]
    t = dest_ref.shape[0] // 2

    def clear(r, _):
        tok_ref[r] = 0
        return 0

    lax.fori_loop(0, n_rows, clear, 0, unroll=8)

    def fill(tok, _):
        tok_ref[dest_ref[tok]] = tok
        tok_ref[dest_ref[t + tok]] = tok
        return 0

    lax.fori_loop(0, t, fill, 0, unroll=8)


def moe_row_tokens(dest_flat, n_rows):
    return pl.pallas_call(
        _row_tokens_body,
        grid_spec=pltpu.PrefetchScalarGridSpec(
            num_scalar_prefetch=1,
            grid=(1,),
            in_specs=[],
            out_specs=pl.BlockSpec(memory_space=pltpu.SMEM),
        ),
        out_shape=jax.ShapeDtypeStruct((n_rows,), I32),
        compiler_params=pltpu.CompilerParams(dimension_semantics=("arbitrary",)),
        name="moe_row_tokens",
    )(dest_flat)


def _experts_body(blk_e_ref, n_used_ref, next_e_ref, grp_ref, tok_ref,
                  hp_hbm, wg_hbm, wu_hbm, wd_hbm, o_ref,
                  xbuf_ref, xsem, wg_st, wu_st, wd_st, wsem,
                  wgb_ref, wub_ref, wdb_ref, *, layer):
    i = pl.program_id(0)
    n_used = n_used_ref[0]
    used = i < n_used
    e_cur = blk_e_ref[i]
    prev = blk_e_ref[jnp.maximum(i - 1, 0)]
    fresh = jnp.logical_and(used, jnp.logical_or(i == 0, e_cur != prev))
    rows = xbuf_ref.shape[1]

    def weight_copies(e, slot):
        return (pltpu.make_async_copy(wg_hbm.at[layer, e], wg_st.at[slot],
                                      wsem.at[0, slot]),
                pltpu.make_async_copy(wu_hbm.at[layer, e], wu_st.at[slot],
                                      wsem.at[1, slot]),
                pltpu.make_async_copy(wd_hbm.at[layer, e], wd_st.at[slot],
                                      wsem.at[2, slot]))

    def row_copy(blk, slot, r):
        tok = tok_ref[blk * rows + r]
        return pltpu.make_async_copy(hp_hbm.at[pl.ds(tok, 1)],
                                     xbuf_ref.at[slot, pl.ds(r, 1)],
                                     xsem.at[slot])

    @pl.when(i == 0)
    def _():
        for cp in weight_copies(e_cur, 0):
            cp.start(priority=WEIGHT_DMA_PRIORITY)

        def body(r, _):
            row_copy(0, 0, r).start()
            return 0
        lax.fori_loop(0, rows, body, 0, unroll=8)

    wslot = grp_ref[i] % 2
    nxt = next_e_ref[i]

    @pl.when(jnp.logical_and(fresh, nxt >= 0))
    def _():
        for cp in weight_copies(nxt, 1 - wslot):
            cp.start(priority=WEIGHT_DMA_PRIORITY)

    @pl.when(fresh)
    def _():
        for cp in weight_copies(e_cur, wslot):
            cp.wait()
        wgb_ref[...] = wg_st[wslot].astype(BF16)
        wub_ref[...] = wu_st[wslot].astype(BF16)
        wdb_ref[...] = wd_st[wslot].astype(BF16)

    def compute(gather_next):
        slot = i % 2
        pltpu.make_async_copy(hp_hbm.at[pl.ds(0, rows)], xbuf_ref.at[slot],
                              xsem.at[slot]).wait()
        if gather_next:
            for r in range(rows):
                row_copy(i + 1, 1 - slot, r).start()
        lo, hi = _unpack_bf16_halves(xbuf_ref[slot])
        h = jnp.concatenate([lo.astype(BF16), hi.astype(BF16)], axis=1)
        gate = jnp.dot(h, wgb_ref[...], preferred_element_type=F32)
        up = jnp.dot(h, wub_ref[...], preferred_element_type=F32)
        act = (_silu(gate) * up).astype(BF16)
        y = jnp.dot(act, wdb_ref[...], preferred_element_type=F32)
        o_ref[...] = _pack_bf16_halves(y)

    has_next = i + 1 < n_used

    @pl.when(has_next)
    def _():
        compute(True)

    @pl.when(jnp.logical_and(used, jnp.logical_not(has_next)))
    def _():
        compute(False)

    @pl.when(jnp.logical_not(used))
    def _():
        o_ref[...] = jnp.zeros_like(o_ref)


def moe_experts(blk_e, n_used, next_e, grp, row_tok, hp, w_gate, w_up, w_down,
                layer):
    n_rows = row_tok.shape[0]
    half = hp.shape[1]
    d = 2 * half
    ff = w_gate.shape[3]
    nblk = n_rows // MOE_BLOCK
    hbm = pl.BlockSpec(memory_space=pl.ANY)
    return pl.pallas_call(
        functools.partial(_experts_body, layer=layer),
        grid_spec=pltpu.PrefetchScalarGridSpec(
            num_scalar_prefetch=5,
            grid=(nblk,),
            in_specs=[hbm, hbm, hbm, hbm],
            out_specs=pl.BlockSpec((MOE_BLOCK, half), lambda i, *_: (i, 0)),
            scratch_shapes=[pltpu.VMEM((2, MOE_BLOCK, half), U32),
                            pltpu.SemaphoreType.DMA((2,)),
                            pltpu.VMEM((2, d, ff), F32), pltpu.VMEM((2, d, ff), F32),
                            pltpu.VMEM((2, ff, d), F32),
                            pltpu.SemaphoreType.DMA((3, 2)),
                            pltpu.VMEM((d, ff), BF16), pltpu.VMEM((d, ff), BF16),
                            pltpu.VMEM((ff, d), BF16)],
        ),
        out_shape=jax.ShapeDtypeStruct((n_rows, half), U32),
        compiler_params=_params(("arbitrary",), 56),
        name="moe_experts",
    )(blk_e, n_used, next_e, grp, row_tok, hp, w_gate, w_up, w_down)


def _combine_body(dest_ref, x_ref, wt_ref, g_ref, ys_ref, o_ref, buf_ref, sem,
                  *, final_norm):
    i = pl.program_id(0)
    n = pl.num_programs(0)
    tm = x_ref.shape[0]
    t = tm * n

    def issue(tile, slot):
        def body(r, _):
            for k in range(2):
                d = dest_ref[k * t + tile * tm + r]
                pltpu.make_async_copy(ys_ref.at[pl.ds(d, 1)],
                                      buf_ref.at[slot, k, pl.ds(r, 1)],
                                      sem.at[slot]).start()
            return 0
        lax.fori_loop(0, tm, body, 0, unroll=8)

    @pl.when(i == 0)
    def _():
        issue(0, 0)

    @pl.when(i + 1 < n)
    def _():
        issue(i + 1, (i + 1) % 2)

    slot = i % 2
    for k in range(2):
        pltpu.make_async_copy(ys_ref.at[pl.ds(0, tm)], buf_ref.at[slot, k],
                              sem.at[slot]).wait()
    w = wt_ref[...]
    lo1, hi1 = _unpack_bf16_halves(buf_ref[slot, 0])
    lo2, hi2 = _unpack_bf16_halves(buf_ref[slot, 1])
    half = lo1.shape[1]
    x = x_ref[...]
    y_lo = x[:, :half] + w[:, 0:1] * lo1 + w[:, 1:2] * lo2
    y_hi = x[:, half:] + w[:, 0:1] * hi1 + w[:, 1:2] * hi2
    if final_norm:
        ssq = (jnp.sum(y_lo * y_lo, axis=-1, keepdims=True)
               + jnp.sum(y_hi * y_hi, axis=-1, keepdims=True))
        inv = lax.rsqrt(ssq / (2 * half) + EPS)
        g = g_ref[...]
        y_lo = y_lo * inv * g[:, :half]
        y_hi = y_hi * inv * g[:, half:]
    o_ref[:, :half] = y_lo
    o_ref[:, half:] = y_hi


def moe_combine(dest_flat, x2d, wt_t, ys, g_final, final_norm, tm=128):
    t, d = x2d.shape
    return pl.pallas_call(
        functools.partial(_combine_body, final_norm=final_norm),
        grid_spec=pltpu.PrefetchScalarGridSpec(
            num_scalar_prefetch=1,
            grid=(t // tm,),
            in_specs=[pl.BlockSpec((tm, d), lambda i, ds: (i, 0)),
                      pl.BlockSpec((tm, 2), lambda i, ds: (i, 0)),
                      pl.BlockSpec((1, d), lambda i, ds: (0, 0)),
                      pl.BlockSpec(memory_space=pl.ANY)],
            out_specs=pl.BlockSpec((tm, d), lambda i, ds: (i, 0)),
            scratch_shapes=[pltpu.VMEM((2, 2, tm, d // 2), U32),
                            pltpu.SemaphoreType.DMA((2,))],
        ),
        out_shape=jax.ShapeDtypeStruct((t, d), F32),
        compiler_params=_params(("arbitrary",), 32),
        name="moe_combine",
    )(dest_flat, x2d, wt_t, g_final.reshape(1, d), ys)


def hier_moe(x2d, g_ffn, router_g, router_g_b, router_e, router_e_b,
             w_gate, w_up, w_down, layer, g_final, final_norm):
    t, d = x2d.shape
    eid, wts, rank, cnt, hp = moe_router(x2d, g_ffn, router_g, router_g_b,
                                         router_e, router_e_b)
    counts = cnt[:, 0]
    padded = (counts + MOE_BLOCK - 1) // MOE_BLOCK * MOE_BLOCK
    pend = jnp.cumsum(padded)
    pstart = pend - padded
    n_rows = 2 * t + MOE_EXPERTS * MOE_BLOCK
    nblk = n_rows // MOE_BLOCK
    expert_ids = jnp.arange(MOE_EXPERTS, dtype=I32)
    start_of = jnp.sum(jnp.where(eid[..., None] == expert_ids, pstart, 0), axis=-1)
    dest = (start_of + rank).reshape(-1)
    blk_start = jnp.arange(nblk, dtype=I32) * MOE_BLOCK
    blk_e = jnp.sum(blk_start[:, None] >= pend[None, :], axis=1).astype(I32)
    blk_e = jnp.minimum(blk_e, MOE_EXPERTS - 1)
    n_used = (pend[-1:] // MOE_BLOCK).astype(I32)
    last_e = blk_e[jnp.maximum(n_used[0] - 1, 0)]
    blk_e = jnp.where(jnp.arange(nblk) < n_used[0], blk_e, last_e)
    changed = jnp.concatenate([jnp.ones((1,), I32),
                               (blk_e[1:] != blk_e[:-1]).astype(I32)])
    grp = jnp.cumsum(changed) - 1
    grp_end = jnp.sum(jnp.where(blk_e[:, None] == expert_ids, pend, 0), axis=-1)
    nxt_blk = grp_end // MOE_BLOCK
    nxt_e = jnp.sum(jnp.where(nxt_blk[:, None] == jnp.arange(nblk), blk_e, 0), axis=-1)
    next_e = jnp.where(nxt_blk < n_used[0], nxt_e, -1).astype(I32)
    row_tok = moe_row_tokens(dest, n_rows)
    ys = moe_experts(blk_e, n_used, next_e, grp.astype(I32), row_tok, hp,
                     w_gate, w_up, w_down, layer)
    return moe_combine(dest, x2d, wts.T, ys, g_final, final_norm)


def _pad_cols(w, n):
    return jnp.pad(w, ((0, 0), (0, n - w.shape[1])))


def pool_fox_layer(x2d, b, l, g_mix, w_in, b_forget, pool_w, pool_scale, w_out):
    h = rmsnorm(x2d, g_mix, BF16)
    main = POOL_WIDTH + 3 * FOX_WIDTH
    col = jnp.arange(main)
    is_q = jnp.logical_and(col >= POOL_WIDTH, col < POOL_WIDTH + FOX_WIDTH)
    q_scale = jnp.where(is_q, FOX_Q_SCALE, 1.0).astype(F32)
    proj = matmul(h, w_in, main, col_scale=q_scale, name="pf_in_proj")
    f = matmul(h, _pad_cols(w_in[:, main:], LANES), LANES, out_dtype=F32,
               name="pf_forget_proj")
    f_t = f[:, :FOX_HEADS].reshape(b, l, FOX_HEADS).transpose(0, 2, 1)
    c = forget_cumsum(f_t, b_forget)
    proj3 = proj.reshape(b, l, main)
    y_pool = pool_mixer(proj3, pool_w, pool_scale)
    y_att = fox_attention(proj3, c, c.transpose(0, 2, 1))
    y = jnp.concatenate([y_pool, y_att], axis=-1).reshape(b * l, -1)
    return matmul(y, w_out, w_out.shape[1], res=x2d, out_dtype=F32, tm=512,
                  name="pf_out_proj")


def ssd_layer(x2d, b, l, g_mix, w_in, conv_w, conv_b, dt_bias, a_log, d_skip,
              gnorm, w_out):
    h = rmsnorm(x2d, g_mix, BF16)
    heads = dt_bias.shape[0]
    inner = heads * SSD_HEAD_DIM
    main = 2 * inner + 2 * SSD_GROUPS * SSD_STATE
    proj = matmul(h, w_in, main, name="ssd_in_proj")
    raw = matmul(h, _pad_cols(w_in[:, main:], LANES), LANES, out_dtype=F32,
                 name="ssd_dt_proj")
    dt, acs = ssd_dt(raw, jnp.pad(dt_bias, (0, LANES - heads)),
                     jnp.pad(a_log, (0, LANES - heads)))
    hpg = SSD_GROUP_HEADS

    def by_group(v):
        return v[:, :heads].reshape(b, l, SSD_GROUPS, hpg).transpose(0, 2, 1, 3)

    dt_gt = by_group(dt).transpose(0, 1, 3, 2)
    acs_g = by_group(acs)
    acs_gt = acs_g.transpose(0, 1, 3, 2)
    d_skip_exp = jnp.repeat(d_skip, SSD_HEAD_DIM).reshape(1, inner)
    y = ssd_core(proj.reshape(b, l, main), conv_w, conv_b, dt_gt, acs_g, acs_gt,
                 d_skip_exp, gnorm)
    return matmul(y.reshape(b * l, inner), w_out, w_out.shape[1], res=x2d,
                  out_dtype=F32, tm=512, tn=512, name="ssd_out_proj")


def kernel(x, norm_mix, norm_ffn, norm_final, pf_w_in, pf_b_forget, pf_pool_w, pf_pool_scale, pf_w_out, ssd_w_in, ssd_conv_w, ssd_conv_b, ssd_dt_bias, ssd_a_log, ssd_d_skip, ssd_gnorm, ssd_w_out, moe_router_g, moe_router_g_b, moe_router_e, moe_router_e_b, moe_w_gate, moe_w_up, moe_w_down):
    b, l, d = x.shape
    depth = norm_mix.shape[0]
    x2d = x.reshape(b * l, d)
    for i in range(depth):
        j = i // 2
        if i % 2 == 0:
            x2d = pool_fox_layer(x2d, b, l, norm_mix[i], pf_w_in[j], pf_b_forget[j],
                                 pf_pool_w[j], pf_pool_scale[j], pf_w_out[j])
        else:
            x2d = ssd_layer(x2d, b, l, norm_mix[i], ssd_w_in[j], ssd_conv_w[j],
                            ssd_conv_b[j], ssd_dt_bias[j], ssd_a_log[j],
                            ssd_d_skip[j], ssd_gnorm[j], ssd_w_out[j])
        last = i == depth - 1
        x2d = hier_moe(x2d, norm_ffn[i], moe_router_g[i], moe_router_g_b[i],
                       moe_router_e[i], moe_router_e_b[i], moe_w_gate,
                       moe_w_up, moe_w_down, i, norm_final, last)
    return x2d.reshape(b, l, d)
```

```python
import functools

import jax
import jax.numpy as jnp
from jax import lax
from jax.experimental import pallas as pl
from jax.experimental.pallas import tpu as pltpu

F32 = jnp.float32
BF16 = jnp.bfloat16
I32 = jnp.int32
U32 = jnp.uint32
EPS = 1e-6

POOL_WINDOWS = (2, 4, 8, 16)
POOL_GROUP = 256
POOL_WIDTH = 1024
FOX_HEADS = 8
FOX_HEAD_DIM = 128
FOX_WIDTH = 1024
SSD_HEAD_DIM = 64
SSD_STATE = 128
SSD_GROUPS = 8
SSD_GROUP_HEADS = 8
SSD_GROUP_WIDTH = SSD_GROUP_HEADS * SSD_HEAD_DIM
SSD_CONV = 4
SSD_CHUNK = 128
MOE_GROUPS = 4
MOE_PER_GROUP = 8
MOE_EXPERTS = 32
MOE_BLOCK = 256
WEIGHT_DMA_PRIORITY = 1
ROW_SLOTS = 3
LANES = 128
SUBLANES = 8
MIB = 1 << 20


def _params(semantics, vmem_mib):
    return pltpu.CompilerParams(dimension_semantics=semantics,
                                vmem_limit_bytes=vmem_mib * MIB)


def _rmsnorm_body(x_ref, g_ref, o_ref):
    x = x_ref[...]
    inv = lax.rsqrt(jnp.mean(x * x, axis=-1, keepdims=True) + EPS)
    o_ref[...] = (x * inv * g_ref[...]).astype(o_ref.dtype)


def rmsnorm(x2d, g, out_dtype, tm=512):
    t, d = x2d.shape
    return pl.pallas_call(
        _rmsnorm_body,
        grid=(t // tm,),
        in_specs=[pl.BlockSpec((tm, d), lambda i: (i, 0)),
                  pl.BlockSpec((1, d), lambda i: (0, 0))],
        out_specs=pl.BlockSpec((tm, d), lambda i: (i, 0)),
        out_shape=jax.ShapeDtypeStruct((t, d), out_dtype),
        compiler_params=_params(("arbitrary",), 40),
        name="rmsnorm",
    )(x2d, g.reshape(1, d))


def _matmul_body(*refs, has_res, has_scale):
    refs = list(refs)
    a_ref, w_ref = refs[:2]
    o_ref, wb_ref = refs[-2:]
    s_ref = refs[2] if has_scale else None
    r_ref = refs[-3] if has_res else None

    @pl.when(pl.program_id(1) == 0)
    def _():
        w = w_ref[...]
        if has_scale:
            w = w * s_ref[...]
        wb_ref[...] = w.astype(BF16)

    acc = jnp.dot(a_ref[...], wb_ref[...], preferred_element_type=F32)
    if has_res:
        acc = acc + r_ref[...]
    o_ref[...] = acc.astype(o_ref.dtype)


def matmul(a, w, n_out, *, col0=0, res=None, col_scale=None, out_dtype=BF16,
           tm=1024, tn=1024, vmem_mib=48, name="matmul"):
    t, k = a.shape
    tn = min(tn, n_out)
    tm = min(tm, t)
    assert t % tm == 0 and n_out % tn == 0 and col0 % tn == 0
    cb0 = col0 // tn
    in_specs = [pl.BlockSpec((tm, k), lambda j, i: (i, 0)),
                pl.BlockSpec((k, tn), lambda j, i: (0, j + cb0))]
    args = [a, w]
    if col_scale is not None:
        in_specs.append(pl.BlockSpec((1, tn), lambda j, i: (0, j)))
        args.append(col_scale.reshape(1, n_out))
    if res is not None:
        in_specs.append(pl.BlockSpec((tm, tn), lambda j, i: (i, j)))
        args.append(res)
    return pl.pallas_call(
        functools.partial(_matmul_body, has_res=res is not None,
                          has_scale=col_scale is not None),
        grid=(n_out // tn, t // tm),
        in_specs=in_specs,
        out_specs=pl.BlockSpec((tm, tn), lambda j, i: (i, j)),
        out_shape=jax.ShapeDtypeStruct((t, n_out), out_dtype),
        scratch_shapes=[pltpu.VMEM((k, tn), BF16)],
        compiler_params=_params(("arbitrary", "arbitrary"), vmem_mib),
        name=name,
    )(*args)


def _forget_cumsum_body(f_ref, b_ref, c_ref):
    z = f_ref[0] + b_ref[...]
    x = jnp.minimum(z, 0.0) - jnp.log1p(jnp.exp(-jnp.abs(z)))
    n = x.shape[1]
    lane = lax.broadcasted_iota(I32, x.shape, 1)
    shift = 1
    while shift < n:
        x = x + jnp.where(lane >= shift, pltpu.roll(x, shift, axis=1), 0.0)
        shift *= 2
    c_ref[0] = x


def forget_cumsum(f_t, b_forget):
    b, h, l = f_t.shape
    return pl.pallas_call(
        _forget_cumsum_body,
        grid=(b,),
        in_specs=[pl.BlockSpec((1, h, l), lambda i: (i, 0, 0)),
                  pl.BlockSpec((h, 1), lambda i: (0, 0))],
        out_specs=pl.BlockSpec((1, h, l), lambda i: (i, 0, 0)),
        out_shape=jax.ShapeDtypeStruct((b, h, l), F32),
        compiler_params=_params(("arbitrary",), 16),
        name="forget_cumsum",
    )(f_t, b_forget.reshape(h, 1))


def _pool_body(u_ref, w_ref, s_ref, o_ref):
    g = pl.program_id(1)
    u = u_ref[0].astype(F32)
    row = lax.broadcasted_iota(I32, u.shape, 0)
    acc = u
    sums = []
    for shift in (1, 2, 4, 8):
        acc = acc + jnp.where(row >= shift, pltpu.roll(acc, shift, axis=0), 0.0)
        sums.append(acc)
    win_sum = jnp.where(g == 0, sums[0],
                        jnp.where(g == 1, sums[1],
                                  jnp.where(g == 2, sums[2], sums[3])))
    window = jnp.left_shift(jnp.int32(2), g)
    count = jnp.minimum(row + 1, window).astype(F32)
    mixed = win_sum / count - u
    y = jnp.dot(mixed.astype(BF16), w_ref[0].astype(BF16),
                preferred_element_type=F32)
    o_ref[0] = (y * s_ref[...]).astype(o_ref.dtype)


def pool_mixer(proj3, pool_w, pool_scale):
    b, l, _ = proj3.shape
    ng = len(POOL_WINDOWS)
    return pl.pallas_call(
        _pool_body,
        grid=(b, ng),
        in_specs=[pl.BlockSpec((1, l, POOL_GROUP), lambda i, g: (i, 0, g)),
                  pl.BlockSpec((1, POOL_GROUP, POOL_GROUP), lambda i, g: (g, 0, 0)),
                  pl.BlockSpec((1, POOL_GROUP), lambda i, g: (0, g))],
        out_specs=pl.BlockSpec((1, l, POOL_GROUP), lambda i, g: (i, 0, g)),
        out_shape=jax.ShapeDtypeStruct((b, l, POOL_WIDTH), BF16),
        compiler_params=_params(("arbitrary", "arbitrary"), 40),
        name="pool_mixer",
    )(proj3, pool_w, pool_scale.reshape(1, POOL_WIDTH))


LOG2E = 1.4426950408889634
FOX_Q_SCALE = FOX_HEAD_DIM ** -0.5 * LOG2E


FOX_HEADS_PER_STEP = 2


def _fox_body(q_ref, k_ref, v_ref, cq_ref, ck_ref, o_ref, m_ref, l_ref, acc_ref,
              *, tile):
    hp = pl.program_id(1)
    qi = pl.program_id(2)
    hd = FOX_HEAD_DIM
    cq8 = cq_ref[0]
    head_lane = lax.broadcasted_iota(I32, cq8.shape, 1)
    heads = []
    for hh in range(FOX_HEADS_PER_STEP):
        head = hp * FOX_HEADS_PER_STEP + hh
        cq = jnp.sum(jnp.where(head_lane == head, cq8, 0.0), axis=1,
                     keepdims=True) * LOG2E
        heads.append((hh, head, slice(hh * hd, (hh + 1) * hd), cq))

    def logits(j, head, cols):
        start = pl.multiple_of(j * tile, tile)
        k = k_ref[0, pl.ds(start, tile), cols]
        ck = ck_ref[0, pl.ds(head, 1), pl.ds(start, tile)] * LOG2E
        s = lax.dot_general(q_ref[0, :, cols], k, (((1,), (1,)), ((), ())),
                            preferred_element_type=F32)
        return s - ck, start

    for hh, head, cols, cq in heads:
        s, start = logits(qi, head, cols)
        r = lax.broadcasted_iota(I32, s.shape, 0)
        c = lax.broadcasted_iota(I32, s.shape, 1)
        s = jnp.where(c <= r, s, -jnp.inf)
        m0 = jnp.max(s, axis=1, keepdims=True) + cq
        p = jnp.exp2(s + (cq - m0))
        m_ref[hh] = m0
        l_ref[hh] = jnp.sum(p, axis=1, keepdims=True)
        acc_ref[hh] = jnp.dot(p.astype(BF16), v_ref[0, pl.ds(start, tile), cols],
                              preferred_element_type=F32)

    def step(j, _):
        for hh, head, cols, cq in heads:
            s, start = logits(j, head, cols)
            m_old = m_ref[hh]
            m_new = jnp.maximum(m_old, jnp.max(s, axis=1, keepdims=True) + cq)
            alpha = jnp.exp2(m_old - m_new)
            p = jnp.exp2(s + (cq - m_new))
            l_ref[hh] = alpha * l_ref[hh] + jnp.sum(p, axis=1, keepdims=True)
            acc_ref[hh] = alpha * acc_ref[hh] + jnp.dot(
                p.astype(BF16), v_ref[0, pl.ds(start, tile), cols],
                preferred_element_type=F32)
            m_ref[hh] = m_new
        return 0

    lax.fori_loop(0, qi, step, 0)
    for hh, head, cols, cq in heads:
        o_ref[0, :, cols] = (acc_ref[hh] / l_ref[hh]).astype(o_ref.dtype)


def fox_attention(proj3, c_row, c_col, tile=512):
    b, l, _ = proj3.shape
    hps = FOX_HEADS_PER_STEP
    w = hps * FOX_HEAD_DIM
    q0 = POOL_WIDTH // w
    k0 = q0 + FOX_HEADS // hps
    v0 = k0 + FOX_HEADS // hps
    return pl.pallas_call(
        functools.partial(_fox_body, tile=tile),
        grid=(b, FOX_HEADS // hps, l // tile),
        in_specs=[
            pl.BlockSpec((1, tile, w), lambda i, h, q: (i, q, q0 + h)),
            pl.BlockSpec((1, l, w), lambda i, h, q: (i, 0, k0 + h)),
            pl.BlockSpec((1, l, w), lambda i, h, q: (i, 0, v0 + h)),
            pl.BlockSpec((1, tile, FOX_HEADS), lambda i, h, q: (i, q, 0)),
            pl.BlockSpec((1, FOX_HEADS, l), lambda i, h, q: (i, 0, 0)),
        ],
        out_specs=pl.BlockSpec((1, tile, w), lambda i, h, q: (i, q, h)),
        out_shape=jax.ShapeDtypeStruct((b, l, FOX_WIDTH), BF16),
        scratch_shapes=[pltpu.VMEM((hps, tile, 1), F32),
                        pltpu.VMEM((hps, tile, 1), F32),
                        pltpu.VMEM((hps, tile, FOX_HEAD_DIM), F32)],
        compiler_params=_params(("arbitrary", "arbitrary", "arbitrary"), 32),
        name="fox_attention",
    )(proj3, proj3, proj3, c_col, c_row)


def _ssd_dt_body(raw_ref, bias_ref, alog_ref, dt_ref, acs_ref):
    z = raw_ref[...] + bias_ref[...]
    dt = jnp.maximum(z, 0.0) + jnp.log1p(jnp.exp(-jnp.abs(z)))
    a_dt = dt * (-jnp.exp(alog_ref[...]))
    n = z.shape[0]
    r = lax.broadcasted_iota(I32, (n, n), 0)
    c = lax.broadcasted_iota(I32, (n, n), 1)
    tri = (c <= r).astype(F32)
    dt_ref[...] = dt
    acs_ref[...] = jnp.dot(tri, a_dt, preferred_element_type=F32,
                           precision=lax.Precision.HIGHEST)


def ssd_dt(raw, bias_pad, alog_pad):
    t, n = raw.shape
    spec = pl.BlockSpec((SSD_CHUNK, n), lambda i: (i, 0))
    vec = pl.BlockSpec((1, n), lambda i: (0, 0))
    return pl.pallas_call(
        _ssd_dt_body,
        grid=(t // SSD_CHUNK,),
        in_specs=[spec, vec, vec],
        out_specs=[spec, spec],
        out_shape=[jax.ShapeDtypeStruct((t, n), F32)] * 2,
        compiler_params=_params(("arbitrary",), 16),
        name="ssd_dt",
    )(raw, bias_pad.reshape(1, n), alog_pad.reshape(1, n))


def _silu(x):
    return x / (1.0 + jnp.exp(-x))


def _causal_conv_silu(u_ref, w_ref, b_ref):
    u = u_ref[0].astype(F32)
    w = w_ref[...]

    def conv(v, causal_rows):
        out = b_ref[...] + v * w[SSD_CONV - 1:SSD_CONV, :]
        for shift in range(1, SSD_CONV):
            prev = pltpu.roll(v, shift, axis=0)
            if causal_rows is not None:
                prev = jnp.where(causal_rows >= shift, prev, 0.0)
            out = out + prev * w[SSD_CONV - 1 - shift:SSD_CONV - shift, :]
        return out

    head = u[:SUBLANES]
    head_rows = lax.broadcasted_iota(I32, head.shape, 0)
    out = jnp.concatenate([conv(head, head_rows), conv(u, None)[SUBLANES:]], axis=0)
    return _silu(out)


def _pair_lanes(cols, j):
    rows = cols.shape[0]
    lane = lax.broadcasted_iota(I32, (rows, LANES), 1)
    lo = jnp.broadcast_to(cols[:, 2 * j:2 * j + 1], (rows, LANES))
    hi = jnp.broadcast_to(cols[:, 2 * j + 1:2 * j + 2], (rows, LANES))
    return jnp.where(lane < SSD_HEAD_DIM, lo, hi)


def _ssd_body(z_ref, x_ref, b_ref, c_ref, wx_ref, wb_ref, wc_ref,
              bx_ref, bb_ref, bc_ref, dtt_ref, acs_ref, acst_ref,
              dskip_ref, gn_ref, o_ref, xs_ref, bs_ref, cs_ref, st_ref):
    q = SSD_CHUNK
    xs_ref[...] = _causal_conv_silu(x_ref, wx_ref, bx_ref)
    bs_ref[...] = _causal_conv_silu(b_ref, wb_ref, bb_ref).astype(BF16)
    cs_ref[...] = _causal_conv_silu(c_ref, wc_ref, bc_ref).astype(BF16)
    st_ref[...] = jnp.zeros_like(st_ref)

    r = lax.broadcasted_iota(I32, (q, q), 0)
    c = lax.broadcasted_iota(I32, (q, q), 1)
    causal = c <= r
    lane = lax.broadcasted_iota(I32, (q, LANES), 1)
    first_head = lane < SSD_HEAD_DIM

    def chunk(ci, _):
        r0 = pl.multiple_of(ci * q, q)
        xc = xs_ref[pl.ds(r0, q), :]
        bm = bs_ref[pl.ds(r0, q), :]
        cm = cs_ref[pl.ds(r0, q), :]
        zc = z_ref[0, pl.ds(r0, q), :].astype(F32)
        dt_t = dtt_ref[0, 0, :, pl.ds(r0, q)]
        acs = acs_ref[0, 0, pl.ds(r0, q), :]
        acs_t = acst_ref[0, 0, :, pl.ds(r0, q)]
        exp_acs = jnp.exp(acs)
        chunk_decay = jnp.exp(acs[q - 1:q, :])
        w_diag_t = dt_t
        w_end_t = dt_t * jnp.exp(acs_t[:, q - 1:q] - acs_t)
        cb = lax.dot_general(cm, bm, (((1,), (1,)), ((), ())),
                             preferred_element_type=F32)
        bm_t = bm.astype(F32).T
        gated = []
        for j in range(SSD_GROUP_HEADS // 2):
            lanes = slice(j * LANES, (j + 1) * LANES)
            xp = xc[:, lanes]
            xp_b = xp.astype(BF16)
            y_heads = []
            s_heads = []
            for hh in range(2):
                hd = 2 * j + hh
                seg = jnp.broadcast_to(acs[:, hd:hd + 1], (q, q)) - acs_t[hd:hd + 1, :]
                decay = jnp.exp(jnp.where(causal, seg, -jnp.inf))
                mix = cb * decay * w_diag_t[hd:hd + 1, :]
                y_heads.append(jnp.dot(mix.astype(BF16), xp_b,
                                       preferred_element_type=F32))
                b_end = (bm_t * w_end_t[hd:hd + 1, :]).astype(BF16)
                s_heads.append(jnp.dot(b_end, xp_b, preferred_element_type=F32))
            y = jnp.where(first_head, y_heads[0], y_heads[1])
            state = st_ref[:, lanes]
            y = y + jnp.dot(cm, state.astype(BF16),
                            preferred_element_type=F32) * _pair_lanes(exp_acs, j)
            y = y + xp * dskip_ref[:, lanes]
            st_ref[:, lanes] = state * _pair_lanes(chunk_decay, j) + jnp.where(
                first_head, s_heads[0], s_heads[1])
            gated.append(y * _silu(zc[:, lanes]))
        gated = jnp.concatenate(gated, axis=1)
        inv = lax.rsqrt(jnp.mean(gated * gated, axis=-1, keepdims=True) + EPS)
        o_ref[0, pl.ds(r0, q), :] = (gated * inv * gn_ref[...]).astype(o_ref.dtype)
        return 0

    lax.fori_loop(0, x_ref.shape[1] // q, chunk, 0)


def ssd_core(proj3, conv_w, conv_b, dt_gt, acs_g, acs_gt, d_skip_exp, gnorm):
    b, l, _ = proj3.shape
    gw = SSD_GROUP_WIDTH
    n = SSD_STATE
    inner = SSD_GROUPS * gw
    x_blk0 = inner // gw
    bm_blk0 = 2 * inner // n
    cm_blk0 = bm_blk0 + SSD_GROUPS
    wb_blk0 = inner // n
    wc_blk0 = wb_blk0 + SSD_GROUPS
    hpg = SSD_GROUP_HEADS
    conv_b2 = conv_b.reshape(1, -1)
    return pl.pallas_call(
        _ssd_body,
        grid=(b, SSD_GROUPS),
        in_specs=[
            pl.BlockSpec((1, l, gw), lambda i, g: (i, 0, g)),
            pl.BlockSpec((1, l, gw), lambda i, g: (i, 0, x_blk0 + g)),
            pl.BlockSpec((1, l, n), lambda i, g: (i, 0, bm_blk0 + g)),
            pl.BlockSpec((1, l, n), lambda i, g: (i, 0, cm_blk0 + g)),
            pl.BlockSpec((SSD_CONV, gw), lambda i, g: (0, g)),
            pl.BlockSpec((SSD_CONV, n), lambda i, g: (0, wb_blk0 + g)),
            pl.BlockSpec((SSD_CONV, n), lambda i, g: (0, wc_blk0 + g)),
            pl.BlockSpec((1, gw), lambda i, g: (0, g)),
            pl.BlockSpec((1, n), lambda i, g: (0, wb_blk0 + g)),
            pl.BlockSpec((1, n), lambda i, g: (0, wc_blk0 + g)),
            pl.BlockSpec((1, 1, hpg, l), lambda i, g: (i, g, 0, 0)),
            pl.BlockSpec((1, 1, l, hpg), lambda i, g: (i, g, 0, 0)),
            pl.BlockSpec((1, 1, hpg, l), lambda i, g: (i, g, 0, 0)),
            pl.BlockSpec((1, gw), lambda i, g: (0, g)),
            pl.BlockSpec((1, gw), lambda i, g: (0, g)),
        ],
        out_specs=pl.BlockSpec((1, l, gw), lambda i, g: (i, 0, g)),
        out_shape=jax.ShapeDtypeStruct((b, l, inner), BF16),
        scratch_shapes=[pltpu.VMEM((l, gw), F32), pltpu.VMEM((l, n), BF16),
                        pltpu.VMEM((l, n), BF16), pltpu.VMEM((n, gw), F32)],
        compiler_params=_params(("arbitrary", "arbitrary"), 48),
        name="ssd_core",
    )(proj3, proj3, proj3, proj3, conv_w, conv_w, conv_w, conv_b2, conv_b2,
      conv_b2, dt_gt, acs_g, acs_gt, d_skip_exp, gnorm.reshape(1, inner))


def _first_argmax(vals, nrows):
    row = lax.broadcasted_iota(I32, vals.shape, 0)
    top = jnp.max(vals, axis=0, keepdims=True)
    idx = jnp.min(jnp.where(vals == top, row, nrows), axis=0, keepdims=True)
    return top, idx, row


def _pack_bf16_halves(y):
    half = y.shape[1] // 2

    def bits(v):
        return lax.bitcast_convert_type(v.astype(BF16).astype(F32), U32)

    return bits(y[:, half:]) | (bits(y[:, :half]) >> 16)


def _unpack_bf16_halves(w):
    lo = lax.bitcast_convert_type(w << 16, F32)
    hi = lax.bitcast_convert_type(w & jnp.uint32(0xFFFF0000), F32)
    return lo, hi


def _router_body(x_ref, g_ref, rt_ref, rb_ref, eid_ref, wt_ref, rank_ref,
                 cnt_ref, hp_ref, carry_ref):
    step = pl.program_id(0)

    @pl.when(step == 0)
    def _():
        carry_ref[...] = jnp.zeros_like(carry_ref)

    x = x_ref[...]
    tm = x.shape[0]
    h = x * lax.rsqrt(jnp.mean(x * x, axis=-1, keepdims=True) + EPS) * g_ref[...]
    hp_ref[...] = _pack_bf16_halves(h)
    logits = lax.dot_general(rt_ref[...], h, (((1,), (1,)), ((), ())),
                             preferred_element_type=F32,
                             precision=lax.Precision.HIGHEST) + rb_ref[...]
    e_logits = logits[:MOE_EXPERTS]
    g_logits = logits[MOE_EXPERTS:MOE_EXPERTS + MOE_GROUPS]
    g_max, g_sel, _ = _first_argmax(g_logits, MOE_GROUPS)
    g_w = 1.0 / jnp.sum(jnp.exp(g_logits - g_max), axis=0, keepdims=True)
    sel = jnp.zeros((MOE_PER_GROUP, tm), F32)
    for grp in range(MOE_GROUPS):
        sel = jnp.where(g_sel == grp,
                        e_logits[grp * MOE_PER_GROUP:(grp + 1) * MOE_PER_GROUP], sel)
    m1, i1, row8 = _first_argmax(sel, MOE_PER_GROUP)
    rest = jnp.where(row8 == i1, -jnp.inf, sel)
    m2, i2, _ = _first_argmax(rest, MOE_PER_GROUP)
    p2 = jnp.exp(m2 - m1)
    w1 = g_w / (1.0 + p2)
    w2 = g_w * p2 / (1.0 + p2)
    e1 = g_sel * MOE_PER_GROUP + i1
    e2 = g_sel * MOE_PER_GROUP + i2

    r = lax.broadcasted_iota(I32, (tm, tm), 0)
    c = lax.broadcasted_iota(I32, (tm, tm), 1)
    upper = jnp.where(r <= c, 1.0, 0.0).astype(BF16)
    row32 = lax.broadcasted_iota(I32, (MOE_EXPERTS, tm), 0)
    hit1 = row32 == e1
    hit2 = row32 == e2
    cum1 = jnp.dot(jnp.where(hit1, 1.0, 0.0).astype(BF16), upper,
                   preferred_element_type=F32)
    cum2 = jnp.dot(jnp.where(hit2, 1.0, 0.0).astype(BF16), upper,
                   preferred_element_type=F32)
    carry = carry_ref[...]
    tot1 = cum1[:, tm - 1:tm]
    tot2 = cum2[:, tm - 1:tm]
    rank1 = jnp.sum(jnp.where(hit1, carry + cum1 - 1.0, 0.0), axis=0, keepdims=True)
    rank2 = jnp.sum(jnp.where(hit2, carry + tot1 + cum2 - 1.0, 0.0), axis=0,
                    keepdims=True)
    new_carry = carry + tot1 + tot2
    carry_ref[...] = new_carry

    eid_ref[...] = jnp.concatenate([e1, e2], axis=0)
    wt_ref[...] = jnp.concatenate([w1, w2], axis=0)
    rank_ref[...] = jnp.concatenate([rank1, rank2], axis=0).astype(I32)
    cnt_ref[...] = jnp.broadcast_to(new_carry, cnt_ref.shape).astype(I32)


def moe_router(x2d, g, router_g, router_g_b, router_e, router_e_b, tm=512):
    t, d = x2d.shape
    pad = 8 - MOE_GROUPS
    rt = jnp.concatenate([router_e.T, router_g.T, jnp.zeros((pad, d), F32)], axis=0)
    rb = jnp.concatenate([router_e_b, router_g_b, jnp.zeros((pad,), F32)]).reshape(-1, 1)
    nr = rt.shape[0]
    tok = pl.BlockSpec((2, tm), lambda i: (0, i))
    return pl.pallas_call(
        _router_body,
        grid=(t // tm,),
        in_specs=[pl.BlockSpec((tm, d), lambda i: (i, 0)),
                  pl.BlockSpec((1, d), lambda i: (0, 0)),
                  pl.BlockSpec((nr, d), lambda i: (0, 0)),
                  pl.BlockSpec((nr, 1), lambda i: (0, 0))],
        out_specs=[tok, tok, tok,
                   pl.BlockSpec((MOE_EXPERTS, LANES), lambda i: (0, 0)),
                   pl.BlockSpec((tm, d // 2), lambda i: (i, 0))],
        out_shape=[jax.ShapeDtypeStruct((2, t), I32),
                   jax.ShapeDtypeStruct((2, t), F32),
                   jax.ShapeDtypeStruct((2, t), I32),
                   jax.ShapeDtypeStruct((MOE_EXPERTS, LANES), I32),
                   jax.ShapeDtypeStruct((t, d // 2), U32)],
        scratch_shapes=[pltpu.VMEM((MOE_EXPERTS, 1), F32)],
        compiler_params=_params(("arbitrary",), 40),
        name="moe_router",
    )(x2d, g.reshape(1, d), rt, rb)


def _row_tokens_body(dest_ref, tok_ref):
    n_rows = tok_ref.shape[0]
    t = dest_ref.shape[0] // 2

    def clear(r, _):
        tok_ref[r] = 0
        return 0

    lax.fori_loop(0, n_rows, clear, 0, unroll=8)

    def fill(tok, _):
        tok_ref[dest_ref[tok]] = tok
        tok_ref[dest_ref[t + tok]] = tok
        return 0

    lax.fori_loop(0, t, fill, 0, unroll=8)


def moe_row_tokens(dest_flat, n_rows):
    return pl.pallas_call(
        _row_tokens_body,
        grid_spec=pltpu.PrefetchScalarGridSpec(
            num_scalar_prefetch=1,
            grid=(1,),
            in_specs=[],
            out_specs=pl.BlockSpec(memory_space=pltpu.SMEM),
        ),
        out_shape=jax.ShapeDtypeStruct((n_rows,), I32),
        compiler_params=pltpu.CompilerParams(dimension_semantics=("arbitrary",)),
        name="moe_row_tokens",
    )(dest_flat)


def _experts_body(blk_e_ref, n_used_ref, next_e_ref, grp_ref, tok_ref,
                  hp_hbm, wg_hbm, wu_hbm, wd_hbm, o_ref,
                  xbuf_ref, xsem, wg_st, wu_st, wd_st, wsem,
                  wgb_ref, wub_ref, wdb_ref, *, layer):
    i = pl.program_id(0)
    n_used = n_used_ref[0]
    used = i < n_used
    e_cur = blk_e_ref[i]
    prev = blk_e_ref[jnp.maximum(i - 1, 0)]
    fresh = jnp.logical_and(used, jnp.logical_or(i == 0, e_cur != prev))
    rows = xbuf_ref.shape[1]

    def weight_copies(e, slot):
        return (pltpu.make_async_copy(wg_hbm.at[layer, e], wg_st.at[slot],
                                      wsem.at[0, slot]),
                pltpu.make_async_copy(wu_hbm.at[layer, e], wu_st.at[slot],
                                      wsem.at[1, slot]),
                pltpu.make_async_copy(wd_hbm.at[layer, e], wd_st.at[slot],
                                      wsem.at[2, slot]))

    def row_copy(blk, slot, r):
        tok = tok_ref[blk * rows + r]
        return pltpu.make_async_copy(hp_hbm.at[pl.ds(tok, 1)],
                                     xbuf_ref.at[slot, pl.ds(r, 1)],
                                     xsem.at[slot])

    @pl.when(i == 0)
    def _():
        for cp in weight_copies(e_cur, 0):
            cp.start(priority=WEIGHT_DMA_PRIORITY)

        def first(r, _):
            row_copy(0, 0, r).start()
            return 0
        lax.fori_loop(0, rows, first, 0, unroll=8)

        @pl.when(n_used > 1)
        def _():
            def second(r, _):
                row_copy(1, 1, r).start()
                return 0
            lax.fori_loop(0, rows, second, 0, unroll=8)

    wslot = grp_ref[i] % 2
    nxt = next_e_ref[i]

    @pl.when(jnp.logical_and(fresh, nxt >= 0))
    def _():
        for cp in weight_copies(nxt, 1 - wslot):
            cp.start(priority=WEIGHT_DMA_PRIORITY)

    @pl.when(fresh)
    def _():
        for cp in weight_copies(e_cur, wslot):
            cp.wait()
        wgb_ref[...] = wg_st[wslot].astype(BF16)
        wub_ref[...] = wu_st[wslot].astype(BF16)
        wdb_ref[...] = wd_st[wslot].astype(BF16)

    def compute(gather_ahead):
        slot = i % ROW_SLOTS
        pltpu.make_async_copy(hp_hbm.at[pl.ds(0, rows)], xbuf_ref.at[slot],
                              xsem.at[slot]).wait()
        if gather_ahead:
            ahead = i + ROW_SLOTS - 1
            for r in range(rows):
                row_copy(ahead, ahead % ROW_SLOTS, r).start()
        lo, hi = _unpack_bf16_halves(xbuf_ref[slot])
        h = jnp.concatenate([lo.astype(BF16), hi.astype(BF16)], axis=1)
        gate = jnp.dot(h, wgb_ref[...], preferred_element_type=F32)
        up = jnp.dot(h, wub_ref[...], preferred_element_type=F32)
        act = (_silu(gate) * up).astype(BF16)
        y = jnp.dot(act, wdb_ref[...], preferred_element_type=F32)
        o_ref[...] = _pack_bf16_halves(y)

    has_ahead = i + ROW_SLOTS - 1 < n_used

    @pl.when(has_ahead)
    def _():
        compute(True)

    @pl.when(jnp.logical_and(used, jnp.logical_not(has_ahead)))
    def _():
        compute(False)

    @pl.when(jnp.logical_not(used))
    def _():
        o_ref[...] = jnp.zeros_like(o_ref)


def moe_experts(blk_e, n_used, next_e, grp, row_tok, hp, w_gate, w_up, w_down,
                layer):
    n_rows = row_tok.shape[0]
    half = hp.shape[1]
    d = 2 * half
    ff = w_gate.shape[3]
    nblk = n_rows // MOE_BLOCK
    hbm = pl.BlockSpec(memory_space=pl.ANY)
    return pl.pallas_call(
        functools.partial(_experts_body, layer=layer),
        grid_spec=pltpu.PrefetchScalarGridSpec(
            num_scalar_prefetch=5,
            grid=(nblk,),
            in_specs=[hbm, hbm, hbm, hbm],
            out_specs=pl.BlockSpec((MOE_BLOCK, half), lambda i, *_: (i, 0)),
            scratch_shapes=[pltpu.VMEM((ROW_SLOTS, MOE_BLOCK, half), U32),
                            pltpu.SemaphoreType.DMA((ROW_SLOTS,)),
                            pltpu.VMEM((2, d, ff), F32), pltpu.VMEM((2, d, ff), F32),
                            pltpu.VMEM((2, ff, d), F32),
                            pltpu.SemaphoreType.DMA((3, 2)),
                            pltpu.VMEM((d, ff), BF16), pltpu.VMEM((d, ff), BF16),
                            pltpu.VMEM((ff, d), BF16)],
        ),
        out_shape=jax.ShapeDtypeStruct((n_rows, half), U32),
        compiler_params=_params(("arbitrary",), 56),
        name="moe_experts",
    )(blk_e, n_used, next_e, grp, row_tok, hp, w_gate, w_up, w_down)


def _combine_body(dest_ref, x_ref, wt_ref, g_ref, ys_ref, *rest, final_norm):
    if final_norm:
        o_ref, buf_ref, sem = rest
        h_ref = None
    else:
        o_ref, h_ref, buf_ref, sem = rest
    i = pl.program_id(0)
    n = pl.num_programs(0)
    tm = x_ref.shape[0]
    t = tm * n

    def issue(tile, slot):
        def body(r, _):
            for k in range(2):
                d = dest_ref[k * t + tile * tm + r]
                pltpu.make_async_copy(ys_ref.at[pl.ds(d, 1)],
                                      buf_ref.at[slot, k, pl.ds(r, 1)],
                                      sem.at[slot]).start(priority=k)
            return 0
        lax.fori_loop(0, tm, body, 0, unroll=8)

    @pl.when(i == 0)
    def _():
        issue(0, 0)

    @pl.when(i + 1 < n)
    def _():
        issue(i + 1, (i + 1) % 2)

    slot = i % 2
    for k in range(2):
        pltpu.make_async_copy(ys_ref.at[pl.ds(0, tm)], buf_ref.at[slot, k],
                              sem.at[slot]).wait()
    w = wt_ref[...]
    lo1, hi1 = _unpack_bf16_halves(buf_ref[slot, 0])
    lo2, hi2 = _unpack_bf16_halves(buf_ref[slot, 1])
    half = lo1.shape[1]
    x = x_ref[...]
    y_lo = x[:, :half] + w[:, 0:1] * lo1 + w[:, 1:2] * lo2
    y_hi = x[:, half:] + w[:, 0:1] * hi1 + w[:, 1:2] * hi2
    ssq = (jnp.sum(y_lo * y_lo, axis=-1, keepdims=True)
           + jnp.sum(y_hi * y_hi, axis=-1, keepdims=True))
    inv = lax.rsqrt(ssq / (2 * half) + EPS)
    g = g_ref[...]
    n_lo = y_lo * inv * g[:, :half]
    n_hi = y_hi * inv * g[:, half:]
    if final_norm:
        o_ref[:, :half] = n_lo
        o_ref[:, half:] = n_hi
    else:
        o_ref[:, :half] = y_lo
        o_ref[:, half:] = y_hi
        h_ref[:, :half] = n_lo.astype(h_ref.dtype)
        h_ref[:, half:] = n_hi.astype(h_ref.dtype)


def moe_combine(dest_flat, x2d, wt_t, ys, g_norm, final_norm, tm=128):
    t, d = x2d.shape
    row_spec = pl.BlockSpec((tm, d), lambda i, ds: (i, 0))
    if final_norm:
        out_specs = row_spec
        out_shape = jax.ShapeDtypeStruct((t, d), F32)
    else:
        out_specs = [row_spec, row_spec]
        out_shape = [jax.ShapeDtypeStruct((t, d), F32),
                     jax.ShapeDtypeStruct((t, d), BF16)]
    return pl.pallas_call(
        functools.partial(_combine_body, final_norm=final_norm),
        grid_spec=pltpu.PrefetchScalarGridSpec(
            num_scalar_prefetch=1,
            grid=(t // tm,),
            in_specs=[row_spec,
                      pl.BlockSpec((tm, 2), lambda i, ds: (i, 0)),
                      pl.BlockSpec((1, d), lambda i, ds: (0, 0)),
                      pl.BlockSpec(memory_space=pl.ANY)],
            out_specs=out_specs,
            scratch_shapes=[pltpu.VMEM((2, 2, tm, d // 2), U32),
                            pltpu.SemaphoreType.DMA((2,))],
        ),
        out_shape=out_shape,
        compiler_params=_params(("arbitrary",), 32),
        name="moe_combine",
    )(dest_flat, x2d, wt_t, g_norm.reshape(1, d), ys)


def hier_moe(x2d, g_ffn, router_g, router_g_b, router_e, router_e_b,
             w_gate, w_up, w_down, layer, g_norm, final_norm):
    t, d = x2d.shape
    eid, wts, rank, cnt, hp = moe_router(x2d, g_ffn, router_g, router_g_b,
                                         router_e, router_e_b)
    counts = cnt[:, 0]
    padded = (counts + MOE_BLOCK - 1) // MOE_BLOCK * MOE_BLOCK
    pend = jnp.cumsum(padded)
    pstart = pend - padded
    n_rows = 2 * t + MOE_EXPERTS * MOE_BLOCK
    nblk = n_rows // MOE_BLOCK
    expert_ids = jnp.arange(MOE_EXPERTS, dtype=I32)
    start_of = jnp.sum(jnp.where(eid[..., None] == expert_ids, pstart, 0), axis=-1)
    dest = (start_of + rank).reshape(-1)
    blk_start = jnp.arange(nblk, dtype=I32) * MOE_BLOCK
    blk_e = jnp.sum(blk_start[:, None] >= pend[None, :], axis=1).astype(I32)
    blk_e = jnp.minimum(blk_e, MOE_EXPERTS - 1)
    n_used = (pend[-1:] // MOE_BLOCK).astype(I32)
    last_e = blk_e[jnp.maximum(n_used[0] - 1, 0)]
    blk_e = jnp.where(jnp.arange(nblk) < n_used[0], blk_e, last_e)
    changed = jnp.concatenate([jnp.ones((1,), I32),
                               (blk_e[1:] != blk_e[:-1]).astype(I32)])
    grp = jnp.cumsum(changed) - 1
    grp_end = jnp.sum(jnp.where(blk_e[:, None] == expert_ids, pend, 0), axis=-1)
    nxt_blk = grp_end // MOE_BLOCK
    nxt_e = jnp.sum(jnp.where(nxt_blk[:, None] == jnp.arange(nblk), blk_e, 0), axis=-1)
    next_e = jnp.where(nxt_blk < n_used[0], nxt_e, -1).astype(I32)
    row_tok = moe_row_tokens(dest, n_rows)
    ys = moe_experts(blk_e, n_used, next_e, grp.astype(I32), row_tok, hp,
                     w_gate, w_up, w_down, layer)
    return moe_combine(dest, x2d, wts.T, ys, g_norm, final_norm)


def _pad_cols(w, n):
    return jnp.pad(w, ((0, 0), (0, n - w.shape[1])))


def pool_fox_layer(x2d, h, b, l, w_in, b_forget, pool_w, pool_scale, w_out):
    main = POOL_WIDTH + 3 * FOX_WIDTH
    col = jnp.arange(main)
    is_q = jnp.logical_and(col >= POOL_WIDTH, col < POOL_WIDTH + FOX_WIDTH)
    q_scale = jnp.where(is_q, FOX_Q_SCALE, 1.0).astype(F32)
    proj = matmul(h, w_in, main, col_scale=q_scale, name="pf_in_proj")
    f = matmul(h, _pad_cols(w_in[:, main:], LANES), LANES, out_dtype=F32,
               name="pf_forget_proj")
    f_t = f[:, :FOX_HEADS].reshape(b, l, FOX_HEADS).transpose(0, 2, 1)
    c = forget_cumsum(f_t, b_forget)
    proj3 = proj.reshape(b, l, main)
    y_pool = pool_mixer(proj3, pool_w, pool_scale)
    y_att = fox_attention(proj3, c, c.transpose(0, 2, 1))
    y = jnp.concatenate([y_pool, y_att], axis=-1).reshape(b * l, -1)
    return matmul(y, w_out, w_out.shape[1], res=x2d, out_dtype=F32, tm=512,
                  name="pf_out_proj")


def ssd_layer(x2d, h, b, l, w_in, conv_w, conv_b, dt_bias, a_log, d_skip,
              gnorm, w_out):
    heads = dt_bias.shape[0]
    inner = heads * SSD_HEAD_DIM
    main = 2 * inner + 2 * SSD_GROUPS * SSD_STATE
    proj = matmul(h, w_in, main, name="ssd_in_proj")
    raw = matmul(h, _pad_cols(w_in[:, main:], LANES), LANES, out_dtype=F32,
                 name="ssd_dt_proj")
    dt, acs = ssd_dt(raw, jnp.pad(dt_bias, (0, LANES - heads)),
                     jnp.pad(a_log, (0, LANES - heads)))
    hpg = SSD_GROUP_HEADS

    def by_group(v):
        return v[:, :heads].reshape(b, l, SSD_GROUPS, hpg).transpose(0, 2, 1, 3)

    dt_gt = by_group(dt).transpose(0, 1, 3, 2)
    acs_g = by_group(acs)
    acs_gt = acs_g.transpose(0, 1, 3, 2)
    d_skip_exp = jnp.repeat(d_skip, SSD_HEAD_DIM).reshape(1, inner)
    y = ssd_core(proj.reshape(b, l, main), conv_w, conv_b, dt_gt, acs_g, acs_gt,
                 d_skip_exp, gnorm)
    return matmul(y.reshape(b * l, inner), w_out, w_out.shape[1], res=x2d,
                  out_dtype=F32, tm=512, tn=512, name="ssd_out_proj")


def kernel(x, norm_mix, norm_ffn, norm_final, pf_w_in, pf_b_forget, pf_pool_w, pf_pool_scale, pf_w_out, ssd_w_in, ssd_conv_w, ssd_conv_b, ssd_dt_bias, ssd_a_log, ssd_d_skip, ssd_gnorm, ssd_w_out, moe_router_g, moe_router_g_b, moe_router_e, moe_router_e_b, moe_w_gate, moe_w_up, moe_w_down):
    b, l, d = x.shape
    depth = norm_mix.shape[0]
    x2d = x.reshape(b * l, d)
    h = rmsnorm(x2d, norm_mix[0], BF16)
    for i in range(depth):
        j = i // 2
        if i % 2 == 0:
            x2d = pool_fox_layer(x2d, h, b, l, pf_w_in[j], pf_b_forget[j],
                                 pf_pool_w[j], pf_pool_scale[j], pf_w_out[j])
        else:
            x2d = ssd_layer(x2d, h, b, l, ssd_w_in[j], ssd_conv_w[j],
                            ssd_conv_b[j], ssd_dt_bias[j], ssd_a_log[j],
                            ssd_d_skip[j], ssd_gnorm[j], ssd_w_out[j])
        last = i == depth - 1
        g_norm = norm_final if last else norm_mix[i + 1]
        out = hier_moe(x2d, norm_ffn[i], moe_router_g[i], moe_router_g_b[i],
                       moe_router_e[i], moe_router_e_b[i], moe_w_gate,
                       moe_w_up, moe_w_down, i, g_norm, last)
        if last:
            x2d = out
        else:
            x2d, h = out
    return x2d.reshape(b, l, d)
```

```python
import functools

import jax
import jax.numpy as jnp
from jax import lax
from jax.experimental import pallas as pl
from jax.experimental.pallas import tpu as pltpu

F32 = jnp.float32
BF16 = jnp.bfloat16
I32 = jnp.int32
U32 = jnp.uint32
EPS = 1e-6

POOL_WINDOWS = (2, 4, 8, 16)
POOL_GROUP = 256
POOL_WIDTH = 1024
FOX_HEADS = 8
FOX_HEAD_DIM = 128
FOX_WIDTH = 1024
SSD_HEAD_DIM = 64
SSD_STATE = 128
SSD_GROUPS = 8
SSD_GROUP_HEADS = 8
SSD_GROUP_WIDTH = SSD_GROUP_HEADS * SSD_HEAD_DIM
SSD_CONV = 4
SSD_CHUNK = 128
MOE_GROUPS = 4
MOE_PER_GROUP = 8
MOE_EXPERTS = 32
MOE_BLOCK = 256
WEIGHT_DMA_PRIORITY = 1
ROW_SLOTS = 3
LANES = 128
SUBLANES = 8
MIB = 1 << 20


def _params(semantics, vmem_mib):
    return pltpu.CompilerParams(dimension_semantics=semantics,
                                vmem_limit_bytes=vmem_mib * MIB)


def _rmsnorm_body(x_ref, g_ref, o_ref):
    x = x_ref[...]
    inv = lax.rsqrt(jnp.mean(x * x, axis=-1, keepdims=True) + EPS)
    o_ref[...] = (x * inv * g_ref[...]).astype(o_ref.dtype)


def rmsnorm(x2d, g, out_dtype, tm=512):
    t, d = x2d.shape
    return pl.pallas_call(
        _rmsnorm_body,
        grid=(t // tm,),
        in_specs=[pl.BlockSpec((tm, d), lambda i: (i, 0)),
                  pl.BlockSpec((1, d), lambda i: (0, 0))],
        out_specs=pl.BlockSpec((tm, d), lambda i: (i, 0)),
        out_shape=jax.ShapeDtypeStruct((t, d), out_dtype),
        compiler_params=_params(("arbitrary",), 40),
        name="rmsnorm",
    )(x2d, g.reshape(1, d))


def _matmul_body(*refs, has_res, has_scale):
    refs = list(refs)
    a_ref, w_ref = refs[:2]
    o_ref, wb_ref = refs[-2:]
    s_ref = refs[2] if has_scale else None
    r_ref = refs[-3] if has_res else None

    @pl.when(pl.program_id(1) == 0)
    def _():
        w = w_ref[...]
        if has_scale:
            w = w * s_ref[...]
        wb_ref[...] = w.astype(BF16)

    acc = jnp.dot(a_ref[...], wb_ref[...], preferred_element_type=F32)
    if has_res:
        acc = acc + r_ref[...]
    o_ref[...] = acc.astype(o_ref.dtype)


def matmul(a, w, n_out, *, col0=0, res=None, col_scale=None, out_dtype=BF16,
           tm=1024, tn=1024, vmem_mib=48, name="matmul"):
    t, k = a.shape
    tn = min(tn, n_out)
    tm = min(tm, t)
    assert t % tm == 0 and n_out % tn == 0 and col0 % tn == 0
    cb0 = col0 // tn
    in_specs = [pl.BlockSpec((tm, k), lambda j, i: (i, 0)),
                pl.BlockSpec((k, tn), lambda j, i: (0, j + cb0))]
    args = [a, w]
    if col_scale is not None:
        in_specs.append(pl.BlockSpec((1, tn), lambda j, i: (0, j)))
        args.append(col_scale.reshape(1, n_out))
    if res is not None:
        in_specs.append(pl.BlockSpec((tm, tn), lambda j, i: (i, j)))
        args.append(res)
    return pl.pallas_call(
        functools.partial(_matmul_body, has_res=res is not None,
                          has_scale=col_scale is not None),
        grid=(n_out // tn, t // tm),
        in_specs=in_specs,
        out_specs=pl.BlockSpec((tm, tn), lambda j, i: (i, j)),
        out_shape=jax.ShapeDtypeStruct((t, n_out), out_dtype),
        scratch_shapes=[pltpu.VMEM((k, tn), BF16)],
        compiler_params=_params(("arbitrary", "arbitrary"), vmem_mib),
        name=name,
    )(*args)


def _forget_cumsum_body(f_ref, b_ref, c_ref):
    z = f_ref[0] + b_ref[...]
    x = jnp.minimum(z, 0.0) - jnp.log1p(jnp.exp(-jnp.abs(z)))
    n = x.shape[1]
    lane = lax.broadcasted_iota(I32, x.shape, 1)
    shift = 1
    while shift < n:
        x = x + jnp.where(lane >= shift, pltpu.roll(x, shift, axis=1), 0.0)
        shift *= 2
    c_ref[0] = x


def forget_cumsum(f_t, b_forget):
    b, h, l = f_t.shape
    return pl.pallas_call(
        _forget_cumsum_body,
        grid=(b,),
        in_specs=[pl.BlockSpec((1, h, l), lambda i: (i, 0, 0)),
                  pl.BlockSpec((h, 1), lambda i: (0, 0))],
        out_specs=pl.BlockSpec((1, h, l), lambda i: (i, 0, 0)),
        out_shape=jax.ShapeDtypeStruct((b, h, l), F32),
        compiler_params=_params(("arbitrary",), 16),
        name="forget_cumsum",
    )(f_t, b_forget.reshape(h, 1))


def _pool_body(u_ref, w_ref, s_ref, o_ref):
    g = pl.program_id(1)
    u = u_ref[0].astype(F32)
    row = lax.broadcasted_iota(I32, u.shape, 0)
    acc = u
    sums = []
    for shift in (1, 2, 4, 8):
        acc = acc + jnp.where(row >= shift, pltpu.roll(acc, shift, axis=0), 0.0)
        sums.append(acc)
    win_sum = jnp.where(g == 0, sums[0],
                        jnp.where(g == 1, sums[1],
                                  jnp.where(g == 2, sums[2], sums[3])))
    window = jnp.left_shift(jnp.int32(2), g)
    count = jnp.minimum(row + 1, window).astype(F32)
    mixed = win_sum / count - u
    y = jnp.dot(mixed.astype(BF16), w_ref[0].astype(BF16),
                preferred_element_type=F32)
    o_ref[0] = (y * s_ref[...]).astype(o_ref.dtype)


def pool_mixer(proj3, pool_w, pool_scale):
    b, l, _ = proj3.shape
    ng = len(POOL_WINDOWS)
    return pl.pallas_call(
        _pool_body,
        grid=(b, ng),
        in_specs=[pl.BlockSpec((1, l, POOL_GROUP), lambda i, g: (i, 0, g)),
                  pl.BlockSpec((1, POOL_GROUP, POOL_GROUP), lambda i, g: (g, 0, 0)),
                  pl.BlockSpec((1, POOL_GROUP), lambda i, g: (0, g))],
        out_specs=pl.BlockSpec((1, l, POOL_GROUP), lambda i, g: (i, 0, g)),
        out_shape=jax.ShapeDtypeStruct((b, l, POOL_WIDTH), BF16),
        compiler_params=_params(("arbitrary", "arbitrary"), 40),
        name="pool_mixer",
    )(proj3, pool_w, pool_scale.reshape(1, POOL_WIDTH))


LOG2E = 1.4426950408889634
FOX_Q_SCALE = FOX_HEAD_DIM ** -0.5 * LOG2E


FOX_HEADS_PER_STEP = 2


def _weighted_values(p, v):
    d = v.shape[1]
    v_aug = jnp.concatenate([v, jnp.ones_like(v)], axis=1)
    out = jnp.dot(p.astype(BF16), v_aug, preferred_element_type=F32)
    return out[:, :d], out[:, d:d + 1]


def _fox_body(q_ref, k_ref, v_ref, cq_ref, ck_ref, o_ref, m_ref, l_ref, acc_ref,
              *, tile):
    hp = pl.program_id(1)
    qi = pl.program_id(2)
    hd = FOX_HEAD_DIM
    cq8 = cq_ref[0]
    head_lane = lax.broadcasted_iota(I32, cq8.shape, 1)
    heads = []
    for hh in range(FOX_HEADS_PER_STEP):
        head = hp * FOX_HEADS_PER_STEP + hh
        cq = jnp.sum(jnp.where(head_lane == head, cq8, 0.0), axis=1,
                     keepdims=True) * LOG2E
        heads.append((hh, head, slice(hh * hd, (hh + 1) * hd), cq))

    def logits(j, head, cols):
        start = pl.multiple_of(j * tile, tile)
        k = k_ref[0, pl.ds(start, tile), cols]
        ck = ck_ref[0, pl.ds(head, 1), pl.ds(start, tile)] * LOG2E
        s = lax.dot_general(q_ref[0, :, cols], k, (((1,), (1,)), ((), ())),
                            preferred_element_type=F32)
        return s - ck, start

    for hh, head, cols, cq in heads:
        s, start = logits(qi, head, cols)
        r = lax.broadcasted_iota(I32, s.shape, 0)
        c = lax.broadcasted_iota(I32, s.shape, 1)
        s = jnp.where(c <= r, s, -jnp.inf)
        m0 = jnp.max(s, axis=1, keepdims=True) + cq
        p = jnp.exp2(s + (cq - m0))
        m_ref[hh] = m0
        pv, row_sum = _weighted_values(p, v_ref[0, pl.ds(start, tile), cols])
        l_ref[hh] = row_sum
        acc_ref[hh] = pv

    def step(j, _):
        for hh, head, cols, cq in heads:
            s, start = logits(j, head, cols)
            m_old = m_ref[hh]
            m_new = jnp.maximum(m_old, jnp.max(s, axis=1, keepdims=True) + cq)
            alpha = jnp.exp2(m_old - m_new)
            p = jnp.exp2(s + (cq - m_new))
            pv, row_sum = _weighted_values(p, v_ref[0, pl.ds(start, tile), cols])
            l_ref[hh] = alpha * l_ref[hh] + row_sum
            acc_ref[hh] = alpha * acc_ref[hh] + pv
            m_ref[hh] = m_new
        return 0

    lax.fori_loop(0, qi, step, 0)
    for hh, head, cols, cq in heads:
        o_ref[0, :, cols] = (acc_ref[hh] / l_ref[hh]).astype(o_ref.dtype)


def fox_attention(proj3, c_row, c_col, tile=512):
    b, l, _ = proj3.shape
    hps = FOX_HEADS_PER_STEP
    w = hps * FOX_HEAD_DIM
    q0 = POOL_WIDTH // w
    k0 = q0 + FOX_HEADS // hps
    v0 = k0 + FOX_HEADS // hps
    return pl.pallas_call(
        functools.partial(_fox_body, tile=tile),
        grid=(b, FOX_HEADS // hps, l // tile),
        in_specs=[
            pl.BlockSpec((1, tile, w), lambda i, h, q: (i, q, q0 + h)),
            pl.BlockSpec((1, l, w), lambda i, h, q: (i, 0, k0 + h)),
            pl.BlockSpec((1, l, w), lambda i, h, q: (i, 0, v0 + h)),
            pl.BlockSpec((1, tile, FOX_HEADS), lambda i, h, q: (i, q, 0)),
            pl.BlockSpec((1, FOX_HEADS, l), lambda i, h, q: (i, 0, 0)),
        ],
        out_specs=pl.BlockSpec((1, tile, w), lambda i, h, q: (i, q, h)),
        out_shape=jax.ShapeDtypeStruct((b, l, FOX_WIDTH), BF16),
        scratch_shapes=[pltpu.VMEM((hps, tile, 1), F32),
                        pltpu.VMEM((hps, tile, 1), F32),
                        pltpu.VMEM((hps, tile, FOX_HEAD_DIM), F32)],
        compiler_params=_params(("arbitrary", "arbitrary", "arbitrary"), 32),
        name="fox_attention",
    )(proj3, proj3, proj3, c_col, c_row)


def _ssd_dt_body(raw_ref, bias_ref, alog_ref, dt_ref, acs_ref):
    z = raw_ref[...] + bias_ref[...]
    dt = jnp.maximum(z, 0.0) + jnp.log1p(jnp.exp(-jnp.abs(z)))
    a_dt = dt * (-jnp.exp(alog_ref[...]))
    n = z.shape[0]
    r = lax.broadcasted_iota(I32, (n, n), 0)
    c = lax.broadcasted_iota(I32, (n, n), 1)
    tri = (c <= r).astype(F32)
    acs = jnp.dot(tri, a_dt, preferred_element_type=F32,
                  precision=lax.Precision.HIGHEST)
    dt_ref[...] = dt.T
    acs_ref[...] = acs.T


def ssd_dt(raw, bias_pad, alog_pad):
    t, n = raw.shape
    assert n == SSD_CHUNK
    spec = pl.BlockSpec((SSD_CHUNK, n), lambda i: (i, 0))
    spec_t = pl.BlockSpec((n, SSD_CHUNK), lambda i: (0, i))
    vec = pl.BlockSpec((1, n), lambda i: (0, 0))
    return pl.pallas_call(
        _ssd_dt_body,
        grid=(t // SSD_CHUNK,),
        in_specs=[spec, vec, vec],
        out_specs=[spec_t, spec_t],
        out_shape=[jax.ShapeDtypeStruct((n, t), F32)] * 2,
        compiler_params=_params(("arbitrary",), 16),
        name="ssd_dt",
    )(raw, bias_pad.reshape(1, n), alog_pad.reshape(1, n))


def _silu(x):
    return x / (1.0 + jnp.exp(-x))


def _causal_conv_silu(u_ref, w_ref, b_ref):
    u = u_ref[0].astype(F32)
    w = w_ref[...]

    def conv(v, causal_rows):
        out = b_ref[...] + v * w[SSD_CONV - 1:SSD_CONV, :]
        for shift in range(1, SSD_CONV):
            prev = pltpu.roll(v, shift, axis=0)
            if causal_rows is not None:
                prev = jnp.where(causal_rows >= shift, prev, 0.0)
            out = out + prev * w[SSD_CONV - 1 - shift:SSD_CONV - shift, :]
        return out

    head = u[:SUBLANES]
    head_rows = lax.broadcasted_iota(I32, head.shape, 0)
    out = jnp.concatenate([conv(head, head_rows), conv(u, None)[SUBLANES:]], axis=0)
    return _silu(out)


def _pair_lanes(cols, j):
    rows = cols.shape[0]
    lane = lax.broadcasted_iota(I32, (rows, LANES), 1)
    lo = jnp.broadcast_to(cols[:, 2 * j:2 * j + 1], (rows, LANES))
    hi = jnp.broadcast_to(cols[:, 2 * j + 1:2 * j + 2], (rows, LANES))
    return jnp.where(lane < SSD_HEAD_DIM, lo, hi)


def _ssd_body(z_ref, x_ref, b_ref, c_ref, wx_ref, wb_ref, wc_ref,
              bx_ref, bb_ref, bc_ref, dtt_ref, acst_ref,
              dskip_ref, gn_ref, o_ref, xs_ref, bs_ref, cs_ref, st_ref):
    q = SSD_CHUNK
    xs_ref[...] = _causal_conv_silu(x_ref, wx_ref, bx_ref)
    bs_ref[...] = _causal_conv_silu(b_ref, wb_ref, bb_ref).astype(BF16)
    cs_ref[...] = _causal_conv_silu(c_ref, wc_ref, bc_ref).astype(BF16)
    st_ref[...] = jnp.zeros_like(st_ref)

    r = lax.broadcasted_iota(I32, (q, q), 0)
    c = lax.broadcasted_iota(I32, (q, q), 1)
    causal = c <= r
    lane = lax.broadcasted_iota(I32, (q, LANES), 1)
    first_head = lane < SSD_HEAD_DIM

    def chunk(ci, _):
        r0 = pl.multiple_of(ci * q, q)
        xc = xs_ref[pl.ds(r0, q), :]
        bm = bs_ref[pl.ds(r0, q), :]
        cm = cs_ref[pl.ds(r0, q), :]
        zc = z_ref[0, pl.ds(r0, q), :].astype(F32)
        dt_t = dtt_ref[0, :, pl.ds(r0, q)]
        acs_t = acst_ref[0, :, pl.ds(r0, q)]
        acs = acs_t.T
        exp_acs = jnp.exp(acs)
        chunk_decay = jnp.exp(acs[q - 1:q, :])
        w_diag_t = dt_t
        w_end_t = dt_t * jnp.exp(acs_t[:, q - 1:q] - acs_t)
        cb = lax.dot_general(cm, bm, (((1,), (1,)), ((), ())),
                             preferred_element_type=F32)
        bm_t = bm.astype(F32).T
        gated = []
        for j in range(SSD_GROUP_HEADS // 2):
            lanes = slice(j * LANES, (j + 1) * LANES)
            xp = xc[:, lanes]
            xp_b = xp.astype(BF16)
            y_heads = []
            s_heads = []
            for hh in range(2):
                hd = 2 * j + hh
                seg = jnp.broadcast_to(acs[:, hd:hd + 1], (q, q)) - acs_t[hd:hd + 1, :]
                decay = jnp.exp(jnp.where(causal, seg, -jnp.inf))
                mix = cb * decay * w_diag_t[hd:hd + 1, :]
                y_heads.append(jnp.dot(mix.astype(BF16), xp_b,
                                       preferred_element_type=F32))
                b_end = (bm_t * w_end_t[hd:hd + 1, :]).astype(BF16)
                s_heads.append(jnp.dot(b_end, xp_b, preferred_element_type=F32))
            y = jnp.where(first_head, y_heads[0], y_heads[1])
            state = st_ref[:, lanes]
            y = y + jnp.dot(cm, state.astype(BF16),
                            preferred_element_type=F32) * _pair_lanes(exp_acs, j)
            y = y + xp * dskip_ref[:, lanes]
            st_ref[:, lanes] = state * _pair_lanes(chunk_decay, j) + jnp.where(
                first_head, s_heads[0], s_heads[1])
            gated.append(y * _silu(zc[:, lanes]))
        gated = jnp.concatenate(gated, axis=1)
        inv = lax.rsqrt(jnp.mean(gated * gated, axis=-1, keepdims=True) + EPS)
        o_ref[0, pl.ds(r0, q), :] = (gated * inv * gn_ref[...]).astype(o_ref.dtype)
        return 0

    lax.fori_loop(0, x_ref.shape[1] // q, chunk, 0)


def ssd_core(proj3, conv_w, conv_b, dt_gt, acs_gt, d_skip_exp, gnorm):
    b, l, _ = proj3.shape
    gw = SSD_GROUP_WIDTH
    n = SSD_STATE
    inner = SSD_GROUPS * gw
    x_blk0 = inner // gw
    bm_blk0 = 2 * inner // n
    cm_blk0 = bm_blk0 + SSD_GROUPS
    wb_blk0 = inner // n
    wc_blk0 = wb_blk0 + SSD_GROUPS
    hpg = SSD_GROUP_HEADS
    conv_b2 = conv_b.reshape(1, -1)
    return pl.pallas_call(
        _ssd_body,
        grid=(b, SSD_GROUPS),
        in_specs=[
            pl.BlockSpec((1, l, gw), lambda i, g: (i, 0, g)),
            pl.BlockSpec((1, l, gw), lambda i, g: (i, 0, x_blk0 + g)),
            pl.BlockSpec((1, l, n), lambda i, g: (i, 0, bm_blk0 + g)),
            pl.BlockSpec((1, l, n), lambda i, g: (i, 0, cm_blk0 + g)),
            pl.BlockSpec((SSD_CONV, gw), lambda i, g: (0, g)),
            pl.BlockSpec((SSD_CONV, n), lambda i, g: (0, wb_blk0 + g)),
            pl.BlockSpec((SSD_CONV, n), lambda i, g: (0, wc_blk0 + g)),
            pl.BlockSpec((1, gw), lambda i, g: (0, g)),
            pl.BlockSpec((1, n), lambda i, g: (0, wb_blk0 + g)),
            pl.BlockSpec((1, n), lambda i, g: (0, wc_blk0 + g)),
            pl.BlockSpec((1, hpg, l), lambda i, g: (g, 0, i)),
            pl.BlockSpec((1, hpg, l), lambda i, g: (g, 0, i)),
            pl.BlockSpec((1, gw), lambda i, g: (0, g)),
            pl.BlockSpec((1, gw), lambda i, g: (0, g)),
        ],
        out_specs=pl.BlockSpec((1, l, gw), lambda i, g: (i, 0, g)),
        out_shape=jax.ShapeDtypeStruct((b, l, inner), BF16),
        scratch_shapes=[pltpu.VMEM((l, gw), F32), pltpu.VMEM((l, n), BF16),
                        pltpu.VMEM((l, n), BF16), pltpu.VMEM((n, gw), F32)],
        compiler_params=_params(("arbitrary", "arbitrary"), 48),
        name="ssd_core",
    )(proj3, proj3, proj3, proj3, conv_w, conv_w, conv_w, conv_b2, conv_b2,
      conv_b2, dt_gt, acs_gt, d_skip_exp, gnorm.reshape(1, inner))


def _first_argmax(vals, nrows):
    row = lax.broadcasted_iota(I32, vals.shape, 0)
    top = jnp.max(vals, axis=0, keepdims=True)
    idx = jnp.min(jnp.where(vals == top, row, nrows), axis=0, keepdims=True)
    return top, idx, row


def _pack_bf16_halves(y):
    half = y.shape[1] // 2

    def bits(v):
        return lax.bitcast_convert_type(v.astype(BF16).astype(F32), U32)

    return bits(y[:, half:]) | (bits(y[:, :half]) >> 16)


def _unpack_bf16_halves(w):
    lo = lax.bitcast_convert_type(w << 16, F32)
    hi = lax.bitcast_convert_type(w & jnp.uint32(0xFFFF0000), F32)
    return lo, hi


def _router_body(x_ref, g_ref, rt_ref, rb_ref, eid_ref, wt_ref, rank_ref,
                 cnt_ref, hp_ref, carry_ref):
    step = pl.program_id(0)

    @pl.when(step == 0)
    def _():
        carry_ref[...] = jnp.zeros_like(carry_ref)

    x = x_ref[...]
    tm = x.shape[0]
    h = x * lax.rsqrt(jnp.mean(x * x, axis=-1, keepdims=True) + EPS) * g_ref[...]
    hp_ref[...] = _pack_bf16_halves(h)
    logits = lax.dot_general(rt_ref[...], h, (((1,), (1,)), ((), ())),
                             preferred_element_type=F32,
                             precision=lax.Precision.HIGHEST) + rb_ref[...]
    e_logits = logits[:MOE_EXPERTS]
    g_logits = logits[MOE_EXPERTS:MOE_EXPERTS + MOE_GROUPS]
    g_max, g_sel, _ = _first_argmax(g_logits, MOE_GROUPS)
    g_w = 1.0 / jnp.sum(jnp.exp(g_logits - g_max), axis=0, keepdims=True)
    sel = jnp.zeros((MOE_PER_GROUP, tm), F32)
    for grp in range(MOE_GROUPS):
        sel = jnp.where(g_sel == grp,
                        e_logits[grp * MOE_PER_GROUP:(grp + 1) * MOE_PER_GROUP], sel)
    m1, i1, row8 = _first_argmax(sel, MOE_PER_GROUP)
    rest = jnp.where(row8 == i1, -jnp.inf, sel)
    m2, i2, _ = _first_argmax(rest, MOE_PER_GROUP)
    p2 = jnp.exp(m2 - m1)
    w1 = g_w / (1.0 + p2)
    w2 = g_w * p2 / (1.0 + p2)
    e1 = g_sel * MOE_PER_GROUP + i1
    e2 = g_sel * MOE_PER_GROUP + i2

    r = lax.broadcasted_iota(I32, (tm, tm), 0)
    c = lax.broadcasted_iota(I32, (tm, tm), 1)
    upper = jnp.where(r <= c, 1.0, 0.0).astype(BF16)
    row32 = lax.broadcasted_iota(I32, (MOE_EXPERTS, tm), 0)
    hit1 = row32 == e1
    hit2 = row32 == e2
    cum1 = jnp.dot(jnp.where(hit1, 1.0, 0.0).astype(BF16), upper,
                   preferred_element_type=F32)
    cum2 = jnp.dot(jnp.where(hit2, 1.0, 0.0).astype(BF16), upper,
                   preferred_element_type=F32)
    carry = carry_ref[...]
    tot1 = cum1[:, tm - 1:tm]
    tot2 = cum2[:, tm - 1:tm]
    rank1 = jnp.sum(jnp.where(hit1, carry + cum1 - 1.0, 0.0), axis=0, keepdims=True)
    rank2 = jnp.sum(jnp.where(hit2, carry + tot1 + cum2 - 1.0, 0.0), axis=0,
                    keepdims=True)
    new_carry = carry + tot1 + tot2
    carry_ref[...] = new_carry

    eid_ref[...] = jnp.concatenate([e1, e2], axis=0)
    wt_ref[...] = jnp.concatenate([w1, w2], axis=0)
    rank_ref[...] = jnp.concatenate([rank1, rank2], axis=0).astype(I32)
    cnt_ref[...] = jnp.broadcast_to(new_carry, cnt_ref.shape).astype(I32)


def moe_router(x2d, g, router_g, router_g_b, router_e, router_e_b, tm=512):
    t, d = x2d.shape
    pad = 8 - MOE_GROUPS
    rt = jnp.concatenate([router_e.T, router_g.T, jnp.zeros((pad, d), F32)], axis=0)
    rb = jnp.concatenate([router_e_b, router_g_b, jnp.zeros((pad,), F32)]).reshape(-1, 1)
    nr = rt.shape[0]
    tok = pl.BlockSpec((2, tm), lambda i: (0, i))
    return pl.pallas_call(
        _router_body,
        grid=(t // tm,),
        in_specs=[pl.BlockSpec((tm, d), lambda i: (i, 0)),
                  pl.BlockSpec((1, d), lambda i: (0, 0)),
                  pl.BlockSpec((nr, d), lambda i: (0, 0)),
                  pl.BlockSpec((nr, 1), lambda i: (0, 0))],
        out_specs=[tok, tok, tok,
                   pl.BlockSpec((MOE_EXPERTS, LANES), lambda i: (0, 0)),
                   pl.BlockSpec((tm, d // 2), lambda i: (i, 0))],
        out_shape=[jax.ShapeDtypeStruct((2, t), I32),
                   jax.ShapeDtypeStruct((2, t), F32),
                   jax.ShapeDtypeStruct((2, t), I32),
                   jax.ShapeDtypeStruct((MOE_EXPERTS, LANES), I32),
                   jax.ShapeDtypeStruct((t, d // 2), U32)],
        scratch_shapes=[pltpu.VMEM((MOE_EXPERTS, 1), F32)],
        compiler_params=_params(("arbitrary",), 40),
        name="moe_router",
    )(x2d, g.reshape(1, d), rt, rb)


def _row_tokens_body(dest_ref, tok_ref):
    n_rows = tok_ref.shape[0]
    t = dest_ref.shape[0] // 2

    def clear(r, _):
        tok_ref[r] = 0
        return 0

    lax.fori_loop(0, n_rows, clear, 0, unroll=8)

    def fill(tok, _):
        tok_ref[dest_ref[tok]] = tok
        tok_ref[dest_ref[t + tok]] = tok
        return 0

    lax.fori_loop(0, t, fill, 0, unroll=8)


def moe_row_tokens(dest_flat, n_rows):
    return pl.pallas_call(
        _row_tokens_body,
        grid_spec=pltpu.PrefetchScalarGridSpec(
            num_scalar_prefetch=1,
            grid=(1,),
            in_specs=[],
            out_specs=pl.BlockSpec(memory_space=pltpu.SMEM),
        ),
        out_shape=jax.ShapeDtypeStruct((n_rows,), I32),
        compiler_params=pltpu.CompilerParams(dimension_semantics=("arbitrary",)),
        name="moe_row_tokens",
    )(dest_flat)


def _experts_body(blk_e_ref, n_used_ref, next_e_ref, grp_ref, tok_ref,
                  hp_hbm, wg_hbm, wu_hbm, wd_hbm, o_ref,
                  xbuf_ref, xsem, wg_st, wu_st, wd_st, wsem,
                  wgb_ref, wub_ref, wdb_ref, *, layer):
    i = pl.program_id(0)
    n_used = n_used_ref[0]
    used = i < n_used
    e_cur = blk_e_ref[i]
    prev = blk_e_ref[jnp.maximum(i - 1, 0)]
    fresh = jnp.logical_and(used, jnp.logical_or(i == 0, e_cur != prev))
    rows = xbuf_ref.shape[1]

    def weight_copies(e, slot):
        return (pltpu.make_async_copy(wg_hbm.at[layer, e], wg_st.at[slot],
                                      wsem.at[0, slot]),
                pltpu.make_async_copy(wu_hbm.at[layer, e], wu_st.at[slot],
                                      wsem.at[1, slot]),
                pltpu.make_async_copy(wd_hbm.at[layer, e], wd_st.at[slot],
                                      wsem.at[2, slot]))

    def row_copy(blk, slot, r):
        tok = tok_ref[blk * rows + r]
        return pltpu.make_async_copy(hp_hbm.at[pl.ds(tok, 1)],
                                     xbuf_ref.at[slot, pl.ds(r, 1)],
                                     xsem.at[slot])

    @pl.when(i == 0)
    def _():
        for cp in weight_copies(e_cur, 0):
            cp.start(priority=WEIGHT_DMA_PRIORITY)

        def first(r, _):
            row_copy(0, 0, r).start()
            return 0
        lax.fori_loop(0, rows, first, 0, unroll=8)

        @pl.when(n_used > 1)
        def _():
            def second(r, _):
                row_copy(1, 1, r).start()
                return 0
            lax.fori_loop(0, rows, second, 0, unroll=8)

    wslot = grp_ref[i] % 2
    nxt = next_e_ref[i]

    @pl.when(jnp.logical_and(fresh, nxt >= 0))
    def _():
        for cp in weight_copies(nxt, 1 - wslot):
            cp.start(priority=WEIGHT_DMA_PRIORITY)

    @pl.when(fresh)
    def _():
        for cp in weight_copies(e_cur, wslot):
            cp.wait()
        wgb_ref[...] = wg_st[wslot].astype(BF16)
        wub_ref[...] = wu_st[wslot].astype(BF16)
        wdb_ref[...] = wd_st[wslot].astype(BF16)

    def compute(gather_ahead):
        slot = i % ROW_SLOTS
        pltpu.make_async_copy(hp_hbm.at[pl.ds(0, rows)], xbuf_ref.at[slot],
                              xsem.at[slot]).wait()
        if gather_ahead:
            ahead = i + ROW_SLOTS - 1
            for r in range(rows):
                row_copy(ahead, ahead % ROW_SLOTS, r).start()
        lo, hi = _unpack_bf16_halves(xbuf_ref[slot])
        h = jnp.concatenate([lo.astype(BF16), hi.astype(BF16)], axis=1)
        gate = jnp.dot(h, wgb_ref[...], preferred_element_type=F32)
        up = jnp.dot(h, wub_ref[...], preferred_element_type=F32)
        act = (_silu(gate) * up).astype(BF16)
        y = jnp.dot(act, wdb_ref[...], preferred_element_type=F32)
        o_ref[...] = _pack_bf16_halves(y)

    has_ahead = i + ROW_SLOTS - 1 < n_used

    @pl.when(has_ahead)
    def _():
        compute(True)

    @pl.when(jnp.logical_and(used, jnp.logical_not(has_ahead)))
    def _():
        compute(False)

    @pl.when(jnp.logical_not(used))
    def _():
        o_ref[...] = jnp.zeros_like(o_ref)


def moe_experts(blk_e, n_used, next_e, grp, row_tok, hp, w_gate, w_up, w_down,
                layer):
    n_rows = row_tok.shape[0]
    half = hp.shape[1]
    d = 2 * half
    ff = w_gate.shape[3]
    nblk = n_rows // MOE_BLOCK
    hbm = pl.BlockSpec(memory_space=pl.ANY)
    return pl.pallas_call(
        functools.partial(_experts_body, layer=layer),
        grid_spec=pltpu.PrefetchScalarGridSpec(
            num_scalar_prefetch=5,
            grid=(nblk,),
            in_specs=[hbm, hbm, hbm, hbm],
            out_specs=pl.BlockSpec((MOE_BLOCK, half), lambda i, *_: (i, 0)),
            scratch_shapes=[pltpu.VMEM((ROW_SLOTS, MOE_BLOCK, half), U32),
                            pltpu.SemaphoreType.DMA((ROW_SLOTS,)),
                            pltpu.VMEM((2, d, ff), F32), pltpu.VMEM((2, d, ff), F32),
                            pltpu.VMEM((2, ff, d), F32),
                            pltpu.SemaphoreType.DMA((3, 2)),
                            pltpu.VMEM((d, ff), BF16), pltpu.VMEM((d, ff), BF16),
                            pltpu.VMEM((ff, d), BF16)],
        ),
        out_shape=jax.ShapeDtypeStruct((n_rows, half), U32),
        compiler_params=_params(("arbitrary",), 56),
        name="moe_experts",
    )(blk_e, n_used, next_e, grp, row_tok, hp, w_gate, w_up, w_down)


def _combine_body(dest_ref, x_ref, wt_ref, g_ref, ys_ref, *rest, final_norm):
    if final_norm:
        o_ref, buf_ref, sem = rest
        h_ref = None
    else:
        o_ref, h_ref, buf_ref, sem = rest
    i = pl.program_id(0)
    n = pl.num_programs(0)
    tm = x_ref.shape[0]
    t = tm * n

    def issue(tile, slot):
        def body(r, _):
            for k in range(2):
                d = dest_ref[k * t + tile * tm + r]
                pltpu.make_async_copy(ys_ref.at[pl.ds(d, 1)],
                                      buf_ref.at[slot, k, pl.ds(r, 1)],
                                      sem.at[slot]).start(priority=k)
            return 0
        lax.fori_loop(0, tm, body, 0, unroll=8)

    @pl.when(i == 0)
    def _():
        issue(0, 0)

    @pl.when(i + 1 < n)
    def _():
        issue(i + 1, (i + 1) % 2)

    slot = i % 2
    for k in range(2):
        pltpu.make_async_copy(ys_ref.at[pl.ds(0, tm)], buf_ref.at[slot, k],
                              sem.at[slot]).wait()
    w = wt_ref[...]
    lo1, hi1 = _unpack_bf16_halves(buf_ref[slot, 0])
    lo2, hi2 = _unpack_bf16_halves(buf_ref[slot, 1])
    half = lo1.shape[1]
    x = x_ref[...]
    y_lo = x[:, :half] + w[:, 0:1] * lo1 + w[:, 1:2] * lo2
    y_hi = x[:, half:] + w[:, 0:1] * hi1 + w[:, 1:2] * hi2
    ssq = (jnp.sum(y_lo * y_lo, axis=-1, keepdims=True)
           + jnp.sum(y_hi * y_hi, axis=-1, keepdims=True))
    inv = lax.rsqrt(ssq / (2 * half) + EPS)
    g = g_ref[...]
    n_lo = y_lo * inv * g[:, :half]
    n_hi = y_hi * inv * g[:, half:]
    if final_norm:
        o_ref[:, :half] = n_lo
        o_ref[:, half:] = n_hi
    else:
        o_ref[:, :half] = y_lo
        o_ref[:, half:] = y_hi
        h_ref[:, :half] = n_lo.astype(h_ref.dtype)
        h_ref[:, half:] = n_hi.astype(h_ref.dtype)


def moe_combine(dest_flat, x2d, wt_t, ys, g_norm, final_norm, tm=128):
    t, d = x2d.shape
    row_spec = pl.BlockSpec((tm, d), lambda i, ds: (i, 0))
    if final_norm:
        out_specs = row_spec
        out_shape = jax.ShapeDtypeStruct((t, d), F32)
    else:
        out_specs = [row_spec, row_spec]
        out_shape = [jax.ShapeDtypeStruct((t, d), F32),
                     jax.ShapeDtypeStruct((t, d), BF16)]
    return pl.pallas_call(
        functools.partial(_combine_body, final_norm=final_norm),
        grid_spec=pltpu.PrefetchScalarGridSpec(
            num_scalar_prefetch=1,
            grid=(t // tm,),
            in_specs=[row_spec,
                      pl.BlockSpec((tm, 2), lambda i, ds: (i, 0)),
                      pl.BlockSpec((1, d), lambda i, ds: (0, 0)),
                      pl.BlockSpec(memory_space=pl.ANY)],
            out_specs=out_specs,
            scratch_shapes=[pltpu.VMEM((2, 2, tm, d // 2), U32),
                            pltpu.SemaphoreType.DMA((2,))],
        ),
        out_shape=out_shape,
        compiler_params=_params(("arbitrary",), 32),
        name="moe_combine",
    )(dest_flat, x2d, wt_t, g_norm.reshape(1, d), ys)


def hier_moe(x2d, g_ffn, router_g, router_g_b, router_e, router_e_b,
             w_gate, w_up, w_down, layer, g_norm, final_norm):
    t, d = x2d.shape
    eid, wts, rank, cnt, hp = moe_router(x2d, g_ffn, router_g, router_g_b,
                                         router_e, router_e_b)
    counts = cnt[:, 0]
    padded = (counts + MOE_BLOCK - 1) // MOE_BLOCK * MOE_BLOCK
    pend = jnp.cumsum(padded)
    pstart = pend - padded
    n_rows = 2 * t + MOE_EXPERTS * MOE_BLOCK
    nblk = n_rows // MOE_BLOCK
    expert_ids = jnp.arange(MOE_EXPERTS, dtype=I32)
    start_of = jnp.sum(jnp.where(eid[..., None] == expert_ids, pstart, 0), axis=-1)
    dest = (start_of + rank).reshape(-1)
    blk_start = jnp.arange(nblk, dtype=I32) * MOE_BLOCK
    blk_e = jnp.sum(blk_start[:, None] >= pend[None, :], axis=1).astype(I32)
    blk_e = jnp.minimum(blk_e, MOE_EXPERTS - 1)
    n_used = (pend[-1:] // MOE_BLOCK).astype(I32)
    last_e = blk_e[jnp.maximum(n_used[0] - 1, 0)]
    blk_e = jnp.where(jnp.arange(nblk) < n_used[0], blk_e, last_e)
    changed = jnp.concatenate([jnp.ones((1,), I32),
                               (blk_e[1:] != blk_e[:-1]).astype(I32)])
    grp = jnp.cumsum(changed) - 1
    grp_end = jnp.sum(jnp.where(blk_e[:, None] == expert_ids, pend, 0), axis=-1)
    nxt_blk = grp_end // MOE_BLOCK
    nxt_e = jnp.sum(jnp.where(nxt_blk[:, None] == jnp.arange(nblk), blk_e, 0), axis=-1)
    next_e = jnp.where(nxt_blk < n_used[0], nxt_e, -1).astype(I32)
    row_tok = moe_row_tokens(dest, n_rows)
    ys = moe_experts(blk_e, n_used, next_e, grp.astype(I32), row_tok, hp,
                     w_gate, w_up, w_down, layer)
    return moe_combine(dest, x2d, wts.T, ys, g_norm, final_norm)


def _pad_cols(w, n):
    return jnp.pad(w, ((0, 0), (0, n - w.shape[1])))


def pool_fox_layer(x2d, h, b, l, w_in, b_forget, pool_w, pool_scale, w_out):
    main = POOL_WIDTH + 3 * FOX_WIDTH
    col = jnp.arange(main)
    is_q = jnp.logical_and(col >= POOL_WIDTH, col < POOL_WIDTH + FOX_WIDTH)
    q_scale = jnp.where(is_q, FOX_Q_SCALE, 1.0).astype(F32)
    proj = matmul(h, w_in, main, col_scale=q_scale, name="pf_in_proj")
    f = matmul(h, _pad_cols(w_in[:, main:], LANES), LANES, out_dtype=F32,
               name="pf_forget_proj")
    f_t = f[:, :FOX_HEADS].reshape(b, l, FOX_HEADS).transpose(0, 2, 1)
    c = forget_cumsum(f_t, b_forget)
    proj3 = proj.reshape(b, l, main)
    y_pool = pool_mixer(proj3, pool_w, pool_scale)
    y_att = fox_attention(proj3, c, c.transpose(0, 2, 1))
    y = jnp.concatenate([y_pool, y_att], axis=-1).reshape(b * l, -1)
    return matmul(y, w_out, w_out.shape[1], res=x2d, out_dtype=F32, vmem_mib=56,
                  name="pf_out_proj")


def ssd_layer(x2d, h, b, l, w_in, conv_w, conv_b, dt_bias, a_log, d_skip,
              gnorm, w_out):
    heads = dt_bias.shape[0]
    inner = heads * SSD_HEAD_DIM
    main = 2 * inner + 2 * SSD_GROUPS * SSD_STATE
    proj = matmul(h, w_in, main, name="ssd_in_proj")
    raw = matmul(h, _pad_cols(w_in[:, main:], LANES), LANES, out_dtype=F32,
                 name="ssd_dt_proj")
    dt_t, acs_t = ssd_dt(raw, jnp.pad(dt_bias, (0, LANES - heads)),
                         jnp.pad(a_log, (0, LANES - heads)))
    hpg = SSD_GROUP_HEADS
    dt_gt = dt_t[:heads].reshape(SSD_GROUPS, hpg, b * l)
    acs_gt = acs_t[:heads].reshape(SSD_GROUPS, hpg, b * l)
    d_skip_exp = jnp.repeat(d_skip, SSD_HEAD_DIM).reshape(1, inner)
    y = ssd_core(proj.reshape(b, l, main), conv_w, conv_b, dt_gt, acs_gt,
                 d_skip_exp, gnorm)
    return matmul(y.reshape(b * l, inner), w_out, w_out.shape[1], res=x2d,
                  out_dtype=F32, tn=512, vmem_mib=56, name="ssd_out_proj")


def kernel(x, norm_mix, norm_ffn, norm_final, pf_w_in, pf_b_forget, pf_pool_w, pf_pool_scale, pf_w_out, ssd_w_in, ssd_conv_w, ssd_conv_b, ssd_dt_bias, ssd_a_log, ssd_d_skip, ssd_gnorm, ssd_w_out, moe_router_g, moe_router_g_b, moe_router_e, moe_router_e_b, moe_w_gate, moe_w_up, moe_w_down):
    b, l, d = x.shape
    depth = norm_mix.shape[0]
    x2d = x.reshape(b * l, d)
    h = rmsnorm(x2d, norm_mix[0], BF16)
    for i in range(depth):
        j = i // 2
        if i % 2 == 0:
            x2d = pool_fox_layer(x2d, h, b, l, pf_w_in[j], pf_b_forget[j],
                                 pf_pool_w[j], pf_pool_scale[j], pf_w_out[j])
        else:
            x2d = ssd_layer(x2d, h, b, l, ssd_w_in[j], ssd_conv_w[j],
                            ssd_conv_b[j], ssd_dt_bias[j], ssd_a_log[j],
                            ssd_d_skip[j], ssd_gnorm[j], ssd_w_out[j])
        last = i == depth - 1
        g_norm = norm_final if last else norm_mix[i + 1]
        out = hier_moe(x2d, norm_ffn[i], moe_router_g[i], moe_router_g_b[i],
                       moe_router_e[i], moe_router_e_b[i], moe_w_gate,
                       moe_w_up, moe_w_down, i, g_norm, last)
        if last:
            x2d = out
        else:
            x2d, h = out
    return x2d.reshape(b, l, d)
```

```python
import functools

import jax
import jax.numpy as jnp
from jax import lax
from jax.experimental import pallas as pl
from jax.experimental.pallas import tpu as pltpu

F32 = jnp.float32
BF16 = jnp.bfloat16
I32 = jnp.int32
U32 = jnp.uint32
EPS = 1e-6

POOL_WINDOWS = (2, 4, 8, 16)
POOL_GROUP = 256
POOL_WIDTH = 1024
FOX_HEADS = 8
FOX_HEAD_DIM = 128
FOX_WIDTH = 1024
SSD_HEAD_DIM = 64
SSD_STATE = 128
SSD_GROUPS = 8
SSD_GROUP_HEADS = 8
SSD_GROUP_WIDTH = SSD_GROUP_HEADS * SSD_HEAD_DIM
SSD_CONV = 4
SSD_CHUNK = 128
MOE_GROUPS = 4
MOE_PER_GROUP = 8
MOE_EXPERTS = 32
MOE_BLOCK = 256
WEIGHT_DMA_PRIORITY = 1
ROW_SLOTS = 3
LANES = 128
SUBLANES = 8
MIB = 1 << 20


def _params(semantics, vmem_mib):
    return pltpu.CompilerParams(dimension_semantics=semantics,
                                vmem_limit_bytes=vmem_mib * MIB)


def _rmsnorm_body(x_ref, g_ref, o_ref):
    x = x_ref[...]
    inv = lax.rsqrt(jnp.mean(x * x, axis=-1, keepdims=True) + EPS)
    o_ref[...] = (x * inv * g_ref[...]).astype(o_ref.dtype)


def rmsnorm(x2d, g, out_dtype, tm=512):
    t, d = x2d.shape
    return pl.pallas_call(
        _rmsnorm_body,
        grid=(t // tm,),
        in_specs=[pl.BlockSpec((tm, d), lambda i: (i, 0)),
                  pl.BlockSpec((1, d), lambda i: (0, 0))],
        out_specs=pl.BlockSpec((tm, d), lambda i: (i, 0)),
        out_shape=jax.ShapeDtypeStruct((t, d), out_dtype),
        compiler_params=_params(("arbitrary",), 40),
        name="rmsnorm",
    )(x2d, g.reshape(1, d))


def _matmul_body(*refs, n_act, has_res, w_transposed, scaled_cols):
    refs = list(refs)
    a_refs = refs[:n_act]
    w_ref = refs[n_act]
    o_ref, wb_ref = refs[-2:]
    r_ref = refs[-3] if has_res else None
    out_axis = 0 if w_transposed else 1

    @pl.when(pl.program_id(1) == 0)
    def _():
        w = w_ref[0]
        if scaled_cols is not None:
            lo, hi, value = scaled_cols
            col = (lax.broadcasted_iota(I32, w.shape, out_axis)
                   + pl.program_id(0) * w.shape[out_axis])
            w = jnp.where(jnp.logical_and(col >= lo, col < hi), w * value, w)
        wb_ref[...] = w.astype(BF16)

    acc = None
    k0 = 0
    for a_ref in a_refs:
        kw = a_ref.shape[1]
        if w_transposed:
            part = lax.dot_general(a_ref[...], wb_ref[:, k0:k0 + kw],
                                   (((1,), (1,)), ((), ())),
                                   preferred_element_type=F32)
        else:
            part = jnp.dot(a_ref[...], wb_ref[k0:k0 + kw, :],
                           preferred_element_type=F32)
        acc = part if acc is None else acc + part
        k0 += kw
    if has_res:
        acc = acc + r_ref[...]
    o_ref[...] = acc.astype(o_ref.dtype)


def matmul(acts, w, layer, n_out, *, w_transposed=False, res=None,
           scaled_cols=None, out_dtype=BF16, tm=1024, tn=1024, vmem_mib=48,
           name="matmul"):
    t = acts[0].shape[0]
    k = sum(a.shape[1] for a in acts)
    assert w.shape[2 if w_transposed else 1] == k
    tn = min(tn, n_out)
    tm = min(tm, t)
    assert t % tm == 0 and n_out % tn == 0
    in_specs = [pl.BlockSpec((tm, a.shape[1]), lambda j, i: (i, 0)) for a in acts]
    if w_transposed:
        in_specs.append(pl.BlockSpec((1, tn, k), lambda j, i: (layer, j, 0)))
        wb_shape = (tn, k)
    else:
        in_specs.append(pl.BlockSpec((1, k, tn), lambda j, i: (layer, 0, j)))
        wb_shape = (k, tn)
    args = list(acts) + [w]
    if res is not None:
        in_specs.append(pl.BlockSpec((tm, tn), lambda j, i: (i, j)))
        args.append(res)
    return pl.pallas_call(
        functools.partial(_matmul_body, n_act=len(acts), has_res=res is not None,
                          w_transposed=w_transposed, scaled_cols=scaled_cols),
        grid=(n_out // tn, t // tm),
        in_specs=in_specs,
        out_specs=pl.BlockSpec((tm, tn), lambda j, i: (i, j)),
        out_shape=jax.ShapeDtypeStruct((t, n_out), out_dtype),
        scratch_shapes=[pltpu.VMEM(wb_shape, BF16)],
        compiler_params=_params(("arbitrary", "arbitrary"), vmem_mib),
        name=name,
    )(*args)


def _forget_cumsum_body(f_ref, b_ref, c_ref):
    z = f_ref[0] + b_ref[...]
    x = jnp.minimum(z, 0.0) - jnp.log1p(jnp.exp(-jnp.abs(z)))
    n = x.shape[1]
    lane = lax.broadcasted_iota(I32, x.shape, 1)
    shift = 1
    while shift < n:
        x = x + jnp.where(lane >= shift, pltpu.roll(x, shift, axis=1), 0.0)
        shift *= 2
    c_ref[0] = x


def forget_cumsum(f_t, b_forget):
    b, h, l = f_t.shape
    return pl.pallas_call(
        _forget_cumsum_body,
        grid=(b,),
        in_specs=[pl.BlockSpec((1, h, l), lambda i: (i, 0, 0)),
                  pl.BlockSpec((h, 1), lambda i: (0, 0))],
        out_specs=pl.BlockSpec((1, h, l), lambda i: (i, 0, 0)),
        out_shape=jax.ShapeDtypeStruct((b, h, l), F32),
        compiler_params=_params(("arbitrary",), 16),
        name="forget_cumsum",
    )(f_t, b_forget.reshape(h, 1))


def _pool_body(u_ref, w_ref, s_ref, o_ref):
    g = pl.program_id(1)
    u = u_ref[0].astype(F32)
    row = lax.broadcasted_iota(I32, u.shape, 0)
    acc = u
    sums = []
    for shift in (1, 2, 4, 8):
        acc = acc + jnp.where(row >= shift, pltpu.roll(acc, shift, axis=0), 0.0)
        sums.append(acc)
    win_sum = jnp.where(g == 0, sums[0],
                        jnp.where(g == 1, sums[1],
                                  jnp.where(g == 2, sums[2], sums[3])))
    window = jnp.left_shift(jnp.int32(2), g)
    count = jnp.minimum(row + 1, window).astype(F32)
    mixed = win_sum / count - u
    y = jnp.dot(mixed.astype(BF16), w_ref[0].astype(BF16),
                preferred_element_type=F32)
    o_ref[0] = (y * s_ref[...]).astype(o_ref.dtype)


def pool_mixer(proj3, pool_w, pool_scale):
    b, l, _ = proj3.shape
    ng = len(POOL_WINDOWS)
    return pl.pallas_call(
        _pool_body,
        grid=(b, ng),
        in_specs=[pl.BlockSpec((1, l, POOL_GROUP), lambda i, g: (i, 0, g)),
                  pl.BlockSpec((1, POOL_GROUP, POOL_GROUP), lambda i, g: (g, 0, 0)),
                  pl.BlockSpec((1, POOL_GROUP), lambda i, g: (0, g))],
        out_specs=pl.BlockSpec((1, l, POOL_GROUP), lambda i, g: (i, 0, g)),
        out_shape=jax.ShapeDtypeStruct((b, l, POOL_WIDTH), BF16),
        compiler_params=_params(("arbitrary", "arbitrary"), 40),
        name="pool_mixer",
    )(proj3, pool_w, pool_scale.reshape(1, POOL_WIDTH))


LOG2E = 1.4426950408889634
FOX_Q_SCALE = FOX_HEAD_DIM ** -0.5 * LOG2E


FOX_HEADS_PER_STEP = 2


def _weighted_values(p, v):
    d = v.shape[1]
    v_aug = jnp.concatenate([v, jnp.ones_like(v)], axis=1)
    out = jnp.dot(p.astype(BF16), v_aug, preferred_element_type=F32)
    return out[:, :d], out[:, d:d + 1]


def _fox_body(q_ref, k_ref, v_ref, cq_ref, ck_ref, o_ref, m_ref, l_ref, acc_ref,
              *, tile):
    hp = pl.program_id(1)
    qi = pl.program_id(2)
    hd = FOX_HEAD_DIM
    cq8 = cq_ref[0]
    head_lane = lax.broadcasted_iota(I32, cq8.shape, 1)
    heads = []
    for hh in range(FOX_HEADS_PER_STEP):
        head = hp * FOX_HEADS_PER_STEP + hh
        cq = jnp.sum(jnp.where(head_lane == head, cq8, 0.0), axis=1,
                     keepdims=True) * LOG2E
        heads.append((hh, head, slice(hh * hd, (hh + 1) * hd), cq))

    def logits(j, head, cols):
        start = pl.multiple_of(j * tile, tile)
        k = k_ref[0, pl.ds(start, tile), cols]
        ck = ck_ref[0, pl.ds(head, 1), pl.ds(start, tile)] * LOG2E
        s = lax.dot_general(q_ref[0, :, cols], k, (((1,), (1,)), ((), ())),
                            preferred_element_type=F32)
        return s - ck, start

    for hh, head, cols, cq in heads:
        s, start = logits(qi, head, cols)
        r = lax.broadcasted_iota(I32, s.shape, 0)
        c = lax.broadcasted_iota(I32, s.shape, 1)
        s = jnp.where(c <= r, s, -jnp.inf)
        m0 = jnp.max(s, axis=1, keepdims=True) + cq
        p = jnp.exp2(s + (cq - m0))
        m_ref[hh] = m0
        pv, row_sum = _weighted_values(p, v_ref[0, pl.ds(start, tile), cols])
        l_ref[hh] = row_sum
        acc_ref[hh] = pv

    def step(j, _):
        for hh, head, cols, cq in heads:
            s, start = logits(j, head, cols)
            m_old = m_ref[hh]
            m_new = jnp.maximum(m_old, jnp.max(s, axis=1, keepdims=True) + cq)
            alpha = jnp.exp2(m_old - m_new)
            p = jnp.exp2(s + (cq - m_new))
            pv, row_sum = _weighted_values(p, v_ref[0, pl.ds(start, tile), cols])
            l_ref[hh] = alpha * l_ref[hh] + row_sum
            acc_ref[hh] = alpha * acc_ref[hh] + pv
            m_ref[hh] = m_new
        return 0

    lax.fori_loop(0, qi, step, 0)
    for hh, head, cols, cq in heads:
        o_ref[0, :, cols] = (acc_ref[hh] / l_ref[hh]).astype(o_ref.dtype)


def fox_attention(proj3, c_row, c_col, tile=512):
    b, l, _ = proj3.shape
    hps = FOX_HEADS_PER_STEP
    w = hps * FOX_HEAD_DIM
    q0 = POOL_WIDTH // w
    k0 = q0 + FOX_HEADS // hps
    v0 = k0 + FOX_HEADS // hps
    return pl.pallas_call(
        functools.partial(_fox_body, tile=tile),
        grid=(b, FOX_HEADS // hps, l // tile),
        in_specs=[
            pl.BlockSpec((1, tile, w), lambda i, h, q: (i, q, q0 + h)),
            pl.BlockSpec((1, l, w), lambda i, h, q: (i, 0, k0 + h)),
            pl.BlockSpec((1, l, w), lambda i, h, q: (i, 0, v0 + h)),
            pl.BlockSpec((1, tile, FOX_HEADS), lambda i, h, q: (i, q, 0)),
            pl.BlockSpec((1, FOX_HEADS, l), lambda i, h, q: (i, 0, 0)),
        ],
        out_specs=pl.BlockSpec((1, tile, w), lambda i, h, q: (i, q, h)),
        out_shape=jax.ShapeDtypeStruct((b, l, FOX_WIDTH), BF16),
        scratch_shapes=[pltpu.VMEM((hps, tile, 1), F32),
                        pltpu.VMEM((hps, tile, 1), F32),
                        pltpu.VMEM((hps, tile, FOX_HEAD_DIM), F32)],
        compiler_params=_params(("arbitrary", "arbitrary", "arbitrary"), 32),
        name="fox_attention",
    )(proj3, proj3, proj3, c_col, c_row)


def _ssd_dt_body(raw_ref, bias_ref, alog_ref, dt_ref, acs_ref):
    z = raw_ref[...] + bias_ref[...]
    dt = jnp.maximum(z, 0.0) + jnp.log1p(jnp.exp(-jnp.abs(z)))
    a_dt = dt * (-jnp.exp(alog_ref[...]))
    n = z.shape[0]
    r = lax.broadcasted_iota(I32, (n, n), 0)
    c = lax.broadcasted_iota(I32, (n, n), 1)
    tri = (c <= r).astype(F32)
    acs = jnp.dot(tri, a_dt, preferred_element_type=F32,
                  precision=lax.Precision.HIGHEST)
    dt_ref[...] = dt.T
    acs_ref[...] = acs.T


def ssd_dt(raw, bias_pad, alog_pad):
    t, n = raw.shape
    assert n == SSD_CHUNK
    spec = pl.BlockSpec((SSD_CHUNK, n), lambda i: (i, 0))
    spec_t = pl.BlockSpec((n, SSD_CHUNK), lambda i: (0, i))
    vec = pl.BlockSpec((1, n), lambda i: (0, 0))
    return pl.pallas_call(
        _ssd_dt_body,
        grid=(t // SSD_CHUNK,),
        in_specs=[spec, vec, vec],
        out_specs=[spec_t, spec_t],
        out_shape=[jax.ShapeDtypeStruct((n, t), F32)] * 2,
        compiler_params=_params(("arbitrary",), 16),
        name="ssd_dt",
    )(raw, bias_pad.reshape(1, n), alog_pad.reshape(1, n))


def _silu(x):
    return x / (1.0 + jnp.exp(-x))


def _causal_conv_silu(u_ref, w_ref, b_ref):
    u = u_ref[0].astype(F32)
    w = w_ref[...]

    def conv(v, causal_rows):
        out = b_ref[...] + v * w[SSD_CONV - 1:SSD_CONV, :]
        for shift in range(1, SSD_CONV):
            prev = pltpu.roll(v, shift, axis=0)
            if causal_rows is not None:
                prev = jnp.where(causal_rows >= shift, prev, 0.0)
            out = out + prev * w[SSD_CONV - 1 - shift:SSD_CONV - shift, :]
        return out

    head = u[:SUBLANES]
    head_rows = lax.broadcasted_iota(I32, head.shape, 0)
    out = jnp.concatenate([conv(head, head_rows), conv(u, None)[SUBLANES:]], axis=0)
    return _silu(out)


def _pair_lanes(cols, j):
    rows = cols.shape[0]
    lane = lax.broadcasted_iota(I32, (rows, LANES), 1)
    lo = jnp.broadcast_to(cols[:, 2 * j:2 * j + 1], (rows, LANES))
    hi = jnp.broadcast_to(cols[:, 2 * j + 1:2 * j + 2], (rows, LANES))
    return jnp.where(lane < SSD_HEAD_DIM, lo, hi)


def _ssd_body(z_ref, x_ref, b_ref, c_ref, wx_ref, wb_ref, wc_ref,
              bx_ref, bb_ref, bc_ref, dtt_ref, acst_ref,
              dskip_ref, gn_ref, o_ref, xs_ref, bs_ref, cs_ref, st_ref):
    q = SSD_CHUNK
    xs_ref[...] = _causal_conv_silu(x_ref, wx_ref, bx_ref)
    bs_ref[...] = _causal_conv_silu(b_ref, wb_ref, bb_ref).astype(BF16)
    cs_ref[...] = _causal_conv_silu(c_ref, wc_ref, bc_ref).astype(BF16)
    st_ref[...] = jnp.zeros_like(st_ref)

    r = lax.broadcasted_iota(I32, (q, q), 0)
    c = lax.broadcasted_iota(I32, (q, q), 1)
    causal = c <= r
    lane = lax.broadcasted_iota(I32, (q, LANES), 1)
    first_head = lane < SSD_HEAD_DIM

    def chunk(ci, _):
        r0 = pl.multiple_of(ci * q, q)
        xc = xs_ref[pl.ds(r0, q), :]
        bm = bs_ref[pl.ds(r0, q), :]
        cm = cs_ref[pl.ds(r0, q), :]
        zc = z_ref[0, pl.ds(r0, q), :].astype(F32)
        dt_t = dtt_ref[0, :, pl.ds(r0, q)]
        acs_t = acst_ref[0, :, pl.ds(r0, q)]
        acs = acs_t.T
        exp_acs = jnp.exp(acs)
        chunk_decay = jnp.exp(acs[q - 1:q, :])
        w_diag_t = dt_t
        w_end_t = dt_t * jnp.exp(acs_t[:, q - 1:q] - acs_t)
        cb = lax.dot_general(cm, bm, (((1,), (1,)), ((), ())),
                             preferred_element_type=F32)
        bm_t = bm.astype(F32).T
        gated = []
        for j in range(SSD_GROUP_HEADS // 2):
            lanes = slice(j * LANES, (j + 1) * LANES)
            xp = xc[:, lanes]
            xp_b = xp.astype(BF16)
            y_heads = []
            s_heads = []
            for hh in range(2):
                hd = 2 * j + hh
                seg = jnp.broadcast_to(acs[:, hd:hd + 1], (q, q)) - acs_t[hd:hd + 1, :]
                decay = jnp.exp(jnp.where(causal, seg, -jnp.inf))
                mix = cb * decay * w_diag_t[hd:hd + 1, :]
                y_heads.append(jnp.dot(mix.astype(BF16), xp_b,
                                       preferred_element_type=F32))
                b_end = (bm_t * w_end_t[hd:hd + 1, :]).astype(BF16)
                s_heads.append(jnp.dot(b_end, xp_b, preferred_element_type=F32))
            y = jnp.where(first_head, y_heads[0], y_heads[1])
            state = st_ref[:, lanes]
            y = y + jnp.dot(cm, state.astype(BF16),
                            preferred_element_type=F32) * _pair_lanes(exp_acs, j)
            y = y + xp * dskip_ref[:, lanes]
            st_ref[:, lanes] = state * _pair_lanes(chunk_decay, j) + jnp.where(
                first_head, s_heads[0], s_heads[1])
            gated.append(y * _silu(zc[:, lanes]))
        gated = jnp.concatenate(gated, axis=1)
        inv = lax.rsqrt(jnp.mean(gated * gated, axis=-1, keepdims=True) + EPS)
        o_ref[0, pl.ds(r0, q), :] = (gated * inv * gn_ref[...]).astype(o_ref.dtype)
        return 0

    lax.fori_loop(0, x_ref.shape[1] // q, chunk, 0)


def ssd_core(proj3, conv_w, conv_b, dt_gt, acs_gt, d_skip_exp, gnorm):
    b, l, _ = proj3.shape
    gw = SSD_GROUP_WIDTH
    n = SSD_STATE
    inner = SSD_GROUPS * gw
    x_blk0 = inner // gw
    bm_blk0 = 2 * inner // n
    cm_blk0 = bm_blk0 + SSD_GROUPS
    wb_blk0 = inner // n
    wc_blk0 = wb_blk0 + SSD_GROUPS
    hpg = SSD_GROUP_HEADS
    conv_b2 = conv_b.reshape(1, -1)
    return pl.pallas_call(
        _ssd_body,
        grid=(b, SSD_GROUPS),
        in_specs=[
            pl.BlockSpec((1, l, gw), lambda i, g: (i, 0, g)),
            pl.BlockSpec((1, l, gw), lambda i, g: (i, 0, x_blk0 + g)),
            pl.BlockSpec((1, l, n), lambda i, g: (i, 0, bm_blk0 + g)),
            pl.BlockSpec((1, l, n), lambda i, g: (i, 0, cm_blk0 + g)),
            pl.BlockSpec((SSD_CONV, gw), lambda i, g: (0, g)),
            pl.BlockSpec((SSD_CONV, n), lambda i, g: (0, wb_blk0 + g)),
            pl.BlockSpec((SSD_CONV, n), lambda i, g: (0, wc_blk0 + g)),
            pl.BlockSpec((1, gw), lambda i, g: (0, g)),
            pl.BlockSpec((1, n), lambda i, g: (0, wb_blk0 + g)),
            pl.BlockSpec((1, n), lambda i, g: (0, wc_blk0 + g)),
            pl.BlockSpec((1, hpg, l), lambda i, g: (g, 0, i)),
            pl.BlockSpec((1, hpg, l), lambda i, g: (g, 0, i)),
            pl.BlockSpec((1, gw), lambda i, g: (0, g)),
            pl.BlockSpec((1, gw), lambda i, g: (0, g)),
        ],
        out_specs=pl.BlockSpec((1, l, gw), lambda i, g: (i, 0, g)),
        out_shape=jax.ShapeDtypeStruct((b, l, inner), BF16),
        scratch_shapes=[pltpu.VMEM((l, gw), F32), pltpu.VMEM((l, n), BF16),
                        pltpu.VMEM((l, n), BF16), pltpu.VMEM((n, gw), F32)],
        compiler_params=_params(("arbitrary", "arbitrary"), 48),
        name="ssd_core",
    )(proj3, proj3, proj3, proj3, conv_w, conv_w, conv_w, conv_b2, conv_b2,
      conv_b2, dt_gt, acs_gt, d_skip_exp, gnorm.reshape(1, inner))


def _first_argmax(vals, nrows):
    row = lax.broadcasted_iota(I32, vals.shape, 0)
    top = jnp.max(vals, axis=0, keepdims=True)
    idx = jnp.min(jnp.where(vals == top, row, nrows), axis=0, keepdims=True)
    return top, idx, row


def _pack_bf16_halves(y):
    half = y.shape[1] // 2

    def bits(v):
        return lax.bitcast_convert_type(v.astype(BF16).astype(F32), U32)

    return bits(y[:, half:]) | (bits(y[:, :half]) >> 16)


def _unpack_bf16_halves(w):
    lo = lax.bitcast_convert_type(w << 16, F32)
    hi = lax.bitcast_convert_type(w & jnp.uint32(0xFFFF0000), F32)
    return lo, hi


def _router_body(x_ref, g_ref, rt_ref, rb_ref, eid_ref, wt_ref, rank_ref,
                 cnt_ref, hp_ref, carry_ref):
    step = pl.program_id(0)

    @pl.when(step == 0)
    def _():
        carry_ref[...] = jnp.zeros_like(carry_ref)

    x = x_ref[...]
    tm = x.shape[0]
    h = x * lax.rsqrt(jnp.mean(x * x, axis=-1, keepdims=True) + EPS) * g_ref[...]
    hp_ref[...] = _pack_bf16_halves(h)
    logits = lax.dot_general(rt_ref[...], h, (((1,), (1,)), ((), ())),
                             preferred_element_type=F32,
                             precision=lax.Precision.HIGHEST) + rb_ref[...]
    e_logits = logits[:MOE_EXPERTS]
    g_logits = logits[MOE_EXPERTS:MOE_EXPERTS + MOE_GROUPS]
    g_max, g_sel, _ = _first_argmax(g_logits, MOE_GROUPS)
    g_w = 1.0 / jnp.sum(jnp.exp(g_logits - g_max), axis=0, keepdims=True)
    sel = jnp.zeros((MOE_PER_GROUP, tm), F32)
    for grp in range(MOE_GROUPS):
        sel = jnp.where(g_sel == grp,
                        e_logits[grp * MOE_PER_GROUP:(grp + 1) * MOE_PER_GROUP], sel)
    m1, i1, row8 = _first_argmax(sel, MOE_PER_GROUP)
    rest = jnp.where(row8 == i1, -jnp.inf, sel)
    m2, i2, _ = _first_argmax(rest, MOE_PER_GROUP)
    p2 = jnp.exp(m2 - m1)
    w1 = g_w / (1.0 + p2)
    w2 = g_w * p2 / (1.0 + p2)
    e1 = g_sel * MOE_PER_GROUP + i1
    e2 = g_sel * MOE_PER_GROUP + i2

    r = lax.broadcasted_iota(I32, (tm, tm), 0)
    c = lax.broadcasted_iota(I32, (tm, tm), 1)
    upper = jnp.where(r <= c, 1.0, 0.0).astype(BF16)
    row32 = lax.broadcasted_iota(I32, (MOE_EXPERTS, tm), 0)
    hit1 = row32 == e1
    hit2 = row32 == e2
    cum1 = jnp.dot(jnp.where(hit1, 1.0, 0.0).astype(BF16), upper,
                   preferred_element_type=F32)
    cum2 = jnp.dot(jnp.where(hit2, 1.0, 0.0).astype(BF16), upper,
                   preferred_element_type=F32)
    carry = carry_ref[...]
    tot1 = cum1[:, tm - 1:tm]
    tot2 = cum2[:, tm - 1:tm]
    rank1 = jnp.sum(jnp.where(hit1, carry + cum1 - 1.0, 0.0), axis=0, keepdims=True)
    rank2 = jnp.sum(jnp.where(hit2, carry + tot1 + cum2 - 1.0, 0.0), axis=0,
                    keepdims=True)
    new_carry = carry + tot1 + tot2
    carry_ref[...] = new_carry

    eid_ref[...] = jnp.concatenate([e1, e2], axis=0)
    wt_ref[...] = jnp.concatenate([w1, w2], axis=0)
    rank_ref[...] = jnp.concatenate([rank1, rank2], axis=0).astype(I32)
    cnt_ref[...] = jnp.broadcast_to(new_carry, cnt_ref.shape).astype(I32)


def moe_router(x2d, g, router_g, router_g_b, router_e, router_e_b, tm=512):
    t, d = x2d.shape
    pad = 8 - MOE_GROUPS
    rt = jnp.concatenate([router_e.T, router_g.T, jnp.zeros((pad, d), F32)], axis=0)
    rb = jnp.concatenate([router_e_b, router_g_b, jnp.zeros((pad,), F32)]).reshape(-1, 1)
    nr = rt.shape[0]
    tok = pl.BlockSpec((2, tm), lambda i: (0, i))
    return pl.pallas_call(
        _router_body,
        grid=(t // tm,),
        in_specs=[pl.BlockSpec((tm, d), lambda i: (i, 0)),
                  pl.BlockSpec((1, d), lambda i: (0, 0)),
                  pl.BlockSpec((nr, d), lambda i: (0, 0)),
                  pl.BlockSpec((nr, 1), lambda i: (0, 0))],
        out_specs=[tok, tok, tok,
                   pl.BlockSpec((MOE_EXPERTS, LANES), lambda i: (0, 0)),
                   pl.BlockSpec((tm, d // 2), lambda i: (i, 0))],
        out_shape=[jax.ShapeDtypeStruct((2, t), I32),
                   jax.ShapeDtypeStruct((2, t), F32),
                   jax.ShapeDtypeStruct((2, t), I32),
                   jax.ShapeDtypeStruct((MOE_EXPERTS, LANES), I32),
                   jax.ShapeDtypeStruct((t, d // 2), U32)],
        scratch_shapes=[pltpu.VMEM((MOE_EXPERTS, 1), F32)],
        compiler_params=_params(("arbitrary",), 40),
        name="moe_router",
    )(x2d, g.reshape(1, d), rt, rb)


def _row_tokens_body(dest_ref, tok_ref):
    n_rows = tok_ref.shape[0]
    t = dest_ref.shape[0] // 2

    def clear(r, _):
        tok_ref[r] = 0
        return 0

    lax.fori_loop(0, n_rows, clear, 0, unroll=8)

    def fill(tok, _):
        tok_ref[dest_ref[tok]] = tok
        tok_ref[dest_ref[t + tok]] = tok
        return 0

    lax.fori_loop(0, t, fill, 0, unroll=8)


def moe_row_tokens(dest_flat, n_rows):
    return pl.pallas_call(
        _row_tokens_body,
        grid_spec=pltpu.PrefetchScalarGridSpec(
            num_scalar_prefetch=1,
            grid=(1,),
            in_specs=[],
            out_specs=pl.BlockSpec(memory_space=pltpu.SMEM),
        ),
        out_shape=jax.ShapeDtypeStruct((n_rows,), I32),
        compiler_params=pltpu.CompilerParams(dimension_semantics=("arbitrary",)),
        name="moe_row_tokens",
    )(dest_flat)


def _experts_body(blk_e_ref, n_used_ref, next_e_ref, grp_ref, tok_ref,
                  hp_hbm, wg_hbm, wu_hbm, wd_hbm, o_ref,
                  xbuf_ref, xsem, wg_st, wu_st, wd_st, wsem,
                  wgb_ref, wub_ref, wdb_ref, *, layer):
    i = pl.program_id(0)
    n_used = n_used_ref[0]
    used = i < n_used
    e_cur = blk_e_ref[i]
    prev = blk_e_ref[jnp.maximum(i - 1, 0)]
    fresh = jnp.logical_and(used, jnp.logical_or(i == 0, e_cur != prev))
    rows = xbuf_ref.shape[1]

    def weight_copies(e, slot):
        return (pltpu.make_async_copy(wg_hbm.at[layer, e], wg_st.at[slot],
                                      wsem.at[0, slot]),
                pltpu.make_async_copy(wu_hbm.at[layer, e], wu_st.at[slot],
                                      wsem.at[1, slot]),
                pltpu.make_async_copy(wd_hbm.at[layer, e], wd_st.at[slot],
                                      wsem.at[2, slot]))

    def row_copy(blk, slot, r):
        tok = tok_ref[blk * rows + r]
        return pltpu.make_async_copy(hp_hbm.at[pl.ds(tok, 1)],
                                     xbuf_ref.at[slot, pl.ds(r, 1)],
                                     xsem.at[slot])

    @pl.when(i == 0)
    def _():
        for cp in weight_copies(e_cur, 0):
            cp.start(priority=WEIGHT_DMA_PRIORITY)

        def first(r, _):
            row_copy(0, 0, r).start()
            return 0
        lax.fori_loop(0, rows, first, 0, unroll=8)

        @pl.when(n_used > 1)
        def _():
            def second(r, _):
                row_copy(1, 1, r).start()
                return 0
            lax.fori_loop(0, rows, second, 0, unroll=8)

    wslot = grp_ref[i] % 2
    nxt = next_e_ref[i]

    @pl.when(jnp.logical_and(fresh, nxt >= 0))
    def _():
        for cp in weight_copies(nxt, 1 - wslot):
            cp.start(priority=WEIGHT_DMA_PRIORITY)

    @pl.when(fresh)
    def _():
        for cp in weight_copies(e_cur, wslot):
            cp.wait()
        wgb_ref[...] = wg_st[wslot].astype(BF16)
        wub_ref[...] = wu_st[wslot].astype(BF16)
        wdb_ref[...] = wd_st[wslot].astype(BF16)

    def compute(gather_ahead):
        slot = i % ROW_SLOTS
        pltpu.make_async_copy(hp_hbm.at[pl.ds(0, rows)], xbuf_ref.at[slot],
                              xsem.at[slot]).wait()
        if gather_ahead:
            ahead = i + ROW_SLOTS - 1
            for r in range(rows):
                row_copy(ahead, ahead % ROW_SLOTS, r).start()
        lo, hi = _unpack_bf16_halves(xbuf_ref[slot])
        h = jnp.concatenate([lo.astype(BF16), hi.astype(BF16)], axis=1)
        gate = jnp.dot(h, wgb_ref[...], preferred_element_type=F32)
        up = jnp.dot(h, wub_ref[...], preferred_element_type=F32)
        act = (_silu(gate) * up).astype(BF16)
        y = jnp.dot(act, wdb_ref[...], preferred_element_type=F32)
        o_ref[...] = _pack_bf16_halves(y)

    has_ahead = i + ROW_SLOTS - 1 < n_used

    @pl.when(has_ahead)
    def _():
        compute(True)

    @pl.when(jnp.logical_and(used, jnp.logical_not(has_ahead)))
    def _():
        compute(False)

    @pl.when(jnp.logical_not(used))
    def _():
        o_ref[...] = jnp.zeros_like(o_ref)


def moe_experts(blk_e, n_used, next_e, grp, row_tok, hp, w_gate, w_up, w_down,
                layer):
    n_rows = row_tok.shape[0]
    half = hp.shape[1]
    d = 2 * half
    ff = w_gate.shape[3]
    nblk = n_rows // MOE_BLOCK
    hbm = pl.BlockSpec(memory_space=pl.ANY)
    return pl.pallas_call(
        functools.partial(_experts_body, layer=layer),
        grid_spec=pltpu.PrefetchScalarGridSpec(
            num_scalar_prefetch=5,
            grid=(nblk,),
            in_specs=[hbm, hbm, hbm, hbm],
            out_specs=pl.BlockSpec((MOE_BLOCK, half), lambda i, *_: (i, 0)),
            scratch_shapes=[pltpu.VMEM((ROW_SLOTS, MOE_BLOCK, half), U32),
                            pltpu.SemaphoreType.DMA((ROW_SLOTS,)),
                            pltpu.VMEM((2, d, ff), F32), pltpu.VMEM((2, d, ff), F32),
                            pltpu.VMEM((2, ff, d), F32),
                            pltpu.SemaphoreType.DMA((3, 2)),
                            pltpu.VMEM((d, ff), BF16), pltpu.VMEM((d, ff), BF16),
                            pltpu.VMEM((ff, d), BF16)],
        ),
        out_shape=jax.ShapeDtypeStruct((n_rows, half), U32),
        compiler_params=_params(("arbitrary",), 56),
        name="moe_experts",
    )(blk_e, n_used, next_e, grp, row_tok, hp, w_gate, w_up, w_down)


def _combine_body(dest_ref, x_ref, wt_ref, g_ref, ys_ref, *rest, final_norm):
    if final_norm:
        o_ref, buf_ref, sem = rest
        h_ref = None
    else:
        o_ref, h_ref, buf_ref, sem = rest
    i = pl.program_id(0)
    n = pl.num_programs(0)
    tm = x_ref.shape[0]
    t = tm * n

    def issue(tile, slot):
        def body(r, _):
            for k in range(2):
                d = dest_ref[k * t + tile * tm + r]
                pltpu.make_async_copy(ys_ref.at[pl.ds(d, 1)],
                                      buf_ref.at[slot, k, pl.ds(r, 1)],
                                      sem.at[slot]).start(priority=k)
            return 0
        lax.fori_loop(0, tm, body, 0, unroll=8)

    @pl.when(i == 0)
    def _():
        issue(0, 0)

    @pl.when(i + 1 < n)
    def _():
        issue(i + 1, (i + 1) % 2)

    slot = i % 2
    for k in range(2):
        pltpu.make_async_copy(ys_ref.at[pl.ds(0, tm)], buf_ref.at[slot, k],
                              sem.at[slot]).wait()
    w = wt_ref[...]
    lo1, hi1 = _unpack_bf16_halves(buf_ref[slot, 0])
    lo2, hi2 = _unpack_bf16_halves(buf_ref[slot, 1])
    half = lo1.shape[1]
    x = x_ref[...]
    y_lo = x[:, :half] + w[:, 0:1] * lo1 + w[:, 1:2] * lo2
    y_hi = x[:, half:] + w[:, 0:1] * hi1 + w[:, 1:2] * hi2
    ssq = (jnp.sum(y_lo * y_lo, axis=-1, keepdims=True)
           + jnp.sum(y_hi * y_hi, axis=-1, keepdims=True))
    inv = lax.rsqrt(ssq / (2 * half) + EPS)
    g = g_ref[...]
    n_lo = y_lo * inv * g[:, :half]
    n_hi = y_hi * inv * g[:, half:]
    if final_norm:
        o_ref[:, :half] = n_lo
        o_ref[:, half:] = n_hi
    else:
        o_ref[:, :half] = y_lo
        o_ref[:, half:] = y_hi
        h_ref[:, :half] = n_lo.astype(h_ref.dtype)
        h_ref[:, half:] = n_hi.astype(h_ref.dtype)


def moe_combine(dest_flat, x2d, wt_t, ys, g_norm, final_norm, tm=128):
    t, d = x2d.shape
    row_spec = pl.BlockSpec((tm, d), lambda i, ds: (i, 0))
    if final_norm:
        out_specs = row_spec
        out_shape = jax.ShapeDtypeStruct((t, d), F32)
    else:
        out_specs = [row_spec, row_spec]
        out_shape = [jax.ShapeDtypeStruct((t, d), F32),
                     jax.ShapeDtypeStruct((t, d), BF16)]
    return pl.pallas_call(
        functools.partial(_combine_body, final_norm=final_norm),
        grid_spec=pltpu.PrefetchScalarGridSpec(
            num_scalar_prefetch=1,
            grid=(t // tm,),
            in_specs=[row_spec,
                      pl.BlockSpec((tm, 2), lambda i, ds: (i, 0)),
                      pl.BlockSpec((1, d), lambda i, ds: (0, 0)),
                      pl.BlockSpec(memory_space=pl.ANY)],
            out_specs=out_specs,
            scratch_shapes=[pltpu.VMEM((2, 2, tm, d // 2), U32),
                            pltpu.SemaphoreType.DMA((2,))],
        ),
        out_shape=out_shape,
        compiler_params=_params(("arbitrary",), 32),
        name="moe_combine",
    )(dest_flat, x2d, wt_t, g_norm.reshape(1, d), ys)


def hier_moe(x2d, g_ffn, router_g, router_g_b, router_e, router_e_b,
             w_gate, w_up, w_down, layer, g_norm, final_norm):
    t, d = x2d.shape
    eid, wts, rank, cnt, hp = moe_router(x2d, g_ffn, router_g, router_g_b,
                                         router_e, router_e_b)
    counts = cnt[:, 0]
    padded = (counts + MOE_BLOCK - 1) // MOE_BLOCK * MOE_BLOCK
    pend = jnp.cumsum(padded)
    pstart = pend - padded
    n_rows = 2 * t + MOE_EXPERTS * MOE_BLOCK
    nblk = n_rows // MOE_BLOCK
    expert_ids = jnp.arange(MOE_EXPERTS, dtype=I32)
    start_of = jnp.sum(jnp.where(eid[..., None] == expert_ids, pstart, 0), axis=-1)
    dest = (start_of + rank).reshape(-1)
    blk_start = jnp.arange(nblk, dtype=I32) * MOE_BLOCK
    blk_e = jnp.sum(blk_start[:, None] >= pend[None, :], axis=1).astype(I32)
    blk_e = jnp.minimum(blk_e, MOE_EXPERTS - 1)
    n_used = (pend[-1:] // MOE_BLOCK).astype(I32)
    last_e = blk_e[jnp.maximum(n_used[0] - 1, 0)]
    blk_e = jnp.where(jnp.arange(nblk) < n_used[0], blk_e, last_e)
    changed = jnp.concatenate([jnp.ones((1,), I32),
                               (blk_e[1:] != blk_e[:-1]).astype(I32)])
    grp = jnp.cumsum(changed) - 1
    grp_end = jnp.sum(jnp.where(blk_e[:, None] == expert_ids, pend, 0), axis=-1)
    nxt_blk = grp_end // MOE_BLOCK
    nxt_e = jnp.sum(jnp.where(nxt_blk[:, None] == jnp.arange(nblk), blk_e, 0), axis=-1)
    next_e = jnp.where(nxt_blk < n_used[0], nxt_e, -1).astype(I32)
    row_tok = moe_row_tokens(dest, n_rows)
    ys = moe_experts(blk_e, n_used, next_e, grp.astype(I32), row_tok, hp,
                     w_gate, w_up, w_down, layer)
    return moe_combine(dest, x2d, wts.T, ys, g_norm, final_norm)


def _pad_cols(w, n):
    return jnp.pad(w, ((0, 0), (0, n - w.shape[1])))


def pool_fox_layer(x2d, h, b, l, j, w_in, b_forget, pool_w, pool_scale, w_out):
    main = POOL_WIDTH + 3 * FOX_WIDTH
    w_in_t = jnp.swapaxes(w_in, 1, 2)
    q_cols = (POOL_WIDTH, POOL_WIDTH + FOX_WIDTH, FOX_Q_SCALE)
    proj = matmul([h], w_in_t, j, main, w_transposed=True, scaled_cols=q_cols,
                  name="pf_in_proj")
    f = matmul([h], _pad_cols(w_in[j, :, main:], LANES)[None], 0, LANES,
               out_dtype=F32, name="pf_forget_proj")
    f_t = f[:, :FOX_HEADS].reshape(b, l, FOX_HEADS).transpose(0, 2, 1)
    c = forget_cumsum(f_t, b_forget)
    proj3 = proj.reshape(b, l, main)
    y_pool = pool_mixer(proj3, pool_w, pool_scale)
    y_att = fox_attention(proj3, c, c.transpose(0, 2, 1))
    acts = [y_pool.reshape(b * l, POOL_WIDTH), y_att.reshape(b * l, FOX_WIDTH)]
    return matmul(acts, w_out, j, w_out.shape[2], res=x2d, out_dtype=F32,
                  vmem_mib=56, name="pf_out_proj")


def ssd_layer(x2d, h, b, l, j, w_in, conv_w, conv_b, dt_bias, a_log, d_skip,
              gnorm, w_out):
    heads = dt_bias.shape[0]
    inner = heads * SSD_HEAD_DIM
    main = 2 * inner + 2 * SSD_GROUPS * SSD_STATE
    w_in_t = jnp.swapaxes(w_in, 1, 2)
    proj = matmul([h], w_in_t, j, main, w_transposed=True, name="ssd_in_proj")
    raw = matmul([h], _pad_cols(w_in[j, :, main:], LANES)[None], 0, LANES,
                 out_dtype=F32, name="ssd_dt_proj")
    dt_t, acs_t = ssd_dt(raw, jnp.pad(dt_bias, (0, LANES - heads)),
                         jnp.pad(a_log, (0, LANES - heads)))
    hpg = SSD_GROUP_HEADS
    dt_gt = dt_t[:heads].reshape(SSD_GROUPS, hpg, b * l)
    acs_gt = acs_t[:heads].reshape(SSD_GROUPS, hpg, b * l)
    d_skip_exp = jnp.repeat(d_skip, SSD_HEAD_DIM).reshape(1, inner)
    y = ssd_core(proj.reshape(b, l, main), conv_w, conv_b, dt_gt, acs_gt,
                 d_skip_exp, gnorm)
    return matmul([y.reshape(b * l, inner)], w_out, j, w_out.shape[2], res=x2d,
                  out_dtype=F32, tn=512, vmem_mib=56, name="ssd_out_proj")


def kernel(x, norm_mix, norm_ffn, norm_final, pf_w_in, pf_b_forget, pf_pool_w, pf_pool_scale, pf_w_out, ssd_w_in, ssd_conv_w, ssd_conv_b, ssd_dt_bias, ssd_a_log, ssd_d_skip, ssd_gnorm, ssd_w_out, moe_router_g, moe_router_g_b, moe_router_e, moe_router_e_b, moe_w_gate, moe_w_up, moe_w_down):
    b, l, d = x.shape
    depth = norm_mix.shape[0]
    x2d = x.reshape(b * l, d)
    h = rmsnorm(x2d, norm_mix[0], BF16)
    for i in range(depth):
        j = i // 2
        if i % 2 == 0:
            x2d = pool_fox_layer(x2d, h, b, l, j, pf_w_in, pf_b_forget[j],
                                 pf_pool_w[j], pf_pool_scale[j], pf_w_out)
        else:
            x2d = ssd_layer(x2d, h, b, l, j, ssd_w_in, ssd_conv_w[j],
                            ssd_conv_b[j], ssd_dt_bias[j], ssd_a_log[j],
                            ssd_d_skip[j], ssd_gnorm[j], ssd_w_out)
        last = i == depth - 1
        g_norm = norm_final if last else norm_mix[i + 1]
        out = hier_moe(x2d, norm_ffn[i], moe_router_g[i], moe_router_g_b[i],
                       moe_router_e[i], moe_router_e_b[i], moe_w_gate,
                       moe_w_up, moe_w_down, i, g_norm, last)
        if last:
            x2d = out
        else:
            x2d, h = out
    return x2d.reshape(b, l, d)
```

```python
import functools

import jax
import jax.numpy as jnp
from jax import lax
from jax.experimental import pallas as pl
from jax.experimental.pallas import tpu as pltpu

F32 = jnp.float32
BF16 = jnp.bfloat16
I32 = jnp.int32
U32 = jnp.uint32
EPS = 1e-6

POOL_WINDOWS = (2, 4, 8, 16)
POOL_GROUP = 256
POOL_WIDTH = 1024
FOX_HEADS = 8
FOX_HEAD_DIM = 128
FOX_WIDTH = 1024
SSD_HEAD_DIM = 64
SSD_STATE = 128
SSD_GROUPS = 8
SSD_GROUP_HEADS = 8
SSD_GROUP_WIDTH = SSD_GROUP_HEADS * SSD_HEAD_DIM
SSD_CONV = 4
SSD_CHUNK = 128
MOE_GROUPS = 4
MOE_PER_GROUP = 8
MOE_EXPERTS = 32
MOE_BLOCK = 256
WEIGHT_DMA_PRIORITY = 1
ROW_SLOTS = 3
WEIGHT_SLOTS = 3
LANES = 128
SUBLANES = 8
MIB = 1 << 20


def _params(semantics, vmem_mib):
    return pltpu.CompilerParams(dimension_semantics=semantics,
                                vmem_limit_bytes=vmem_mib * MIB)


def _rmsnorm_body(x_ref, g_ref, o_ref):
    x = x_ref[...]
    inv = lax.rsqrt(jnp.mean(x * x, axis=-1, keepdims=True) + EPS)
    o_ref[...] = (x * inv * g_ref[...]).astype(o_ref.dtype)


def rmsnorm(x2d, g, out_dtype, tm=512):
    t, d = x2d.shape
    return pl.pallas_call(
        _rmsnorm_body,
        grid=(t // tm,),
        in_specs=[pl.BlockSpec((tm, d), lambda i: (i, 0)),
                  pl.BlockSpec((1, d), lambda i: (0, 0))],
        out_specs=pl.BlockSpec((tm, d), lambda i: (i, 0)),
        out_shape=jax.ShapeDtypeStruct((t, d), out_dtype),
        compiler_params=_params(("arbitrary",), 40),
        name="rmsnorm",
    )(x2d, g.reshape(1, d))


def _matmul_body(*refs, n_act, has_res, w_transposed, scaled_cols):
    refs = list(refs)
    a_refs = refs[:n_act]
    w_ref = refs[n_act]
    o_ref, wb_ref = refs[-2:]
    r_ref = refs[-3] if has_res else None
    out_axis = 0 if w_transposed else 1

    @pl.when(pl.program_id(1) == 0)
    def _():
        w = w_ref[0]
        if scaled_cols is not None:
            lo, hi, value = scaled_cols
            col = (lax.broadcasted_iota(I32, w.shape, out_axis)
                   + pl.program_id(0) * w.shape[out_axis])
            w = jnp.where(jnp.logical_and(col >= lo, col < hi), w * value, w)
        wb_ref[...] = w.astype(BF16)

    acc = None
    k0 = 0
    for a_ref in a_refs:
        kw = a_ref.shape[1]
        if w_transposed:
            part = lax.dot_general(a_ref[...], wb_ref[:, k0:k0 + kw],
                                   (((1,), (1,)), ((), ())),
                                   preferred_element_type=F32)
        else:
            part = jnp.dot(a_ref[...], wb_ref[k0:k0 + kw, :],
                           preferred_element_type=F32)
        acc = part if acc is None else acc + part
        k0 += kw
    if has_res:
        acc = acc + r_ref[...]
    o_ref[...] = acc.astype(o_ref.dtype)


def matmul(acts, w, layer, n_out, *, w_transposed=False, res=None,
           scaled_cols=None, out_dtype=BF16, tm=1024, tn=1024, vmem_mib=48,
           name="matmul"):
    t = acts[0].shape[0]
    k = sum(a.shape[1] for a in acts)
    assert w.shape[2 if w_transposed else 1] == k
    tn = min(tn, n_out)
    tm = min(tm, t)
    assert t % tm == 0 and n_out % tn == 0
    in_specs = [pl.BlockSpec((tm, a.shape[1]), lambda j, i: (i, 0)) for a in acts]
    if w_transposed:
        in_specs.append(pl.BlockSpec((1, tn, k), lambda j, i: (layer, j, 0)))
        wb_shape = (tn, k)
    else:
        in_specs.append(pl.BlockSpec((1, k, tn), lambda j, i: (layer, 0, j)))
        wb_shape = (k, tn)
    args = list(acts) + [w]
    if res is not None:
        in_specs.append(pl.BlockSpec((tm, tn), lambda j, i: (i, j)))
        args.append(res)
    return pl.pallas_call(
        functools.partial(_matmul_body, n_act=len(acts), has_res=res is not None,
                          w_transposed=w_transposed, scaled_cols=scaled_cols),
        grid=(n_out // tn, t // tm),
        in_specs=in_specs,
        out_specs=pl.BlockSpec((tm, tn), lambda j, i: (i, j)),
        out_shape=jax.ShapeDtypeStruct((t, n_out), out_dtype),
        scratch_shapes=[pltpu.VMEM(wb_shape, BF16)],
        compiler_params=_params(("arbitrary", "arbitrary"), vmem_mib),
        name=name,
    )(*args)


def _forget_cumsum_body(f_ref, b_ref, c_ref):
    z = f_ref[0] + b_ref[...]
    x = jnp.minimum(z, 0.0) - jnp.log1p(jnp.exp(-jnp.abs(z)))
    n = x.shape[1]
    lane = lax.broadcasted_iota(I32, x.shape, 1)
    shift = 1
    while shift < n:
        x = x + jnp.where(lane >= shift, pltpu.roll(x, shift, axis=1), 0.0)
        shift *= 2
    c_ref[0] = x


def forget_cumsum(f_t, b_forget):
    b, h, l = f_t.shape
    return pl.pallas_call(
        _forget_cumsum_body,
        grid=(b,),
        in_specs=[pl.BlockSpec((1, h, l), lambda i: (i, 0, 0)),
                  pl.BlockSpec((h, 1), lambda i: (0, 0))],
        out_specs=pl.BlockSpec((1, h, l), lambda i: (i, 0, 0)),
        out_shape=jax.ShapeDtypeStruct((b, h, l), F32),
        compiler_params=_params(("arbitrary",), 16),
        name="forget_cumsum",
    )(f_t, b_forget.reshape(h, 1))


def _pool_body(u_ref, w_ref, s_ref, o_ref):
    g = pl.program_id(1)
    u = u_ref[0].astype(F32)
    row = lax.broadcasted_iota(I32, u.shape, 0)
    acc = u
    sums = []
    for shift in (1, 2, 4, 8):
        acc = acc + jnp.where(row >= shift, pltpu.roll(acc, shift, axis=0), 0.0)
        sums.append(acc)
    win_sum = jnp.where(g == 0, sums[0],
                        jnp.where(g == 1, sums[1],
                                  jnp.where(g == 2, sums[2], sums[3])))
    window = jnp.left_shift(jnp.int32(2), g)
    count = jnp.minimum(row + 1, window).astype(F32)
    mixed = win_sum / count - u
    y = jnp.dot(mixed.astype(BF16), w_ref[0].astype(BF16),
                preferred_element_type=F32)
    o_ref[0] = (y * s_ref[...]).astype(o_ref.dtype)


def pool_mixer(proj3, pool_w, pool_scale):
    b, l, _ = proj3.shape
    ng = len(POOL_WINDOWS)
    return pl.pallas_call(
        _pool_body,
        grid=(b, ng),
        in_specs=[pl.BlockSpec((1, l, POOL_GROUP), lambda i, g: (i, 0, g)),
                  pl.BlockSpec((1, POOL_GROUP, POOL_GROUP), lambda i, g: (g, 0, 0)),
                  pl.BlockSpec((1, POOL_GROUP), lambda i, g: (0, g))],
        out_specs=pl.BlockSpec((1, l, POOL_GROUP), lambda i, g: (i, 0, g)),
        out_shape=jax.ShapeDtypeStruct((b, l, POOL_WIDTH), BF16),
        compiler_params=_params(("arbitrary", "arbitrary"), 40),
        name="pool_mixer",
    )(proj3, pool_w, pool_scale.reshape(1, POOL_WIDTH))


LOG2E = 1.4426950408889634
FOX_Q_SCALE = FOX_HEAD_DIM ** -0.5 * LOG2E


FOX_HEADS_PER_STEP = 2


def _weighted_values(p, v):
    d = v.shape[1]
    v_aug = jnp.concatenate([v, jnp.ones_like(v)], axis=1)
    out = jnp.dot(p.astype(BF16), v_aug, preferred_element_type=F32)
    return out[:, :d], out[:, d:d + 1]


def _fox_body(q_ref, k_ref, v_ref, cq_ref, ck_ref, o_ref, m_ref, l_ref, acc_ref,
              *, tile):
    hp = pl.program_id(1)
    qi = pl.program_id(2)
    hd = FOX_HEAD_DIM
    cq8 = cq_ref[0]
    head_lane = lax.broadcasted_iota(I32, cq8.shape, 1)
    heads = []
    for hh in range(FOX_HEADS_PER_STEP):
        head = hp * FOX_HEADS_PER_STEP + hh
        cq = jnp.sum(jnp.where(head_lane == head, cq8, 0.0), axis=1,
                     keepdims=True) * LOG2E
        heads.append((hh, head, slice(hh * hd, (hh + 1) * hd), cq))

    def logits(j, head, cols):
        start = pl.multiple_of(j * tile, tile)
        k = k_ref[0, pl.ds(start, tile), cols]
        ck = ck_ref[0, pl.ds(head, 1), pl.ds(start, tile)] * LOG2E
        s = lax.dot_general(q_ref[0, :, cols], k, (((1,), (1,)), ((), ())),
                            preferred_element_type=F32)
        return s - ck, start

    for hh, head, cols, cq in heads:
        s, start = logits(qi, head, cols)
        r = lax.broadcasted_iota(I32, s.shape, 0)
        c = lax.broadcasted_iota(I32, s.shape, 1)
        s = jnp.where(c <= r, s, -jnp.inf)
        m0 = jnp.max(s, axis=1, keepdims=True) + cq
        p = jnp.exp2(s + (cq - m0))
        m_ref[hh] = m0
        pv, row_sum = _weighted_values(p, v_ref[0, pl.ds(start, tile), cols])
        l_ref[hh] = row_sum
        acc_ref[hh] = pv

    def step(j, _):
        for hh, head, cols, cq in heads:
            s, start = logits(j, head, cols)
            m_old = m_ref[hh]
            m_new = jnp.maximum(m_old, jnp.max(s, axis=1, keepdims=True) + cq)
            alpha = jnp.exp2(m_old - m_new)
            p = jnp.exp2(s + (cq - m_new))
            pv, row_sum = _weighted_values(p, v_ref[0, pl.ds(start, tile), cols])
            l_ref[hh] = alpha * l_ref[hh] + row_sum
            acc_ref[hh] = alpha * acc_ref[hh] + pv
            m_ref[hh] = m_new
        return 0

    lax.fori_loop(0, qi, step, 0)
    for hh, head, cols, cq in heads:
        o_ref[0, :, cols] = (acc_ref[hh] / l_ref[hh]).astype(o_ref.dtype)


def fox_attention(proj3, c_row, c_col, tile=512):
    b, l, _ = proj3.shape
    hps = FOX_HEADS_PER_STEP
    w = hps * FOX_HEAD_DIM
    q0 = POOL_WIDTH // w
    k0 = q0 + FOX_HEADS // hps
    v0 = k0 + FOX_HEADS // hps
    return pl.pallas_call(
        functools.partial(_fox_body, tile=tile),
        grid=(b, FOX_HEADS // hps, l // tile),
        in_specs=[
            pl.BlockSpec((1, tile, w), lambda i, h, q: (i, q, q0 + h)),
            pl.BlockSpec((1, l, w), lambda i, h, q: (i, 0, k0 + h)),
            pl.BlockSpec((1, l, w), lambda i, h, q: (i, 0, v0 + h)),
            pl.BlockSpec((1, tile, FOX_HEADS), lambda i, h, q: (i, q, 0)),
            pl.BlockSpec((1, FOX_HEADS, l), lambda i, h, q: (i, 0, 0)),
        ],
        out_specs=pl.BlockSpec((1, tile, w), lambda i, h, q: (i, q, h)),
        out_shape=jax.ShapeDtypeStruct((b, l, FOX_WIDTH), BF16),
        scratch_shapes=[pltpu.VMEM((hps, tile, 1), F32),
                        pltpu.VMEM((hps, tile, 1), F32),
                        pltpu.VMEM((hps, tile, FOX_HEAD_DIM), F32)],
        compiler_params=_params(("arbitrary", "arbitrary", "arbitrary"), 32),
        name="fox_attention",
    )(proj3, proj3, proj3, c_col, c_row)


def _ssd_dt_body(raw_ref, bias_ref, alog_ref, dt_ref, acs_ref):
    z = raw_ref[...] + bias_ref[...]
    dt = jnp.maximum(z, 0.0) + jnp.log1p(jnp.exp(-jnp.abs(z)))
    a_dt = dt * (-jnp.exp(alog_ref[...]))
    n = z.shape[0]
    r = lax.broadcasted_iota(I32, (n, n), 0)
    c = lax.broadcasted_iota(I32, (n, n), 1)
    tri = (c <= r).astype(F32)
    acs = jnp.dot(tri, a_dt, preferred_element_type=F32,
                  precision=lax.Precision.HIGHEST)
    dt_ref[...] = dt.T
    acs_ref[...] = acs.T


def ssd_dt(raw, bias_pad, alog_pad):
    t, n = raw.shape
    assert n == SSD_CHUNK
    spec = pl.BlockSpec((SSD_CHUNK, n), lambda i: (i, 0))
    spec_t = pl.BlockSpec((n, SSD_CHUNK), lambda i: (0, i))
    vec = pl.BlockSpec((1, n), lambda i: (0, 0))
    return pl.pallas_call(
        _ssd_dt_body,
        grid=(t // SSD_CHUNK,),
        in_specs=[spec, vec, vec],
        out_specs=[spec_t, spec_t],
        out_shape=[jax.ShapeDtypeStruct((n, t), F32)] * 2,
        compiler_params=_params(("arbitrary",), 16),
        name="ssd_dt",
    )(raw, bias_pad.reshape(1, n), alog_pad.reshape(1, n))


def _silu(x):
    return x / (1.0 + jnp.exp(-x))


def _causal_conv_silu(u_ref, w_ref, b_ref):
    u = u_ref[0].astype(F32)
    w = w_ref[...]

    def conv(v, causal_rows):
        out = b_ref[...] + v * w[SSD_CONV - 1:SSD_CONV, :]
        for shift in range(1, SSD_CONV):
            prev = pltpu.roll(v, shift, axis=0)
            if causal_rows is not None:
                prev = jnp.where(causal_rows >= shift, prev, 0.0)
            out = out + prev * w[SSD_CONV - 1 - shift:SSD_CONV - shift, :]
        return out

    head = u[:SUBLANES]
    head_rows = lax.broadcasted_iota(I32, head.shape, 0)
    out = jnp.concatenate([conv(head, head_rows), conv(u, None)[SUBLANES:]], axis=0)
    return _silu(out)


def _pair_lanes(cols, j):
    rows = cols.shape[0]
    lane = lax.broadcasted_iota(I32, (rows, LANES), 1)
    lo = jnp.broadcast_to(cols[:, 2 * j:2 * j + 1], (rows, LANES))
    hi = jnp.broadcast_to(cols[:, 2 * j + 1:2 * j + 2], (rows, LANES))
    return jnp.where(lane < SSD_HEAD_DIM, lo, hi)


def _ssd_body(z_ref, x_ref, b_ref, c_ref, wx_ref, wb_ref, wc_ref,
              bx_ref, bb_ref, bc_ref, dtt_ref, acst_ref,
              dskip_ref, gn_ref, o_ref, xs_ref, bs_ref, cs_ref, st_ref):
    q = SSD_CHUNK
    xs_ref[...] = _causal_conv_silu(x_ref, wx_ref, bx_ref)
    bs_ref[...] = _causal_conv_silu(b_ref, wb_ref, bb_ref).astype(BF16)
    cs_ref[...] = _causal_conv_silu(c_ref, wc_ref, bc_ref).astype(BF16)
    st_ref[...] = jnp.zeros_like(st_ref)

    r = lax.broadcasted_iota(I32, (q, q), 0)
    c = lax.broadcasted_iota(I32, (q, q), 1)
    causal = c <= r
    lane = lax.broadcasted_iota(I32, (q, LANES), 1)
    first_head = lane < SSD_HEAD_DIM

    def chunk(ci, _):
        r0 = pl.multiple_of(ci * q, q)
        xc = xs_ref[pl.ds(r0, q), :]
        bm = bs_ref[pl.ds(r0, q), :]
        cm = cs_ref[pl.ds(r0, q), :]
        zc = z_ref[0, pl.ds(r0, q), :].astype(F32)
        dt_t = dtt_ref[0, :, pl.ds(r0, q)]
        acs_t = acst_ref[0, :, pl.ds(r0, q)]
        acs = acs_t.T
        exp_acs = jnp.exp(acs)
        chunk_decay = jnp.exp(acs[q - 1:q, :])
        w_diag_t = dt_t
        w_end_t = dt_t * jnp.exp(acs_t[:, q - 1:q] - acs_t)
        cb = lax.dot_general(cm, bm, (((1,), (1,)), ((), ())),
                             preferred_element_type=F32)
        bm_t = bm.astype(F32).T
        gated = []
        for j in range(SSD_GROUP_HEADS // 2):
            lanes = slice(j * LANES, (j + 1) * LANES)
            xp = xc[:, lanes]
            xp_b = xp.astype(BF16)
            y_heads = []
            s_heads = []
            for hh in range(2):
                hd = 2 * j + hh
                seg = jnp.broadcast_to(acs[:, hd:hd + 1], (q, q)) - acs_t[hd:hd + 1, :]
                decay = jnp.exp(jnp.where(causal, seg, -jnp.inf))
                mix = cb * decay * w_diag_t[hd:hd + 1, :]
                y_heads.append(jnp.dot(mix.astype(BF16), xp_b,
                                       preferred_element_type=F32))
                b_end = (bm_t * w_end_t[hd:hd + 1, :]).astype(BF16)
                s_heads.append(jnp.dot(b_end, xp_b, preferred_element_type=F32))
            y = jnp.where(first_head, y_heads[0], y_heads[1])
            state = st_ref[:, lanes]
            y = y + jnp.dot(cm, state.astype(BF16),
                            preferred_element_type=F32) * _pair_lanes(exp_acs, j)
            y = y + xp * dskip_ref[:, lanes]
            st_ref[:, lanes] = state * _pair_lanes(chunk_decay, j) + jnp.where(
                first_head, s_heads[0], s_heads[1])
            gated.append(y * _silu(zc[:, lanes]))
        gated = jnp.concatenate(gated, axis=1)
        inv = lax.rsqrt(jnp.mean(gated * gated, axis=-1, keepdims=True) + EPS)
        o_ref[0, pl.ds(r0, q), :] = (gated * inv * gn_ref[...]).astype(o_ref.dtype)
        return 0

    lax.fori_loop(0, x_ref.shape[1] // q, chunk, 0)


def ssd_core(proj3, conv_w, conv_b, dt_gt, acs_gt, d_skip_exp, gnorm):
    b, l, _ = proj3.shape
    gw = SSD_GROUP_WIDTH
    n = SSD_STATE
    inner = SSD_GROUPS * gw
    x_blk0 = inner // gw
    bm_blk0 = 2 * inner // n
    cm_blk0 = bm_blk0 + SSD_GROUPS
    wb_blk0 = inner // n
    wc_blk0 = wb_blk0 + SSD_GROUPS
    hpg = SSD_GROUP_HEADS
    conv_b2 = conv_b.reshape(1, -1)
    return pl.pallas_call(
        _ssd_body,
        grid=(b, SSD_GROUPS),
        in_specs=[
            pl.BlockSpec((1, l, gw), lambda i, g: (i, 0, g)),
            pl.BlockSpec((1, l, gw), lambda i, g: (i, 0, x_blk0 + g)),
            pl.BlockSpec((1, l, n), lambda i, g: (i, 0, bm_blk0 + g)),
            pl.BlockSpec((1, l, n), lambda i, g: (i, 0, cm_blk0 + g)),
            pl.BlockSpec((SSD_CONV, gw), lambda i, g: (0, g)),
            pl.BlockSpec((SSD_CONV, n), lambda i, g: (0, wb_blk0 + g)),
            pl.BlockSpec((SSD_CONV, n), lambda i, g: (0, wc_blk0 + g)),
            pl.BlockSpec((1, gw), lambda i, g: (0, g)),
            pl.BlockSpec((1, n), lambda i, g: (0, wb_blk0 + g)),
            pl.BlockSpec((1, n), lambda i, g: (0, wc_blk0 + g)),
            pl.BlockSpec((1, hpg, l), lambda i, g: (g, 0, i)),
            pl.BlockSpec((1, hpg, l), lambda i, g: (g, 0, i)),
            pl.BlockSpec((1, gw), lambda i, g: (0, g)),
            pl.BlockSpec((1, gw), lambda i, g: (0, g)),
        ],
        out_specs=pl.BlockSpec((1, l, gw), lambda i, g: (i, 0, g)),
        out_shape=jax.ShapeDtypeStruct((b, l, inner), BF16),
        scratch_shapes=[pltpu.VMEM((l, gw), F32), pltpu.VMEM((l, n), BF16),
                        pltpu.VMEM((l, n), BF16), pltpu.VMEM((n, gw), F32)],
        compiler_params=_params(("arbitrary", "arbitrary"), 48),
        name="ssd_core",
    )(proj3, proj3, proj3, proj3, conv_w, conv_w, conv_w, conv_b2, conv_b2,
      conv_b2, dt_gt, acs_gt, d_skip_exp, gnorm.reshape(1, inner))


def _first_argmax(vals, nrows):
    row = lax.broadcasted_iota(I32, vals.shape, 0)
    top = jnp.max(vals, axis=0, keepdims=True)
    idx = jnp.min(jnp.where(vals == top, row, nrows), axis=0, keepdims=True)
    return top, idx, row


def _pack_bf16_halves(y):
    half = y.shape[1] // 2

    def bits(v):
        return lax.bitcast_convert_type(v.astype(BF16).astype(F32), U32)

    return bits(y[:, half:]) | (bits(y[:, :half]) >> 16)


def _unpack_bf16_halves(w):
    lo = lax.bitcast_convert_type(w << 16, F32)
    hi = lax.bitcast_convert_type(w & jnp.uint32(0xFFFF0000), F32)
    return lo, hi


def _router_body(x_ref, g_ref, rt_ref, rb_ref, eid_ref, wt_ref, rank_ref,
                 cnt_ref, hp_ref, carry_ref):
    step = pl.program_id(0)

    @pl.when(step == 0)
    def _():
        carry_ref[...] = jnp.zeros_like(carry_ref)

    x = x_ref[...]
    tm = x.shape[0]
    h = x * lax.rsqrt(jnp.mean(x * x, axis=-1, keepdims=True) + EPS) * g_ref[...]
    hp_ref[...] = _pack_bf16_halves(h)
    def split(v):
        hi = v.astype(BF16)
        return hi, (v - hi.astype(F32)).astype(BF16)

    def dot_nt(a, b):
        return lax.dot_general(a, b, (((1,), (1,)), ((), ())),
                               preferred_element_type=F32)

    h_hi, h_lo = split(h)
    rt_hi, rt_lo = split(rt_ref[...])
    logits = (dot_nt(rt_hi, h_hi) + dot_nt(rt_hi, h_lo) + dot_nt(rt_lo, h_hi)
              + rb_ref[...])
    e_logits = logits[:MOE_EXPERTS]
    g_logits = logits[MOE_EXPERTS:MOE_EXPERTS + MOE_GROUPS]
    g_max, g_sel, _ = _first_argmax(g_logits, MOE_GROUPS)
    g_w = 1.0 / jnp.sum(jnp.exp(g_logits - g_max), axis=0, keepdims=True)
    sel = jnp.zeros((MOE_PER_GROUP, tm), F32)
    for grp in range(MOE_GROUPS):
        sel = jnp.where(g_sel == grp,
                        e_logits[grp * MOE_PER_GROUP:(grp + 1) * MOE_PER_GROUP], sel)
    m1, i1, row8 = _first_argmax(sel, MOE_PER_GROUP)
    rest = jnp.where(row8 == i1, -jnp.inf, sel)
    m2, i2, _ = _first_argmax(rest, MOE_PER_GROUP)
    p2 = jnp.exp(m2 - m1)
    w1 = g_w / (1.0 + p2)
    w2 = g_w * p2 / (1.0 + p2)
    e1 = g_sel * MOE_PER_GROUP + i1
    e2 = g_sel * MOE_PER_GROUP + i2

    r = lax.broadcasted_iota(I32, (tm, tm), 0)
    c = lax.broadcasted_iota(I32, (tm, tm), 1)
    upper = jnp.where(r <= c, 1.0, 0.0).astype(BF16)
    row32 = lax.broadcasted_iota(I32, (MOE_EXPERTS, tm), 0)
    hit1 = row32 == e1
    hit2 = row32 == e2
    cum1 = jnp.dot(jnp.where(hit1, 1.0, 0.0).astype(BF16), upper,
                   preferred_element_type=F32)
    cum2 = jnp.dot(jnp.where(hit2, 1.0, 0.0).astype(BF16), upper,
                   preferred_element_type=F32)
    carry = carry_ref[...]
    tot1 = cum1[:, tm - 1:tm]
    tot2 = cum2[:, tm - 1:tm]
    rank1 = jnp.sum(jnp.where(hit1, carry + cum1 - 1.0, 0.0), axis=0, keepdims=True)
    rank2 = jnp.sum(jnp.where(hit2, carry + tot1 + cum2 - 1.0, 0.0), axis=0,
                    keepdims=True)
    new_carry = carry + tot1 + tot2
    carry_ref[...] = new_carry

    eid_ref[...] = jnp.concatenate([e1, e2], axis=0)
    wt_ref[...] = jnp.concatenate([w1, w2], axis=0)
    rank_ref[...] = jnp.concatenate([rank1, rank2], axis=0).astype(I32)
    cnt_ref[...] = jnp.broadcast_to(new_carry, cnt_ref.shape).astype(I32)


def moe_router(x2d, g, router_g, router_g_b, router_e, router_e_b, tm=512):
    t, d = x2d.shape
    pad = 8 - MOE_GROUPS
    rt = jnp.concatenate([router_e.T, router_g.T, jnp.zeros((pad, d), F32)], axis=0)
    rb = jnp.concatenate([router_e_b, router_g_b, jnp.zeros((pad,), F32)]).reshape(-1, 1)
    nr = rt.shape[0]
    tok = pl.BlockSpec((2, tm), lambda i: (0, i))
    return pl.pallas_call(
        _router_body,
        grid=(t // tm,),
        in_specs=[pl.BlockSpec((tm, d), lambda i: (i, 0)),
                  pl.BlockSpec((1, d), lambda i: (0, 0)),
                  pl.BlockSpec((nr, d), lambda i: (0, 0)),
                  pl.BlockSpec((nr, 1), lambda i: (0, 0))],
        out_specs=[tok, tok, tok,
                   pl.BlockSpec((MOE_EXPERTS, LANES), lambda i: (0, 0)),
                   pl.BlockSpec((tm, d // 2), lambda i: (i, 0))],
        out_shape=[jax.ShapeDtypeStruct((2, t), I32),
                   jax.ShapeDtypeStruct((2, t), F32),
                   jax.ShapeDtypeStruct((2, t), I32),
                   jax.ShapeDtypeStruct((MOE_EXPERTS, LANES), I32),
                   jax.ShapeDtypeStruct((t, d // 2), U32)],
        scratch_shapes=[pltpu.VMEM((MOE_EXPERTS, 1), F32)],
        compiler_params=_params(("arbitrary",), 40),
        name="moe_router",
    )(x2d, g.reshape(1, d), rt, rb)


def _row_tokens_body(dest_ref, tok_ref):
    n_rows = tok_ref.shape[0]
    t = dest_ref.shape[0] // 2

    def clear(r, _):
        tok_ref[r] = 0
        return 0

    lax.fori_loop(0, n_rows, clear, 0, unroll=8)

    def fill(tok, _):
        tok_ref[dest_ref[tok]] = tok
        tok_ref[dest_ref[t + tok]] = tok
        return 0

    lax.fori_loop(0, t, fill, 0, unroll=8)


def moe_row_tokens(dest_flat, n_rows):
    return pl.pallas_call(
        _row_tokens_body,
        grid_spec=pltpu.PrefetchScalarGridSpec(
            num_scalar_prefetch=1,
            grid=(1,),
            in_specs=[],
            out_specs=pl.BlockSpec(memory_space=pltpu.SMEM),
        ),
        out_shape=jax.ShapeDtypeStruct((n_rows,), I32),
        compiler_params=pltpu.CompilerParams(dimension_semantics=("arbitrary",)),
        name="moe_row_tokens",
    )(dest_flat)


def _experts_body(blk_e_ref, n_used_ref, next_e_ref, next2_e_ref, grp_ref, tok_ref,
                  hp_hbm, wg_hbm, wu_hbm, wd_hbm, o_ref,
                  xbuf_ref, xsem, wg_st, wu_st, wd_st, wsem,
                  wgb_ref, wub_ref, wdb_ref, *, layer):
    i = pl.program_id(0)
    n_used = n_used_ref[0]
    used = i < n_used
    e_cur = blk_e_ref[i]
    prev = blk_e_ref[jnp.maximum(i - 1, 0)]
    fresh = jnp.logical_and(used, jnp.logical_or(i == 0, e_cur != prev))
    rows = xbuf_ref.shape[1]

    def weight_copies(e, slot):
        return (pltpu.make_async_copy(wg_hbm.at[layer, e], wg_st.at[slot],
                                      wsem.at[0, slot]),
                pltpu.make_async_copy(wu_hbm.at[layer, e], wu_st.at[slot],
                                      wsem.at[1, slot]),
                pltpu.make_async_copy(wd_hbm.at[layer, e], wd_st.at[slot],
                                      wsem.at[2, slot]))

    def row_copy(blk, slot, r):
        tok = tok_ref[blk * rows + r]
        return pltpu.make_async_copy(hp_hbm.at[pl.ds(tok, 1)],
                                     xbuf_ref.at[slot, pl.ds(r, 1)],
                                     xsem.at[slot])

    @pl.when(i == 0)
    def _():
        for cp in weight_copies(e_cur, 0):
            cp.start(priority=WEIGHT_DMA_PRIORITY)

        @pl.when(next_e_ref[0] >= 0)
        def _():
            for cp in weight_copies(next_e_ref[0], 1):
                cp.start(priority=WEIGHT_DMA_PRIORITY)

        def first(r, _):
            row_copy(0, 0, r).start()
            return 0
        lax.fori_loop(0, rows, first, 0, unroll=8)

        @pl.when(n_used > 1)
        def _():
            def second(r, _):
                row_copy(1, 1, r).start()
                return 0
            lax.fori_loop(0, rows, second, 0, unroll=8)

    wslot = grp_ref[i] % WEIGHT_SLOTS
    ahead_e = next2_e_ref[i]

    @pl.when(jnp.logical_and(fresh, ahead_e >= 0))
    def _():
        for cp in weight_copies(ahead_e, (grp_ref[i] + 2) % WEIGHT_SLOTS):
            cp.start(priority=WEIGHT_DMA_PRIORITY)

    @pl.when(fresh)
    def _():
        for cp in weight_copies(e_cur, wslot):
            cp.wait()
        wgb_ref[...] = wg_st[wslot].astype(BF16)
        wub_ref[...] = wu_st[wslot].astype(BF16)
        wdb_ref[...] = wd_st[wslot].astype(BF16)

    def compute(gather_ahead):
        slot = i % ROW_SLOTS
        pltpu.make_async_copy(hp_hbm.at[pl.ds(0, rows)], xbuf_ref.at[slot],
                              xsem.at[slot]).wait()
        if gather_ahead:
            ahead = i + ROW_SLOTS - 1
            for r in range(rows):
                row_copy(ahead, ahead % ROW_SLOTS, r).start()
        lo, hi = _unpack_bf16_halves(xbuf_ref[slot])
        h = jnp.concatenate([lo.astype(BF16), hi.astype(BF16)], axis=1)
        gate = jnp.dot(h, wgb_ref[...], preferred_element_type=F32)
        up = jnp.dot(h, wub_ref[...], preferred_element_type=F32)
        act = (_silu(gate) * up).astype(BF16)
        y = jnp.dot(act, wdb_ref[...], preferred_element_type=F32)
        o_ref[...] = _pack_bf16_halves(y)

    has_ahead = i + ROW_SLOTS - 1 < n_used

    @pl.when(has_ahead)
    def _():
        compute(True)

    @pl.when(jnp.logical_and(used, jnp.logical_not(has_ahead)))
    def _():
        compute(False)

    @pl.when(jnp.logical_not(used))
    def _():
        o_ref[...] = jnp.zeros_like(o_ref)


def moe_experts(blk_e, n_used, next_e, next2_e, grp, row_tok, hp, w_gate, w_up,
                w_down, layer):
    n_rows = row_tok.shape[0]
    half = hp.shape[1]
    d = 2 * half
    ff = w_gate.shape[3]
    nblk = n_rows // MOE_BLOCK
    hbm = pl.BlockSpec(memory_space=pl.ANY)
    return pl.pallas_call(
        functools.partial(_experts_body, layer=layer),
        grid_spec=pltpu.PrefetchScalarGridSpec(
            num_scalar_prefetch=6,
            grid=(nblk,),
            in_specs=[hbm, hbm, hbm, hbm],
            out_specs=pl.BlockSpec((MOE_BLOCK, half), lambda i, *_: (i, 0)),
            scratch_shapes=[pltpu.VMEM((ROW_SLOTS, MOE_BLOCK, half), U32),
                            pltpu.SemaphoreType.DMA((ROW_SLOTS,)),
                            pltpu.VMEM((WEIGHT_SLOTS, d, ff), F32),
                            pltpu.VMEM((WEIGHT_SLOTS, d, ff), F32),
                            pltpu.VMEM((WEIGHT_SLOTS, ff, d), F32),
                            pltpu.SemaphoreType.DMA((3, WEIGHT_SLOTS)),
                            pltpu.VMEM((d, ff), BF16), pltpu.VMEM((d, ff), BF16),
                            pltpu.VMEM((ff, d), BF16)],
        ),
        out_shape=jax.ShapeDtypeStruct((n_rows, half), U32),
        compiler_params=_params(("arbitrary",), 60),
        name="moe_experts",
    )(blk_e, n_used, next_e, next2_e, grp, row_tok, hp, w_gate, w_up, w_down)


def _combine_body(dest_ref, x_ref, wt_ref, g_ref, ys_ref, *rest, final_norm):
    if final_norm:
        o_ref, buf_ref, sem = rest
        h_ref = None
    else:
        o_ref, h_ref, buf_ref, sem = rest
    i = pl.program_id(0)
    n = pl.num_programs(0)
    tm = x_ref.shape[0]
    t = tm * n

    def issue(tile, slot):
        def body(r, _):
            for k in range(2):
                d = dest_ref[k * t + tile * tm + r]
                pltpu.make_async_copy(ys_ref.at[pl.ds(d, 1)],
                                      buf_ref.at[slot, k, pl.ds(r, 1)],
                                      sem.at[slot]).start(priority=k)
            return 0
        lax.fori_loop(0, tm, body, 0, unroll=8)

    @pl.when(i == 0)
    def _():
        issue(0, 0)

    @pl.when(i + 1 < n)
    def _():
        issue(i + 1, (i + 1) % 2)

    slot = i % 2
    for k in range(2):
        pltpu.make_async_copy(ys_ref.at[pl.ds(0, tm)], buf_ref.at[slot, k],
                              sem.at[slot]).wait()
    w = wt_ref[...]
    lo1, hi1 = _unpack_bf16_halves(buf_ref[slot, 0])
    lo2, hi2 = _unpack_bf16_halves(buf_ref[slot, 1])
    half = lo1.shape[1]
    x = x_ref[...]
    y_lo = x[:, :half] + w[:, 0:1] * lo1 + w[:, 1:2] * lo2
    y_hi = x[:, half:] + w[:, 0:1] * hi1 + w[:, 1:2] * hi2
    ssq = (jnp.sum(y_lo * y_lo, axis=-1, keepdims=True)
           + jnp.sum(y_hi * y_hi, axis=-1, keepdims=True))
    inv = lax.rsqrt(ssq / (2 * half) + EPS)
    g = g_ref[...]
    n_lo = y_lo * inv * g[:, :half]
    n_hi = y_hi * inv * g[:, half:]
    if final_norm:
        o_ref[:, :half] = n_lo
        o_ref[:, half:] = n_hi
    else:
        o_ref[:, :half] = y_lo
        o_ref[:, half:] = y_hi
        h_ref[:, :half] = n_lo.astype(h_ref.dtype)
        h_ref[:, half:] = n_hi.astype(h_ref.dtype)


def moe_combine(dest_flat, x2d, wt_t, ys, g_norm, final_norm, tm=128):
    t, d = x2d.shape
    row_spec = pl.BlockSpec((tm, d), lambda i, ds: (i, 0))
    if final_norm:
        out_specs = row_spec
        out_shape = jax.ShapeDtypeStruct((t, d), F32)
    else:
        out_specs = [row_spec, row_spec]
        out_shape = [jax.ShapeDtypeStruct((t, d), F32),
                     jax.ShapeDtypeStruct((t, d), BF16)]
    return pl.pallas_call(
        functools.partial(_combine_body, final_norm=final_norm),
        grid_spec=pltpu.PrefetchScalarGridSpec(
            num_scalar_prefetch=1,
            grid=(t // tm,),
            in_specs=[row_spec,
                      pl.BlockSpec((tm, 2), lambda i, ds: (i, 0)),
                      pl.BlockSpec((1, d), lambda i, ds: (0, 0)),
                      pl.BlockSpec(memory_space=pl.ANY)],
            out_specs=out_specs,
            scratch_shapes=[pltpu.VMEM((2, 2, tm, d // 2), U32),
                            pltpu.SemaphoreType.DMA((2,))],
        ),
        out_shape=out_shape,
        compiler_params=_params(("arbitrary",), 32),
        name="moe_combine",
    )(dest_flat, x2d, wt_t, g_norm.reshape(1, d), ys)


def hier_moe(x2d, g_ffn, router_g, router_g_b, router_e, router_e_b,
             w_gate, w_up, w_down, layer, g_norm, final_norm):
    t, d = x2d.shape
    eid, wts, rank, cnt, hp = moe_router(x2d, g_ffn, router_g, router_g_b,
                                         router_e, router_e_b)
    counts = cnt[:, 0]
    padded = (counts + MOE_BLOCK - 1) // MOE_BLOCK * MOE_BLOCK
    pend = jnp.cumsum(padded)
    pstart = pend - padded
    n_rows = 2 * t + MOE_EXPERTS * MOE_BLOCK
    nblk = n_rows // MOE_BLOCK
    expert_ids = jnp.arange(MOE_EXPERTS, dtype=I32)
    start_of = jnp.sum(jnp.where(eid[..., None] == expert_ids, pstart, 0), axis=-1)
    dest = (start_of + rank).reshape(-1)
    blk_start = jnp.arange(nblk, dtype=I32) * MOE_BLOCK
    blk_e = jnp.sum(blk_start[:, None] >= pend[None, :], axis=1).astype(I32)
    blk_e = jnp.minimum(blk_e, MOE_EXPERTS - 1)
    n_used = (pend[-1:] // MOE_BLOCK).astype(I32)
    last_e = blk_e[jnp.maximum(n_used[0] - 1, 0)]
    blk_e = jnp.where(jnp.arange(nblk) < n_used[0], blk_e, last_e)
    changed = jnp.concatenate([jnp.ones((1,), I32),
                               (blk_e[1:] != blk_e[:-1]).astype(I32)])
    grp = jnp.cumsum(changed) - 1
    grp_end = jnp.sum(jnp.where(blk_e[:, None] == expert_ids, pend, 0), axis=-1)
    nxt_blk = grp_end // MOE_BLOCK
    nxt_e = jnp.sum(jnp.where(nxt_blk[:, None] == jnp.arange(nblk), blk_e, 0), axis=-1)
    next_e = jnp.where(nxt_blk < n_used[0], nxt_e, -1).astype(I32)
    nxt2_e = jnp.sum(jnp.where(nxt_blk[:, None] == jnp.arange(nblk), next_e, 0), axis=-1)
    next2_e = jnp.where(nxt_blk < n_used[0], nxt2_e, -1).astype(I32)
    row_tok = moe_row_tokens(dest, n_rows)
    ys = moe_experts(blk_e, n_used, next_e, next2_e, grp.astype(I32), row_tok, hp,
                     w_gate, w_up, w_down, layer)
    return moe_combine(dest, x2d, wts.T, ys, g_norm, final_norm)


def _pad_cols(w, n):
    return jnp.pad(w, ((0, 0), (0, n - w.shape[1])))


def pool_fox_layer(x2d, h, b, l, j, w_in, b_forget, pool_w, pool_scale, w_out):
    main = POOL_WIDTH + 3 * FOX_WIDTH
    w_in_t = jnp.swapaxes(w_in, 1, 2)
    q_cols = (POOL_WIDTH, POOL_WIDTH + FOX_WIDTH, FOX_Q_SCALE)
    proj = matmul([h], w_in_t, j, main, w_transposed=True, scaled_cols=q_cols,
                  name="pf_in_proj")
    f = matmul([h], _pad_cols(w_in[j, :, main:], LANES)[None], 0, LANES,
               out_dtype=F32, name="pf_forget_proj")
    f_t = f[:, :FOX_HEADS].reshape(b, l, FOX_HEADS).transpose(0, 2, 1)
    c = forget_cumsum(f_t, b_forget)
    proj3 = proj.reshape(b, l, main)
    y_pool = pool_mixer(proj3, pool_w, pool_scale)
    y_att = fox_attention(proj3, c, c.transpose(0, 2, 1))
    acts = [y_pool.reshape(b * l, POOL_WIDTH), y_att.reshape(b * l, FOX_WIDTH)]
    return matmul(acts, w_out, j, w_out.shape[2], res=x2d, out_dtype=F32,
                  vmem_mib=56, name="pf_out_proj")


def ssd_layer(x2d, h, b, l, j, w_in, conv_w, conv_b, dt_bias, a_log, d_skip,
              gnorm, w_out):
    heads = dt_bias.shape[0]
    inner = heads * SSD_HEAD_DIM
    main = 2 * inner + 2 * SSD_GROUPS * SSD_STATE
    w_in_t = jnp.swapaxes(w_in, 1, 2)
    proj = matmul([h], w_in_t, j, main, w_transposed=True, name="ssd_in_proj")
    raw = matmul([h], _pad_cols(w_in[j, :, main:], LANES)[None], 0, LANES,
                 out_dtype=F32, name="ssd_dt_proj")
    dt_t, acs_t = ssd_dt(raw, jnp.pad(dt_bias, (0, LANES - heads)),
                         jnp.pad(a_log, (0, LANES - heads)))
    hpg = SSD_GROUP_HEADS
    dt_gt = dt_t[:heads].reshape(SSD_GROUPS, hpg, b * l)
    acs_gt = acs_t[:heads].reshape(SSD_GROUPS, hpg, b * l)
    d_skip_exp = jnp.repeat(d_skip, SSD_HEAD_DIM).reshape(1, inner)
    y = ssd_core(proj.reshape(b, l, main), conv_w, conv_b, dt_gt, acs_gt,
                 d_skip_exp, gnorm)
    return matmul([y.reshape(b * l, inner)], w_out, j, w_out.shape[2], res=x2d,
                  out_dtype=F32, tn=512, vmem_mib=56, name="ssd_out_proj")


def kernel(x, norm_mix, norm_ffn, norm_final, pf_w_in, pf_b_forget, pf_pool_w, pf_pool_scale, pf_w_out, ssd_w_in, ssd_conv_w, ssd_conv_b, ssd_dt_bias, ssd_a_log, ssd_d_skip, ssd_gnorm, ssd_w_out, moe_router_g, moe_router_g_b, moe_router_e, moe_router_e_b, moe_w_gate, moe_w_up, moe_w_down):
    b, l, d = x.shape
    depth = norm_mix.shape[0]
    x2d = x.reshape(b * l, d)
    h = rmsnorm(x2d, norm_mix[0], BF16)
    for i in range(depth):
        j = i // 2
        if i % 2 == 0:
            x2d = pool_fox_layer(x2d, h, b, l, j, pf_w_in, pf_b_forget[j],
                                 pf_pool_w[j], pf_pool_scale[j], pf_w_out)
        else:
            x2d = ssd_layer(x2d, h, b, l, j, ssd_w_in, ssd_conv_w[j],
                            ssd_conv_b[j], ssd_dt_bias[j], ssd_a_log[j],
                            ssd_d_skip[j], ssd_gnorm[j], ssd_w_out)
        last = i == depth - 1
        g_norm = norm_final if last else norm_mix[i + 1]
        out = hier_moe(x2d, norm_ffn[i], moe_router_g[i], moe_router_g_b[i],
                       moe_router_e[i], moe_router_e_b[i], moe_w_gate,
                       moe_w_up, moe_w_down, i, g_norm, last)
        if last:
            x2d = out
        else:
            x2d, h = out
    return x2d.reshape(b, l, d)
```

```python
import functools

import jax
import jax.numpy as jnp
from jax import lax
from jax.experimental import pallas as pl
from jax.experimental.pallas import tpu as pltpu

F32 = jnp.float32
BF16 = jnp.bfloat16
I32 = jnp.int32
U32 = jnp.uint32
EPS = 1e-6

POOL_WINDOWS = (2, 4, 8, 16)
POOL_GROUP = 256
POOL_WIDTH = 1024
FOX_HEADS = 8
FOX_HEAD_DIM = 128
FOX_WIDTH = 1024
SSD_HEAD_DIM = 64
SSD_STATE = 128
SSD_GROUPS = 8
SSD_GROUP_HEADS = 8
SSD_GROUP_WIDTH = SSD_GROUP_HEADS * SSD_HEAD_DIM
SSD_CONV = 4
SSD_CHUNK = 128
MOE_GROUPS = 4
MOE_PER_GROUP = 8
MOE_EXPERTS = 32
MOE_BLOCK = 256
WEIGHT_DMA_PRIORITY = 1
ROW_SLOTS = 3
WEIGHT_SLOTS = 3
LANES = 128
SUBLANES = 8
MIB = 1 << 20


def _params(semantics, vmem_mib):
    return pltpu.CompilerParams(dimension_semantics=semantics,
                                vmem_limit_bytes=vmem_mib * MIB)


def _rmsnorm_body(x_ref, g_ref, o_ref):
    x = x_ref[...]
    inv = lax.rsqrt(jnp.mean(x * x, axis=-1, keepdims=True) + EPS)
    o_ref[...] = (x * inv * g_ref[...]).astype(o_ref.dtype)


def rmsnorm(x2d, g, out_dtype, tm=512):
    t, d = x2d.shape
    return pl.pallas_call(
        _rmsnorm_body,
        grid=(t // tm,),
        in_specs=[pl.BlockSpec((tm, d), lambda i: (i, 0)),
                  pl.BlockSpec((1, d), lambda i: (0, 0))],
        out_specs=pl.BlockSpec((tm, d), lambda i: (i, 0)),
        out_shape=jax.ShapeDtypeStruct((t, d), out_dtype),
        compiler_params=_params(("arbitrary",), 40),
        name="rmsnorm",
    )(x2d, g.reshape(1, d))


def _matmul_body(*refs, n_act, has_res, w_transposed, scaled_cols):
    refs = list(refs)
    a_refs = refs[:n_act]
    w_ref = refs[n_act]
    o_ref, wb_ref = refs[-2:]
    r_ref = refs[-3] if has_res else None
    out_axis = 0 if w_transposed else 1

    @pl.when(pl.program_id(1) == 0)
    def _():
        w = w_ref[0]
        if scaled_cols is not None:
            lo, hi, value = scaled_cols
            col = (lax.broadcasted_iota(I32, w.shape, out_axis)
                   + pl.program_id(0) * w.shape[out_axis])
            w = jnp.where(jnp.logical_and(col >= lo, col < hi), w * value, w)
        wb_ref[...] = w.astype(BF16)

    acc = None
    k0 = 0
    for a_ref in a_refs:
        kw = a_ref.shape[1]
        if w_transposed:
            part = lax.dot_general(a_ref[...], wb_ref[:, k0:k0 + kw],
                                   (((1,), (1,)), ((), ())),
                                   preferred_element_type=F32)
        else:
            part = jnp.dot(a_ref[...], wb_ref[k0:k0 + kw, :],
                           preferred_element_type=F32)
        acc = part if acc is None else acc + part
        k0 += kw
    if has_res:
        acc = acc + r_ref[...]
    o_ref[...] = acc.astype(o_ref.dtype)


def matmul(acts, w, layer, n_out, *, w_transposed=False, res=None,
           scaled_cols=None, out_dtype=BF16, tm=1024, tn=1024, vmem_mib=48,
           name="matmul"):
    t = acts[0].shape[0]
    k = sum(a.shape[1] for a in acts)
    assert w.shape[2 if w_transposed else 1] == k
    tn = min(tn, n_out)
    tm = min(tm, t)
    assert t % tm == 0 and n_out % tn == 0
    in_specs = [pl.BlockSpec((tm, a.shape[1]), lambda j, i: (i, 0)) for a in acts]
    if w_transposed:
        in_specs.append(pl.BlockSpec((1, tn, k), lambda j, i: (layer, j, 0)))
        wb_shape = (tn, k)
    else:
        in_specs.append(pl.BlockSpec((1, k, tn), lambda j, i: (layer, 0, j)))
        wb_shape = (k, tn)
    args = list(acts) + [w]
    if res is not None:
        in_specs.append(pl.BlockSpec((tm, tn), lambda j, i: (i, j)))
        args.append(res)
    return pl.pallas_call(
        functools.partial(_matmul_body, n_act=len(acts), has_res=res is not None,
                          w_transposed=w_transposed, scaled_cols=scaled_cols),
        grid=(n_out // tn, t // tm),
        in_specs=in_specs,
        out_specs=pl.BlockSpec((tm, tn), lambda j, i: (i, j)),
        out_shape=jax.ShapeDtypeStruct((t, n_out), out_dtype),
        scratch_shapes=[pltpu.VMEM(wb_shape, BF16)],
        compiler_params=_params(("arbitrary", "arbitrary"), vmem_mib),
        name=name,
    )(*args)


def _forget_cumsum_body(f_ref, b_ref, c_ref):
    z = f_ref[0] + b_ref[...]
    x = jnp.minimum(z, 0.0) - jnp.log1p(jnp.exp(-jnp.abs(z)))
    n = x.shape[1]
    lane = lax.broadcasted_iota(I32, x.shape, 1)
    shift = 1
    while shift < n:
        x = x + jnp.where(lane >= shift, pltpu.roll(x, shift, axis=1), 0.0)
        shift *= 2
    c_ref[0] = x


def forget_cumsum(f_t, b_forget):
    b, h, l = f_t.shape
    return pl.pallas_call(
        _forget_cumsum_body,
        grid=(b,),
        in_specs=[pl.BlockSpec((1, h, l), lambda i: (i, 0, 0)),
                  pl.BlockSpec((h, 1), lambda i: (0, 0))],
        out_specs=pl.BlockSpec((1, h, l), lambda i: (i, 0, 0)),
        out_shape=jax.ShapeDtypeStruct((b, h, l), F32),
        compiler_params=_params(("arbitrary",), 16),
        name="forget_cumsum",
    )(f_t, b_forget.reshape(h, 1))


def _pool_body(u_ref, w_ref, s_ref, o_ref):
    g = pl.program_id(1)
    u = u_ref[0].astype(F32)
    row = lax.broadcasted_iota(I32, u.shape, 0)
    acc = u
    sums = []
    for shift in (1, 2, 4, 8):
        acc = acc + jnp.where(row >= shift, pltpu.roll(acc, shift, axis=0), 0.0)
        sums.append(acc)
    win_sum = jnp.where(g == 0, sums[0],
                        jnp.where(g == 1, sums[1],
                                  jnp.where(g == 2, sums[2], sums[3])))
    window = jnp.left_shift(jnp.int32(2), g)
    count = jnp.minimum(row + 1, window).astype(F32)
    mixed = win_sum / count - u
    y = jnp.dot(mixed.astype(BF16), w_ref[0].astype(BF16),
                preferred_element_type=F32)
    o_ref[0] = (y * s_ref[...]).astype(o_ref.dtype)


def pool_mixer(proj3, pool_w, pool_scale):
    b, l, _ = proj3.shape
    ng = len(POOL_WINDOWS)
    return pl.pallas_call(
        _pool_body,
        grid=(b, ng),
        in_specs=[pl.BlockSpec((1, l, POOL_GROUP), lambda i, g: (i, 0, g)),
                  pl.BlockSpec((1, POOL_GROUP, POOL_GROUP), lambda i, g: (g, 0, 0)),
                  pl.BlockSpec((1, POOL_GROUP), lambda i, g: (0, g))],
        out_specs=pl.BlockSpec((1, l, POOL_GROUP), lambda i, g: (i, 0, g)),
        out_shape=jax.ShapeDtypeStruct((b, l, POOL_WIDTH), BF16),
        compiler_params=_params(("arbitrary", "arbitrary"), 40),
        name="pool_mixer",
    )(proj3, pool_w, pool_scale.reshape(1, POOL_WIDTH))


LOG2E = 1.4426950408889634
FOX_Q_SCALE = FOX_HEAD_DIM ** -0.5 * LOG2E


FOX_HEADS_PER_STEP = 2


def _weighted_values(p, v):
    d = v.shape[1]
    v_aug = jnp.concatenate([v, jnp.ones_like(v)], axis=1)
    out = jnp.dot(p.astype(BF16), v_aug, preferred_element_type=F32)
    return out[:, :d], out[:, d:d + 1]


def _fox_body(q_ref, k_ref, v_ref, cq_ref, ck_ref, o_ref, m_ref, l_ref, acc_ref,
              *, tile):
    hp = pl.program_id(1)
    qi = pl.program_id(2)
    hd = FOX_HEAD_DIM
    cq8 = cq_ref[0]
    head_lane = lax.broadcasted_iota(I32, cq8.shape, 1)
    heads = []
    for hh in range(FOX_HEADS_PER_STEP):
        head = hp * FOX_HEADS_PER_STEP + hh
        cq = jnp.sum(jnp.where(head_lane == head, cq8, 0.0), axis=1,
                     keepdims=True) * LOG2E
        heads.append((hh, head, slice(hh * hd, (hh + 1) * hd), cq))

    def logits(j, head, cols):
        start = pl.multiple_of(j * tile, tile)
        k = k_ref[0, pl.ds(start, tile), cols]
        ck = ck_ref[0, pl.ds(head, 1), pl.ds(start, tile)] * LOG2E
        s = lax.dot_general(q_ref[0, :, cols], k, (((1,), (1,)), ((), ())),
                            preferred_element_type=F32)
        return s - ck, start

    for hh, head, cols, cq in heads:
        s, start = logits(qi, head, cols)
        r = lax.broadcasted_iota(I32, s.shape, 0)
        c = lax.broadcasted_iota(I32, s.shape, 1)
        s = jnp.where(c <= r, s, -jnp.inf)
        m0 = jnp.max(s, axis=1, keepdims=True) + cq
        p = jnp.exp2(s + (cq - m0))
        m_ref[hh] = m0
        pv, row_sum = _weighted_values(p, v_ref[0, pl.ds(start, tile), cols])
        l_ref[hh] = row_sum
        acc_ref[hh] = pv

    def step(j, _):
        for hh, head, cols, cq in heads:
            s, start = logits(j, head, cols)
            m_old = m_ref[hh]
            m_new = jnp.maximum(m_old, jnp.max(s, axis=1, keepdims=True) + cq)
            alpha = jnp.exp2(m_old - m_new)
            p = jnp.exp2(s + (cq - m_new))
            pv, row_sum = _weighted_values(p, v_ref[0, pl.ds(start, tile), cols])
            l_ref[hh] = alpha * l_ref[hh] + row_sum
            acc_ref[hh] = alpha * acc_ref[hh] + pv
            m_ref[hh] = m_new
        return 0

    lax.fori_loop(0, qi, step, 0)
    for hh, head, cols, cq in heads:
        o_ref[0, :, cols] = (acc_ref[hh] / l_ref[hh]).astype(o_ref.dtype)


def fox_attention(proj3, c_row, c_col, tile=512):
    b, l, _ = proj3.shape
    hps = FOX_HEADS_PER_STEP
    w = hps * FOX_HEAD_DIM
    q0 = POOL_WIDTH // w
    k0 = q0 + FOX_HEADS // hps
    v0 = k0 + FOX_HEADS // hps
    return pl.pallas_call(
        functools.partial(_fox_body, tile=tile),
        grid=(b, FOX_HEADS // hps, l // tile),
        in_specs=[
            pl.BlockSpec((1, tile, w), lambda i, h, q: (i, q, q0 + h)),
            pl.BlockSpec((1, l, w), lambda i, h, q: (i, 0, k0 + h)),
            pl.BlockSpec((1, l, w), lambda i, h, q: (i, 0, v0 + h)),
            pl.BlockSpec((1, tile, FOX_HEADS), lambda i, h, q: (i, q, 0)),
            pl.BlockSpec((1, FOX_HEADS, l), lambda i, h, q: (i, 0, 0)),
        ],
        out_specs=pl.BlockSpec((1, tile, w), lambda i, h, q: (i, q, h)),
        out_shape=jax.ShapeDtypeStruct((b, l, FOX_WIDTH), BF16),
        scratch_shapes=[pltpu.VMEM((hps, tile, 1), F32),
                        pltpu.VMEM((hps, tile, 1), F32),
                        pltpu.VMEM((hps, tile, FOX_HEAD_DIM), F32)],
        compiler_params=_params(("arbitrary", "arbitrary", "arbitrary"), 32),
        name="fox_attention",
    )(proj3, proj3, proj3, c_col, c_row)


def _ssd_dt_body(raw_ref, bias_ref, alog_ref, dt_ref, acs_ref):
    z = raw_ref[...] + bias_ref[...]
    dt = jnp.maximum(z, 0.0) + jnp.log1p(jnp.exp(-jnp.abs(z)))
    a_dt = dt * (-jnp.exp(alog_ref[...]))
    n = z.shape[0]
    r = lax.broadcasted_iota(I32, (n, n), 0)
    c = lax.broadcasted_iota(I32, (n, n), 1)
    tri = (c <= r).astype(F32)
    acs = jnp.dot(tri, a_dt, preferred_element_type=F32,
                  precision=lax.Precision.HIGHEST)
    dt_ref[...] = dt.T
    acs_ref[...] = acs.T


def ssd_dt(raw, bias_pad, alog_pad):
    t, n = raw.shape
    assert n == SSD_CHUNK
    spec = pl.BlockSpec((SSD_CHUNK, n), lambda i: (i, 0))
    spec_t = pl.BlockSpec((n, SSD_CHUNK), lambda i: (0, i))
    vec = pl.BlockSpec((1, n), lambda i: (0, 0))
    return pl.pallas_call(
        _ssd_dt_body,
        grid=(t // SSD_CHUNK,),
        in_specs=[spec, vec, vec],
        out_specs=[spec_t, spec_t],
        out_shape=[jax.ShapeDtypeStruct((n, t), F32)] * 2,
        compiler_params=_params(("arbitrary",), 16),
        name="ssd_dt",
    )(raw, bias_pad.reshape(1, n), alog_pad.reshape(1, n))


def _silu(x):
    return x / (1.0 + jnp.exp(-x))


def _causal_conv_silu(u_ref, w_ref, b_ref):
    u = u_ref[0].astype(F32)
    w = w_ref[...]

    def conv(v, causal_rows):
        out = b_ref[...] + v * w[SSD_CONV - 1:SSD_CONV, :]
        for shift in range(1, SSD_CONV):
            prev = pltpu.roll(v, shift, axis=0)
            if causal_rows is not None:
                prev = jnp.where(causal_rows >= shift, prev, 0.0)
            out = out + prev * w[SSD_CONV - 1 - shift:SSD_CONV - shift, :]
        return out

    head = u[:SUBLANES]
    head_rows = lax.broadcasted_iota(I32, head.shape, 0)
    out = jnp.concatenate([conv(head, head_rows), conv(u, None)[SUBLANES:]], axis=0)
    return _silu(out)


def _pair_lanes(cols, j):
    rows = cols.shape[0]
    lane = lax.broadcasted_iota(I32, (rows, LANES), 1)
    lo = jnp.broadcast_to(cols[:, 2 * j:2 * j + 1], (rows, LANES))
    hi = jnp.broadcast_to(cols[:, 2 * j + 1:2 * j + 2], (rows, LANES))
    return jnp.where(lane < SSD_HEAD_DIM, lo, hi)


def _ssd_body(z_ref, x_ref, b_ref, c_ref, wx_ref, wb_ref, wc_ref,
              bx_ref, bb_ref, bc_ref, dtt_ref, acst_ref,
              dskip_ref, gn_ref, o_ref, xs_ref, bs_ref, cs_ref, st_ref):
    q = SSD_CHUNK
    xs_ref[...] = _causal_conv_silu(x_ref, wx_ref, bx_ref)
    bs_ref[...] = _causal_conv_silu(b_ref, wb_ref, bb_ref).astype(BF16)
    cs_ref[...] = _causal_conv_silu(c_ref, wc_ref, bc_ref).astype(BF16)
    st_ref[...] = jnp.zeros_like(st_ref)

    r = lax.broadcasted_iota(I32, (q, q), 0)
    c = lax.broadcasted_iota(I32, (q, q), 1)
    causal = c <= r
    lane = lax.broadcasted_iota(I32, (q, LANES), 1)
    first_head = lane < SSD_HEAD_DIM

    def chunk(ci, _):
        r0 = pl.multiple_of(ci * q, q)
        xc = xs_ref[pl.ds(r0, q), :]
        bm = bs_ref[pl.ds(r0, q), :]
        cm = cs_ref[pl.ds(r0, q), :]
        zc = z_ref[0, pl.ds(r0, q), :].astype(F32)
        dt_t = dtt_ref[0, :, pl.ds(r0, q)]
        acs_t = acst_ref[0, :, pl.ds(r0, q)]
        acs = acs_t.T
        exp_acs = jnp.exp(acs)
        chunk_decay = jnp.exp(acs[q - 1:q, :])
        w_diag_t = dt_t
        w_end_t = dt_t * jnp.exp(acs_t[:, q - 1:q] - acs_t)
        cb = lax.dot_general(cm, bm, (((1,), (1,)), ((), ())),
                             preferred_element_type=F32)
        bm_t = bm.astype(F32).T
        gated = []
        for j in range(SSD_GROUP_HEADS // 2):
            lanes = slice(j * LANES, (j + 1) * LANES)
            xp = xc[:, lanes]
            xp_b = xp.astype(BF16)
            y_heads = []
            s_heads = []
            for hh in range(2):
                hd = 2 * j + hh
                seg = jnp.broadcast_to(acs[:, hd:hd + 1], (q, q)) - acs_t[hd:hd + 1, :]
                decay = jnp.exp(jnp.where(causal, seg, -jnp.inf))
                mix = cb * decay * w_diag_t[hd:hd + 1, :]
                y_heads.append(jnp.dot(mix.astype(BF16), xp_b,
                                       preferred_element_type=F32))
                b_end = (bm_t * w_end_t[hd:hd + 1, :]).astype(BF16)
                s_heads.append(jnp.dot(b_end, xp_b, preferred_element_type=F32))
            y = jnp.where(first_head, y_heads[0], y_heads[1])
            state = st_ref[:, lanes]
            y = y + jnp.dot(cm, state.astype(BF16),
                            preferred_element_type=F32) * _pair_lanes(exp_acs, j)
            y = y + xp * dskip_ref[:, lanes]
            st_ref[:, lanes] = state * _pair_lanes(chunk_decay, j) + jnp.where(
                first_head, s_heads[0], s_heads[1])
            gated.append(y * _silu(zc[:, lanes]))
        gated = jnp.concatenate(gated, axis=1)
        inv = lax.rsqrt(jnp.mean(gated * gated, axis=-1, keepdims=True) + EPS)
        o_ref[0, pl.ds(r0, q), :] = (gated * inv * gn_ref[...]).astype(o_ref.dtype)
        return 0

    lax.fori_loop(0, x_ref.shape[1] // q, chunk, 0)


def ssd_core(proj3, conv_w, conv_b, dt_gt, acs_gt, d_skip_exp, gnorm):
    b, l, _ = proj3.shape
    gw = SSD_GROUP_WIDTH
    n = SSD_STATE
    inner = SSD_GROUPS * gw
    x_blk0 = inner // gw
    bm_blk0 = 2 * inner // n
    cm_blk0 = bm_blk0 + SSD_GROUPS
    wb_blk0 = inner // n
    wc_blk0 = wb_blk0 + SSD_GROUPS
    hpg = SSD_GROUP_HEADS
    conv_b2 = conv_b.reshape(1, -1)
    return pl.pallas_call(
        _ssd_body,
        grid=(b, SSD_GROUPS),
        in_specs=[
            pl.BlockSpec((1, l, gw), lambda i, g: (i, 0, g)),
            pl.BlockSpec((1, l, gw), lambda i, g: (i, 0, x_blk0 + g)),
            pl.BlockSpec((1, l, n), lambda i, g: (i, 0, bm_blk0 + g)),
            pl.BlockSpec((1, l, n), lambda i, g: (i, 0, cm_blk0 + g)),
            pl.BlockSpec((SSD_CONV, gw), lambda i, g: (0, g)),
            pl.BlockSpec((SSD_CONV, n), lambda i, g: (0, wb_blk0 + g)),
            pl.BlockSpec((SSD_CONV, n), lambda i, g: (0, wc_blk0 + g)),
            pl.BlockSpec((1, gw), lambda i, g: (0, g)),
            pl.BlockSpec((1, n), lambda i, g: (0, wb_blk0 + g)),
            pl.BlockSpec((1, n), lambda i, g: (0, wc_blk0 + g)),
            pl.BlockSpec((1, hpg, l), lambda i, g: (g, 0, i)),
            pl.BlockSpec((1, hpg, l), lambda i, g: (g, 0, i)),
            pl.BlockSpec((1, gw), lambda i, g: (0, g)),
            pl.BlockSpec((1, gw), lambda i, g: (0, g)),
        ],
        out_specs=pl.BlockSpec((1, l, gw), lambda i, g: (i, 0, g)),
        out_shape=jax.ShapeDtypeStruct((b, l, inner), BF16),
        scratch_shapes=[pltpu.VMEM((l, gw), F32), pltpu.VMEM((l, n), BF16),
                        pltpu.VMEM((l, n), BF16), pltpu.VMEM((n, gw), F32)],
        compiler_params=_params(("arbitrary", "arbitrary"), 48),
        name="ssd_core",
    )(proj3, proj3, proj3, proj3, conv_w, conv_w, conv_w, conv_b2, conv_b2,
      conv_b2, dt_gt, acs_gt, d_skip_exp, gnorm.reshape(1, inner))


def _first_argmax(vals, nrows):
    row = lax.broadcasted_iota(I32, vals.shape, 0)
    top = jnp.max(vals, axis=0, keepdims=True)
    idx = jnp.min(jnp.where(vals == top, row, nrows), axis=0, keepdims=True)
    return top, idx, row


def _pack_bf16_halves(y):
    half = y.shape[1] // 2

    def bits(v):
        return lax.bitcast_convert_type(v.astype(BF16).astype(F32), U32)

    return bits(y[:, half:]) | (bits(y[:, :half]) >> 16)


def _unpack_bf16_halves(w):
    lo = lax.bitcast_convert_type(w << 16, F32)
    hi = lax.bitcast_convert_type(w & jnp.uint32(0xFFFF0000), F32)
    return lo, hi


def _router_body(x_ref, g_ref, rt_ref, rb_ref, eid_ref, wt_ref, rank_ref,
                 cnt_ref, hp_ref, carry_ref):
    step = pl.program_id(0)

    @pl.when(step == 0)
    def _():
        carry_ref[...] = jnp.zeros_like(carry_ref)

    x = x_ref[...]
    tm = x.shape[0]
    h = x * lax.rsqrt(jnp.mean(x * x, axis=-1, keepdims=True) + EPS) * g_ref[...]
    hp_ref[...] = _pack_bf16_halves(h)
    def split(v):
        hi = v.astype(BF16)
        return hi, (v - hi.astype(F32)).astype(BF16)

    def dot_nt(a, b):
        return lax.dot_general(a, b, (((1,), (1,)), ((), ())),
                               preferred_element_type=F32)

    h_hi, h_lo = split(h)
    rt_hi, rt_lo = split(rt_ref[...])
    logits = (dot_nt(rt_hi, h_hi) + dot_nt(rt_hi, h_lo) + dot_nt(rt_lo, h_hi)
              + rb_ref[...])
    e_logits = logits[:MOE_EXPERTS]
    g_logits = logits[MOE_EXPERTS:MOE_EXPERTS + MOE_GROUPS]
    g_max, g_sel, _ = _first_argmax(g_logits, MOE_GROUPS)
    g_w = 1.0 / jnp.sum(jnp.exp(g_logits - g_max), axis=0, keepdims=True)
    sel = jnp.zeros((MOE_PER_GROUP, tm), F32)
    for grp in range(MOE_GROUPS):
        sel = jnp.where(g_sel == grp,
                        e_logits[grp * MOE_PER_GROUP:(grp + 1) * MOE_PER_GROUP], sel)
    m1, i1, row8 = _first_argmax(sel, MOE_PER_GROUP)
    rest = jnp.where(row8 == i1, -jnp.inf, sel)
    m2, i2, _ = _first_argmax(rest, MOE_PER_GROUP)
    p2 = jnp.exp(m2 - m1)
    w1 = g_w / (1.0 + p2)
    w2 = g_w * p2 / (1.0 + p2)
    e1 = g_sel * MOE_PER_GROUP + i1
    e2 = g_sel * MOE_PER_GROUP + i2

    r = lax.broadcasted_iota(I32, (tm, tm), 0)
    c = lax.broadcasted_iota(I32, (tm, tm), 1)
    upper = jnp.where(r <= c, 1.0, 0.0).astype(BF16)
    row32 = lax.broadcasted_iota(I32, (MOE_EXPERTS, tm), 0)
    hit1 = row32 == e1
    hit2 = row32 == e2
    cum1 = jnp.dot(jnp.where(hit1, 1.0, 0.0).astype(BF16), upper,
                   preferred_element_type=F32)
    cum2 = jnp.dot(jnp.where(hit2, 1.0, 0.0).astype(BF16), upper,
                   preferred_element_type=F32)
    carry = carry_ref[...]
    tot1 = cum1[:, tm - 1:tm]
    tot2 = cum2[:, tm - 1:tm]
    rank1 = jnp.sum(jnp.where(hit1, carry + cum1 - 1.0, 0.0), axis=0, keepdims=True)
    rank2 = jnp.sum(jnp.where(hit2, carry + tot1 + cum2 - 1.0, 0.0), axis=0,
                    keepdims=True)
    new_carry = carry + tot1 + tot2
    carry_ref[...] = new_carry

    eid_ref[...] = jnp.concatenate([e1, e2], axis=0)
    wt_ref[...] = jnp.concatenate([w1, w2], axis=0)
    rank_ref[...] = jnp.concatenate([rank1, rank2], axis=0).astype(I32)
    cnt_ref[...] = jnp.broadcast_to(new_carry, cnt_ref.shape).astype(I32)


def moe_router(x2d, g, router_g, router_g_b, router_e, router_e_b, tm=512):
    t, d = x2d.shape
    pad = 8 - MOE_GROUPS
    rt = jnp.concatenate([router_e.T, router_g.T, jnp.zeros((pad, d), F32)], axis=0)
    rb = jnp.concatenate([router_e_b, router_g_b, jnp.zeros((pad,), F32)]).reshape(-1, 1)
    nr = rt.shape[0]
    tok = pl.BlockSpec((2, tm), lambda i: (0, i))
    return pl.pallas_call(
        _router_body,
        grid=(t // tm,),
        in_specs=[pl.BlockSpec((tm, d), lambda i: (i, 0)),
                  pl.BlockSpec((1, d), lambda i: (0, 0)),
                  pl.BlockSpec((nr, d), lambda i: (0, 0)),
                  pl.BlockSpec((nr, 1), lambda i: (0, 0))],
        out_specs=[tok, tok, tok,
                   pl.BlockSpec((MOE_EXPERTS, LANES), lambda i: (0, 0)),
                   pl.BlockSpec((tm, d // 2), lambda i: (i, 0))],
        out_shape=[jax.ShapeDtypeStruct((2, t), I32),
                   jax.ShapeDtypeStruct((2, t), F32),
                   jax.ShapeDtypeStruct((2, t), I32),
                   jax.ShapeDtypeStruct((MOE_EXPERTS, LANES), I32),
                   jax.ShapeDtypeStruct((t, d // 2), U32)],
        scratch_shapes=[pltpu.VMEM((MOE_EXPERTS, 1), F32)],
        compiler_params=_params(("arbitrary",), 40),
        name="moe_router",
    )(x2d, g.reshape(1, d), rt, rb)


def _row_tokens_body(dest_ref, tok_ref):
    n_rows = tok_ref.shape[0]
    t = dest_ref.shape[0] // 2

    def clear(r, _):
        tok_ref[r] = 0
        return 0

    lax.fori_loop(0, n_rows, clear, 0, unroll=8)

    def fill(tok, _):
        tok_ref[dest_ref[tok]] = tok
        tok_ref[dest_ref[t + tok]] = tok
        return 0

    lax.fori_loop(0, t, fill, 0, unroll=8)


def moe_row_tokens(dest_flat, n_rows):
    return pl.pallas_call(
        _row_tokens_body,
        grid_spec=pltpu.PrefetchScalarGridSpec(
            num_scalar_prefetch=1,
            grid=(1,),
            in_specs=[],
            out_specs=pl.BlockSpec(memory_space=pltpu.SMEM),
        ),
        out_shape=jax.ShapeDtypeStruct((n_rows,), I32),
        compiler_params=pltpu.CompilerParams(dimension_semantics=("arbitrary",)),
        name="moe_row_tokens",
    )(dest_flat)


def _experts_body(blk_e_ref, n_used_ref, next_e_ref, next2_e_ref, grp_ref, tok_ref,
                  hp_hbm, wg_hbm, wu_hbm, wd_hbm, o_ref,
                  xbuf_ref, xsem, wg_st, wu_st, wd_st, wsem,
                  wgb_ref, wub_ref, wdb_ref, *, layer):
    i = pl.program_id(0)
    n_used = n_used_ref[0]
    used = i < n_used
    e_cur = blk_e_ref[i]
    prev = blk_e_ref[jnp.maximum(i - 1, 0)]
    fresh = jnp.logical_and(used, jnp.logical_or(i == 0, e_cur != prev))
    rows = xbuf_ref.shape[1]

    def weight_copies(e, slot):
        return (pltpu.make_async_copy(wg_hbm.at[layer, e], wg_st.at[slot],
                                      wsem.at[0, slot]),
                pltpu.make_async_copy(wu_hbm.at[layer, e], wu_st.at[slot],
                                      wsem.at[1, slot]),
                pltpu.make_async_copy(wd_hbm.at[layer, e], wd_st.at[slot],
                                      wsem.at[2, slot]))

    def row_copy(blk, slot, r):
        tok = tok_ref[blk * rows + r]
        return pltpu.make_async_copy(hp_hbm.at[pl.ds(tok, 1)],
                                     xbuf_ref.at[slot, pl.ds(r, 1)],
                                     xsem.at[slot])

    @pl.when(i == 0)
    def _():
        for cp in weight_copies(e_cur, 0):
            cp.start(priority=WEIGHT_DMA_PRIORITY)

        @pl.when(next_e_ref[0] >= 0)
        def _():
            for cp in weight_copies(next_e_ref[0], 1):
                cp.start(priority=WEIGHT_DMA_PRIORITY)

        def first(r, _):
            row_copy(0, 0, r).start()
            return 0
        lax.fori_loop(0, rows, first, 0, unroll=8)

        @pl.when(n_used > 1)
        def _():
            def second(r, _):
                row_copy(1, 1, r).start()
                return 0
            lax.fori_loop(0, rows, second, 0, unroll=8)

    wslot = grp_ref[i] % WEIGHT_SLOTS
    ahead_e = next2_e_ref[i]

    @pl.when(jnp.logical_and(fresh, ahead_e >= 0))
    def _():
        for cp in weight_copies(ahead_e, (grp_ref[i] + 2) % WEIGHT_SLOTS):
            cp.start(priority=WEIGHT_DMA_PRIORITY)

    @pl.when(fresh)
    def _():
        for cp in weight_copies(e_cur, wslot):
            cp.wait()
        wgb_ref[...] = wg_st[wslot].astype(BF16)
        wub_ref[...] = wu_st[wslot].astype(BF16)
        wdb_ref[...] = wd_st[wslot].astype(BF16)

    def compute(gather_ahead):
        slot = i % ROW_SLOTS
        pltpu.make_async_copy(hp_hbm.at[pl.ds(0, rows)], xbuf_ref.at[slot],
                              xsem.at[slot]).wait()
        if gather_ahead:
            ahead = i + ROW_SLOTS - 1
            for r in range(rows):
                row_copy(ahead, ahead % ROW_SLOTS, r).start(priority=r % 2)
        lo, hi = _unpack_bf16_halves(xbuf_ref[slot])
        h = jnp.concatenate([lo.astype(BF16), hi.astype(BF16)], axis=1)
        gate = jnp.dot(h, wgb_ref[...], preferred_element_type=F32)
        up = jnp.dot(h, wub_ref[...], preferred_element_type=F32)
        act = (_silu(gate) * up).astype(BF16)
        y = jnp.dot(act, wdb_ref[...], preferred_element_type=F32)
        o_ref[...] = _pack_bf16_halves(y)

    has_ahead = i + ROW_SLOTS - 1 < n_used

    @pl.when(has_ahead)
    def _():
        compute(True)

    @pl.when(jnp.logical_and(used, jnp.logical_not(has_ahead)))
    def _():
        compute(False)

    @pl.when(jnp.logical_not(used))
    def _():
        o_ref[...] = jnp.zeros_like(o_ref)


def moe_experts(blk_e, n_used, next_e, next2_e, grp, row_tok, hp, w_gate, w_up,
                w_down, layer):
    n_rows = row_tok.shape[0]
    half = hp.shape[1]
    d = 2 * half
    ff = w_gate.shape[3]
    nblk = n_rows // MOE_BLOCK
    hbm = pl.BlockSpec(memory_space=pl.ANY)
    return pl.pallas_call(
        functools.partial(_experts_body, layer=layer),
        grid_spec=pltpu.PrefetchScalarGridSpec(
            num_scalar_prefetch=6,
            grid=(nblk,),
            in_specs=[hbm, hbm, hbm, hbm],
            out_specs=pl.BlockSpec((MOE_BLOCK, half), lambda i, *_: (i, 0)),
            scratch_shapes=[pltpu.VMEM((ROW_SLOTS, MOE_BLOCK, half), U32),
                            pltpu.SemaphoreType.DMA((ROW_SLOTS,)),
                            pltpu.VMEM((WEIGHT_SLOTS, d, ff), F32),
                            pltpu.VMEM((WEIGHT_SLOTS, d, ff), F32),
                            pltpu.VMEM((WEIGHT_SLOTS, ff, d), F32),
                            pltpu.SemaphoreType.DMA((3, WEIGHT_SLOTS)),
                            pltpu.VMEM((d, ff), BF16), pltpu.VMEM((d, ff), BF16),
                            pltpu.VMEM((ff, d), BF16)],
        ),
        out_shape=jax.ShapeDtypeStruct((n_rows, half), U32),
        compiler_params=_params(("arbitrary",), 60),
        name="moe_experts",
    )(blk_e, n_used, next_e, next2_e, grp, row_tok, hp, w_gate, w_up, w_down)


def _combine_body(dest_ref, x_ref, wt_ref, g_ref, ys_ref, *rest, final_norm):
    if final_norm:
        o_ref, buf_ref, sem = rest
        h_ref = None
    else:
        o_ref, h_ref, buf_ref, sem = rest
    i = pl.program_id(0)
    n = pl.num_programs(0)
    tm = x_ref.shape[0]
    t = tm * n

    def issue(tile, slot):
        def body(r, _):
            for k in range(2):
                d = dest_ref[k * t + tile * tm + r]
                pltpu.make_async_copy(ys_ref.at[pl.ds(d, 1)],
                                      buf_ref.at[slot, k, pl.ds(r, 1)],
                                      sem.at[slot]).start(priority=k)
            return 0
        lax.fori_loop(0, tm, body, 0, unroll=8)

    @pl.when(i == 0)
    def _():
        issue(0, 0)

    @pl.when(i + 1 < n)
    def _():
        issue(i + 1, (i + 1) % 2)

    slot = i % 2
    for k in range(2):
        pltpu.make_async_copy(ys_ref.at[pl.ds(0, tm)], buf_ref.at[slot, k],
                              sem.at[slot]).wait()
    w = wt_ref[...]
    lo1, hi1 = _unpack_bf16_halves(buf_ref[slot, 0])
    lo2, hi2 = _unpack_bf16_halves(buf_ref[slot, 1])
    half = lo1.shape[1]
    x = x_ref[...]
    y_lo = x[:, :half] + w[:, 0:1] * lo1 + w[:, 1:2] * lo2
    y_hi = x[:, half:] + w[:, 0:1] * hi1 + w[:, 1:2] * hi2
    ssq = (jnp.sum(y_lo * y_lo, axis=-1, keepdims=True)
           + jnp.sum(y_hi * y_hi, axis=-1, keepdims=True))
    inv = lax.rsqrt(ssq / (2 * half) + EPS)
    g = g_ref[...]
    n_lo = y_lo * inv * g[:, :half]
    n_hi = y_hi * inv * g[:, half:]
    if final_norm:
        o_ref[:, :half] = n_lo
        o_ref[:, half:] = n_hi
    else:
        o_ref[:, :half] = y_lo
        o_ref[:, half:] = y_hi
        h_ref[:, :half] = n_lo.astype(h_ref.dtype)
        h_ref[:, half:] = n_hi.astype(h_ref.dtype)


def moe_combine(dest_flat, x2d, wt_t, ys, g_norm, final_norm, tm=256):
    t, d = x2d.shape
    row_spec = pl.BlockSpec((tm, d), lambda i, ds: (i, 0))
    if final_norm:
        out_specs = row_spec
        out_shape = jax.ShapeDtypeStruct((t, d), F32)
    else:
        out_specs = [row_spec, row_spec]
        out_shape = [jax.ShapeDtypeStruct((t, d), F32),
                     jax.ShapeDtypeStruct((t, d), BF16)]
    return pl.pallas_call(
        functools.partial(_combine_body, final_norm=final_norm),
        grid_spec=pltpu.PrefetchScalarGridSpec(
            num_scalar_prefetch=1,
            grid=(t // tm,),
            in_specs=[row_spec,
                      pl.BlockSpec((tm, 2), lambda i, ds: (i, 0)),
                      pl.BlockSpec((1, d), lambda i, ds: (0, 0)),
                      pl.BlockSpec(memory_space=pl.ANY)],
            out_specs=out_specs,
            scratch_shapes=[pltpu.VMEM((2, 2, tm, d // 2), U32),
                            pltpu.SemaphoreType.DMA((2,))],
        ),
        out_shape=out_shape,
        compiler_params=_params(("arbitrary",), 32),
        name="moe_combine",
    )(dest_flat, x2d, wt_t, g_norm.reshape(1, d), ys)


def hier_moe(x2d, g_ffn, router_g, router_g_b, router_e, router_e_b,
             w_gate, w_up, w_down, layer, g_norm, final_norm):
    t, d = x2d.shape
    eid, wts, rank, cnt, hp = moe_router(x2d, g_ffn, router_g, router_g_b,
                                         router_e, router_e_b)
    counts = cnt[:, 0]
    padded = (counts + MOE_BLOCK - 1) // MOE_BLOCK * MOE_BLOCK
    pend = jnp.cumsum(padded)
    pstart = pend - padded
    n_rows = 2 * t + MOE_EXPERTS * MOE_BLOCK
    nblk = n_rows // MOE_BLOCK
    expert_ids = jnp.arange(MOE_EXPERTS, dtype=I32)
    start_of = jnp.sum(jnp.where(eid[..., None] == expert_ids, pstart, 0), axis=-1)
    dest = (start_of + rank).reshape(-1)
    blk_start = jnp.arange(nblk, dtype=I32) * MOE_BLOCK
    blk_e = jnp.sum(blk_start[:, None] >= pend[None, :], axis=1).astype(I32)
    blk_e = jnp.minimum(blk_e, MOE_EXPERTS - 1)
    n_used = (pend[-1:] // MOE_BLOCK).astype(I32)
    last_e = blk_e[jnp.maximum(n_used[0] - 1, 0)]
    blk_e = jnp.where(jnp.arange(nblk) < n_used[0], blk_e, last_e)
    changed = jnp.concatenate([jnp.ones((1,), I32),
                               (blk_e[1:] != blk_e[:-1]).astype(I32)])
    grp = jnp.cumsum(changed) - 1
    grp_end = jnp.sum(jnp.where(blk_e[:, None] == expert_ids, pend, 0), axis=-1)
    nxt_blk = grp_end // MOE_BLOCK
    nxt_e = jnp.sum(jnp.where(nxt_blk[:, None] == jnp.arange(nblk), blk_e, 0), axis=-1)
    next_e = jnp.where(nxt_blk < n_used[0], nxt_e, -1).astype(I32)
    nxt2_e = jnp.sum(jnp.where(nxt_blk[:, None] == jnp.arange(nblk), next_e, 0), axis=-1)
    next2_e = jnp.where(nxt_blk < n_used[0], nxt2_e, -1).astype(I32)
    row_tok = moe_row_tokens(dest, n_rows)
    ys = moe_experts(blk_e, n_used, next_e, next2_e, grp.astype(I32), row_tok, hp,
                     w_gate, w_up, w_down, layer)
    return moe_combine(dest, x2d, wts.T, ys, g_norm, final_norm)


def _pad_cols(w, n):
    return jnp.pad(w, ((0, 0), (0, n - w.shape[1])))


def pool_fox_layer(x2d, h, b, l, j, w_in, b_forget, pool_w, pool_scale, w_out):
    main = POOL_WIDTH + 3 * FOX_WIDTH
    w_in_t = jnp.swapaxes(w_in, 1, 2)
    q_cols = (POOL_WIDTH, POOL_WIDTH + FOX_WIDTH, FOX_Q_SCALE)
    proj = matmul([h], w_in_t, j, main, w_transposed=True, scaled_cols=q_cols,
                  name="pf_in_proj")
    f = matmul([h], _pad_cols(w_in[j, :, main:], LANES)[None], 0, LANES,
               out_dtype=F32, name="pf_forget_proj")
    f_t = f[:, :FOX_HEADS].reshape(b, l, FOX_HEADS).transpose(0, 2, 1)
    c = forget_cumsum(f_t, b_forget)
    proj3 = proj.reshape(b, l, main)
    y_pool = pool_mixer(proj3, pool_w, pool_scale)
    y_att = fox_attention(proj3, c, c.transpose(0, 2, 1))
    acts = [y_pool.reshape(b * l, POOL_WIDTH), y_att.reshape(b * l, FOX_WIDTH)]
    return matmul(acts, w_out, j, w_out.shape[2], res=x2d, out_dtype=F32,
                  vmem_mib=56, name="pf_out_proj")


def ssd_layer(x2d, h, b, l, j, w_in, conv_w, conv_b, dt_bias, a_log, d_skip,
              gnorm, w_out):
    heads = dt_bias.shape[0]
    inner = heads * SSD_HEAD_DIM
    main = 2 * inner + 2 * SSD_GROUPS * SSD_STATE
    w_in_t = jnp.swapaxes(w_in, 1, 2)
    proj = matmul([h], w_in_t, j, main, w_transposed=True, name="ssd_in_proj")
    raw = matmul([h], _pad_cols(w_in[j, :, main:], LANES)[None], 0, LANES,
                 out_dtype=F32, name="ssd_dt_proj")
    dt_t, acs_t = ssd_dt(raw, jnp.pad(dt_bias, (0, LANES - heads)),
                         jnp.pad(a_log, (0, LANES - heads)))
    hpg = SSD_GROUP_HEADS
    dt_gt = dt_t[:heads].reshape(SSD_GROUPS, hpg, b * l)
    acs_gt = acs_t[:heads].reshape(SSD_GROUPS, hpg, b * l)
    d_skip_exp = jnp.repeat(d_skip, SSD_HEAD_DIM).reshape(1, inner)
    y = ssd_core(proj.reshape(b, l, main), conv_w, conv_b, dt_gt, acs_gt,
                 d_skip_exp, gnorm)
    return matmul([y.reshape(b * l, inner)], w_out, j, w_out.shape[2], res=x2d,
                  out_dtype=F32, tn=512, vmem_mib=56, name="ssd_out_proj")


def kernel(x, norm_mix, norm_ffn, norm_final, pf_w_in, pf_b_forget, pf_pool_w, pf_pool_scale, pf_w_out, ssd_w_in, ssd_conv_w, ssd_conv_b, ssd_dt_bias, ssd_a_log, ssd_d_skip, ssd_gnorm, ssd_w_out, moe_router_g, moe_router_g_b, moe_router_e, moe_router_e_b, moe_w_gate, moe_w_up, moe_w_down):
    b, l, d = x.shape
    depth = norm_mix.shape[0]
    x2d = x.reshape(b * l, d)
    h = rmsnorm(x2d, norm_mix[0], BF16)
    for i in range(depth):
        j = i // 2
        if i % 2 == 0:
            x2d = pool_fox_layer(x2d, h, b, l, j, pf_w_in, pf_b_forget[j],
                                 pf_pool_w[j], pf_pool_scale[j], pf_w_out)
        else:
            x2d = ssd_layer(x2d, h, b, l, j, ssd_w_in, ssd_conv_w[j],
                            ssd_conv_b[j], ssd_dt_bias[j], ssd_a_log[j],
                            ssd_d_skip[j], ssd_gnorm[j], ssd_w_out)
        last = i == depth - 1
        g_norm = norm_final if last else norm_mix[i + 1]
        out = hier_moe(x2d, norm_ffn[i], moe_router_g[i], moe_router_g_b[i],
                       moe_router_e[i], moe_router_e_b[i], moe_w_gate,
                       moe_w_up, moe_w_down, i, g_norm, last)
        if last:
            x2d = out
        else:
            x2d, h = out
    return x2d.reshape(b, l, d)
```

```python
import functools

import jax
import jax.numpy as jnp
from jax import lax
from jax.experimental import pallas as pl
from jax.experimental.pallas import tpu as pltpu

F32 = jnp.float32
BF16 = jnp.bfloat16
I32 = jnp.int32
U32 = jnp.uint32
EPS = 1e-6

POOL_WINDOWS = (2, 4, 8, 16)
POOL_GROUP = 256
POOL_WIDTH = 1024
FOX_HEADS = 8
FOX_HEAD_DIM = 128
FOX_WIDTH = 1024
SSD_HEAD_DIM = 64
SSD_STATE = 128
SSD_GROUPS = 8
SSD_GROUP_HEADS = 8
SSD_GROUP_WIDTH = SSD_GROUP_HEADS * SSD_HEAD_DIM
SSD_CONV = 4
SSD_CHUNK = 128
MOE_GROUPS = 4
MOE_PER_GROUP = 8
MOE_EXPERTS = 32
MOE_BLOCK = 256
WEIGHT_DMA_PRIORITY = 1
ROW_SLOTS = 3
WEIGHT_SLOTS = 3
LANES = 128
SUBLANES = 8
MIB = 1 << 20


def _params(semantics, vmem_mib):
    return pltpu.CompilerParams(dimension_semantics=semantics,
                                vmem_limit_bytes=vmem_mib * MIB)


def _rmsnorm_body(x_ref, g_ref, o_ref):
    x = x_ref[...]
    inv = lax.rsqrt(jnp.mean(x * x, axis=-1, keepdims=True) + EPS)
    o_ref[...] = (x * inv * g_ref[...]).astype(o_ref.dtype)


def rmsnorm(x2d, g, out_dtype, tm=512):
    t, d = x2d.shape
    return pl.pallas_call(
        _rmsnorm_body,
        grid=(t // tm,),
        in_specs=[pl.BlockSpec((tm, d), lambda i: (i, 0)),
                  pl.BlockSpec((1, d), lambda i: (0, 0))],
        out_specs=pl.BlockSpec((tm, d), lambda i: (i, 0)),
        out_shape=jax.ShapeDtypeStruct((t, d), out_dtype),
        compiler_params=_params(("arbitrary",), 40),
        name="rmsnorm",
    )(x2d, g.reshape(1, d))


def _matmul_body(*refs, n_act, has_res, w_transposed, scaled_cols):
    refs = list(refs)
    a_refs = refs[:n_act]
    w_ref = refs[n_act]
    o_ref, wb_ref = refs[-2:]
    r_ref = refs[-3] if has_res else None
    out_axis = 0 if w_transposed else 1

    @pl.when(pl.program_id(1) == 0)
    def _():
        w = w_ref[0]
        if scaled_cols is not None:
            lo, hi, value = scaled_cols
            col = (lax.broadcasted_iota(I32, w.shape, out_axis)
                   + pl.program_id(0) * w.shape[out_axis])
            w = jnp.where(jnp.logical_and(col >= lo, col < hi), w * value, w)
        wb_ref[...] = w.astype(BF16)

    acc = None
    k0 = 0
    for a_ref in a_refs:
        kw = a_ref.shape[1]
        if w_transposed:
            part = lax.dot_general(a_ref[...], wb_ref[:, k0:k0 + kw],
                                   (((1,), (1,)), ((), ())),
                                   preferred_element_type=F32)
        else:
            part = jnp.dot(a_ref[...], wb_ref[k0:k0 + kw, :],
                           preferred_element_type=F32)
        acc = part if acc is None else acc + part
        k0 += kw
    if has_res:
        acc = acc + r_ref[...]
    o_ref[...] = acc.astype(o_ref.dtype)


def matmul(acts, w, layer, n_out, *, w_transposed=False, res=None,
           scaled_cols=None, out_dtype=BF16, tm=1024, tn=1024, vmem_mib=48,
           name="matmul"):
    t = acts[0].shape[0]
    k = sum(a.shape[1] for a in acts)
    assert w.shape[2 if w_transposed else 1] == k
    tn = min(tn, n_out)
    tm = min(tm, t)
    assert t % tm == 0 and n_out % tn == 0
    in_specs = [pl.BlockSpec((tm, a.shape[1]), lambda j, i: (i, 0)) for a in acts]
    if w_transposed:
        in_specs.append(pl.BlockSpec((1, tn, k), lambda j, i: (layer, j, 0)))
        wb_shape = (tn, k)
    else:
        in_specs.append(pl.BlockSpec((1, k, tn), lambda j, i: (layer, 0, j)))
        wb_shape = (k, tn)
    args = list(acts) + [w]
    if res is not None:
        in_specs.append(pl.BlockSpec((tm, tn), lambda j, i: (i, j)))
        args.append(res)
    return pl.pallas_call(
        functools.partial(_matmul_body, n_act=len(acts), has_res=res is not None,
                          w_transposed=w_transposed, scaled_cols=scaled_cols),
        grid=(n_out // tn, t // tm),
        in_specs=in_specs,
        out_specs=pl.BlockSpec((tm, tn), lambda j, i: (i, j)),
        out_shape=jax.ShapeDtypeStruct((t, n_out), out_dtype),
        scratch_shapes=[pltpu.VMEM(wb_shape, BF16)],
        compiler_params=_params(("arbitrary", "arbitrary"), vmem_mib),
        name=name,
    )(*args)


def _forget_cumsum_body(f_ref, b_ref, c_ref):
    z = f_ref[0] + b_ref[...]
    x = jnp.minimum(z, 0.0) - jnp.log1p(jnp.exp(-jnp.abs(z)))
    n = x.shape[1]
    lane = lax.broadcasted_iota(I32, x.shape, 1)
    shift = 1
    while shift < n:
        x = x + jnp.where(lane >= shift, pltpu.roll(x, shift, axis=1), 0.0)
        shift *= 2
    c_ref[0] = x


def forget_cumsum(f_t, b_forget):
    b, h, l = f_t.shape
    return pl.pallas_call(
        _forget_cumsum_body,
        grid=(b,),
        in_specs=[pl.BlockSpec((1, h, l), lambda i: (i, 0, 0)),
                  pl.BlockSpec((h, 1), lambda i: (0, 0))],
        out_specs=pl.BlockSpec((1, h, l), lambda i: (i, 0, 0)),
        out_shape=jax.ShapeDtypeStruct((b, h, l), F32),
        compiler_params=_params(("arbitrary",), 16),
        name="forget_cumsum",
    )(f_t, b_forget.reshape(h, 1))


def _pool_body(u_ref, w_ref, s_ref, o_ref):
    g = pl.program_id(1)
    u = u_ref[0].astype(F32)
    row = lax.broadcasted_iota(I32, u.shape, 0)
    acc = u
    sums = []
    for shift in (1, 2, 4, 8):
        acc = acc + jnp.where(row >= shift, pltpu.roll(acc, shift, axis=0), 0.0)
        sums.append(acc)
    win_sum = jnp.where(g == 0, sums[0],
                        jnp.where(g == 1, sums[1],
                                  jnp.where(g == 2, sums[2], sums[3])))
    window = jnp.left_shift(jnp.int32(2), g)
    count = jnp.minimum(row + 1, window).astype(F32)
    mixed = win_sum / count - u
    y = jnp.dot(mixed.astype(BF16), w_ref[0].astype(BF16),
                preferred_element_type=F32)
    o_ref[0] = (y * s_ref[...]).astype(o_ref.dtype)


def pool_mixer(proj3, pool_w, pool_scale):
    b, l, _ = proj3.shape
    ng = len(POOL_WINDOWS)
    return pl.pallas_call(
        _pool_body,
        grid=(b, ng),
        in_specs=[pl.BlockSpec((1, l, POOL_GROUP), lambda i, g: (i, 0, g)),
                  pl.BlockSpec((1, POOL_GROUP, POOL_GROUP), lambda i, g: (g, 0, 0)),
                  pl.BlockSpec((1, POOL_GROUP), lambda i, g: (0, g))],
        out_specs=pl.BlockSpec((1, l, POOL_GROUP), lambda i, g: (i, 0, g)),
        out_shape=jax.ShapeDtypeStruct((b, l, POOL_WIDTH), BF16),
        compiler_params=_params(("arbitrary", "arbitrary"), 40),
        name="pool_mixer",
    )(proj3, pool_w, pool_scale.reshape(1, POOL_WIDTH))


LOG2E = 1.4426950408889634
FOX_Q_SCALE = FOX_HEAD_DIM ** -0.5 * LOG2E


FOX_HEADS_PER_STEP = 2


def _weighted_values(p, v):
    d = v.shape[1]
    v_aug = jnp.concatenate([v, jnp.ones_like(v)], axis=1)
    out = jnp.dot(p.astype(BF16), v_aug, preferred_element_type=F32)
    return out[:, :d], out[:, d:d + 1]


def _fox_body(q_ref, k_ref, v_ref, cq_ref, ck_ref, o_ref, m_ref, l_ref, acc_ref,
              *, tile):
    hp = pl.program_id(1)
    qi = pl.program_id(2)
    hd = FOX_HEAD_DIM
    cq8 = cq_ref[0]
    head_lane = lax.broadcasted_iota(I32, cq8.shape, 1)
    heads = []
    for hh in range(FOX_HEADS_PER_STEP):
        head = hp * FOX_HEADS_PER_STEP + hh
        cq = jnp.sum(jnp.where(head_lane == head, cq8, 0.0), axis=1,
                     keepdims=True) * LOG2E
        heads.append((hh, head, slice(hh * hd, (hh + 1) * hd), cq))

    def logits(j, head, cols):
        start = pl.multiple_of(j * tile, tile)
        k = k_ref[0, pl.ds(start, tile), cols]
        ck = ck_ref[0, pl.ds(head, 1), pl.ds(start, tile)] * LOG2E
        s = lax.dot_general(q_ref[0, :, cols], k, (((1,), (1,)), ((), ())),
                            preferred_element_type=F32)
        return s - ck, start

    for hh, head, cols, cq in heads:
        s, start = logits(qi, head, cols)
        r = lax.broadcasted_iota(I32, s.shape, 0)
        c = lax.broadcasted_iota(I32, s.shape, 1)
        s = jnp.where(c <= r, s, -jnp.inf)
        m0 = jnp.max(s, axis=1, keepdims=True) + cq
        p = jnp.exp2(s + (cq - m0))
        m_ref[hh] = m0
        pv, row_sum = _weighted_values(p, v_ref[0, pl.ds(start, tile), cols])
        l_ref[hh] = row_sum
        acc_ref[hh] = pv

    def step(j, _):
        for hh, head, cols, cq in heads:
            s, start = logits(j, head, cols)
            m_old = m_ref[hh]
            m_new = jnp.maximum(m_old, jnp.max(s, axis=1, keepdims=True) + cq)
            alpha = jnp.exp2(m_old - m_new)
            p = jnp.exp2(s + (cq - m_new))
            pv, row_sum = _weighted_values(p, v_ref[0, pl.ds(start, tile), cols])
            l_ref[hh] = alpha * l_ref[hh] + row_sum
            acc_ref[hh] = alpha * acc_ref[hh] + pv
            m_ref[hh] = m_new
        return 0

    lax.fori_loop(0, qi, step, 0)
    for hh, head, cols, cq in heads:
        o_ref[0, :, cols] = (acc_ref[hh] / l_ref[hh]).astype(o_ref.dtype)


def fox_attention(proj3, c_row, c_col, tile=512):
    b, l, _ = proj3.shape
    hps = FOX_HEADS_PER_STEP
    w = hps * FOX_HEAD_DIM
    q0 = POOL_WIDTH // w
    k0 = q0 + FOX_HEADS // hps
    v0 = k0 + FOX_HEADS // hps
    return pl.pallas_call(
        functools.partial(_fox_body, tile=tile),
        grid=(b, FOX_HEADS // hps, l // tile),
        in_specs=[
            pl.BlockSpec((1, tile, w), lambda i, h, q: (i, q, q0 + h)),
            pl.BlockSpec((1, l, w), lambda i, h, q: (i, 0, k0 + h)),
            pl.BlockSpec((1, l, w), lambda i, h, q: (i, 0, v0 + h)),
            pl.BlockSpec((1, tile, FOX_HEADS), lambda i, h, q: (i, q, 0)),
            pl.BlockSpec((1, FOX_HEADS, l), lambda i, h, q: (i, 0, 0)),
        ],
        out_specs=pl.BlockSpec((1, tile, w), lambda i, h, q: (i, q, h)),
        out_shape=jax.ShapeDtypeStruct((b, l, FOX_WIDTH), BF16),
        scratch_shapes=[pltpu.VMEM((hps, tile, 1), F32),
                        pltpu.VMEM((hps, tile, 1), F32),
                        pltpu.VMEM((hps, tile, FOX_HEAD_DIM), F32)],
        compiler_params=_params(("arbitrary", "arbitrary", "arbitrary"), 32),
        name="fox_attention",
    )(proj3, proj3, proj3, c_col, c_row)


def _ssd_dt_body(raw_ref, bias_ref, alog_ref, dt_ref, acs_ref):
    z = raw_ref[...] + bias_ref[...]
    dt = jnp.maximum(z, 0.0) + jnp.log1p(jnp.exp(-jnp.abs(z)))
    a_dt = dt * (-jnp.exp(alog_ref[...]))
    n = z.shape[0]
    r = lax.broadcasted_iota(I32, (n, n), 0)
    c = lax.broadcasted_iota(I32, (n, n), 1)
    tri = (c <= r).astype(F32)
    acs = jnp.dot(tri, a_dt, preferred_element_type=F32,
                  precision=lax.Precision.HIGHEST)
    dt_ref[...] = dt.T
    acs_ref[...] = acs.T


def ssd_dt(raw, bias_pad, alog_pad):
    t, n = raw.shape
    assert n == SSD_CHUNK
    spec = pl.BlockSpec((SSD_CHUNK, n), lambda i: (i, 0))
    spec_t = pl.BlockSpec((n, SSD_CHUNK), lambda i: (0, i))
    vec = pl.BlockSpec((1, n), lambda i: (0, 0))
    return pl.pallas_call(
        _ssd_dt_body,
        grid=(t // SSD_CHUNK,),
        in_specs=[spec, vec, vec],
        out_specs=[spec_t, spec_t],
        out_shape=[jax.ShapeDtypeStruct((n, t), F32)] * 2,
        compiler_params=_params(("arbitrary",), 16),
        name="ssd_dt",
    )(raw, bias_pad.reshape(1, n), alog_pad.reshape(1, n))


def _silu(x):
    return x / (1.0 + jnp.exp(-x))


def _causal_conv_silu(u_ref, w_ref, b_ref):
    u = u_ref[0].astype(F32)
    w = w_ref[...]

    def conv(v, causal_rows):
        out = b_ref[...] + v * w[SSD_CONV - 1:SSD_CONV, :]
        for shift in range(1, SSD_CONV):
            prev = pltpu.roll(v, shift, axis=0)
            if causal_rows is not None:
                prev = jnp.where(causal_rows >= shift, prev, 0.0)
            out = out + prev * w[SSD_CONV - 1 - shift:SSD_CONV - shift, :]
        return out

    head = u[:SUBLANES]
    head_rows = lax.broadcasted_iota(I32, head.shape, 0)
    out = jnp.concatenate([conv(head, head_rows), conv(u, None)[SUBLANES:]], axis=0)
    return _silu(out)


def _pair_lanes(cols, j):
    rows = cols.shape[0]
    lane = lax.broadcasted_iota(I32, (rows, LANES), 1)
    lo = jnp.broadcast_to(cols[:, 2 * j:2 * j + 1], (rows, LANES))
    hi = jnp.broadcast_to(cols[:, 2 * j + 1:2 * j + 2], (rows, LANES))
    return jnp.where(lane < SSD_HEAD_DIM, lo, hi)


def _ssd_body(z_ref, x_ref, b_ref, c_ref, wx_ref, wb_ref, wc_ref,
              bx_ref, bb_ref, bc_ref, dtt_ref, acst_ref,
              dskip_ref, gn_ref, o_ref, xs_ref, bs_ref, cs_ref, st_ref):
    q = SSD_CHUNK
    xs_ref[...] = _causal_conv_silu(x_ref, wx_ref, bx_ref)
    bs_ref[...] = _causal_conv_silu(b_ref, wb_ref, bb_ref).astype(BF16)
    cs_ref[...] = _causal_conv_silu(c_ref, wc_ref, bc_ref).astype(BF16)
    st_ref[...] = jnp.zeros_like(st_ref)

    r = lax.broadcasted_iota(I32, (q, q), 0)
    c = lax.broadcasted_iota(I32, (q, q), 1)
    causal = c <= r
    lane = lax.broadcasted_iota(I32, (q, LANES), 1)
    first_head = lane < SSD_HEAD_DIM

    def chunk(ci, _):
        r0 = pl.multiple_of(ci * q, q)
        xc = xs_ref[pl.ds(r0, q), :]
        bm = bs_ref[pl.ds(r0, q), :]
        cm = cs_ref[pl.ds(r0, q), :]
        zc = z_ref[0, pl.ds(r0, q), :].astype(F32)
        dt_t = dtt_ref[0, :, pl.ds(r0, q)]
        acs_t = acst_ref[0, :, pl.ds(r0, q)]
        acs = acs_t.T
        exp_acs = jnp.exp(acs)
        chunk_decay = jnp.exp(acs[q - 1:q, :])
        w_diag_t = dt_t
        w_end_t = dt_t * jnp.exp(acs_t[:, q - 1:q] - acs_t)
        cb = lax.dot_general(cm, bm, (((1,), (1,)), ((), ())),
                             preferred_element_type=F32)
        bm_t = bm.astype(F32).T
        gated = []
        for j in range(SSD_GROUP_HEADS // 2):
            lanes = slice(j * LANES, (j + 1) * LANES)
            xp = xc[:, lanes]
            xp_b = xp.astype(BF16)
            y_heads = []
            s_heads = []
            for hh in range(2):
                hd = 2 * j + hh
                seg = jnp.broadcast_to(acs[:, hd:hd + 1], (q, q)) - acs_t[hd:hd + 1, :]
                decay = jnp.exp(jnp.where(causal, seg, -jnp.inf))
                mix = cb * decay * w_diag_t[hd:hd + 1, :]
                y_heads.append(jnp.dot(mix.astype(BF16), xp_b,
                                       preferred_element_type=F32))
                b_end = (bm_t * w_end_t[hd:hd + 1, :]).astype(BF16)
                s_heads.append(jnp.dot(b_end, xp_b, preferred_element_type=F32))
            y = jnp.where(first_head, y_heads[0], y_heads[1])
            state = st_ref[:, lanes]
            y = y + jnp.dot(cm, state.astype(BF16),
                            preferred_element_type=F32) * _pair_lanes(exp_acs, j)
            y = y + xp * dskip_ref[:, lanes]
            st_ref[:, lanes] = state * _pair_lanes(chunk_decay, j) + jnp.where(
                first_head, s_heads[0], s_heads[1])
            gated.append(y * _silu(zc[:, lanes]))
        gated = jnp.concatenate(gated, axis=1)
        inv = lax.rsqrt(jnp.mean(gated * gated, axis=-1, keepdims=True) + EPS)
        o_ref[0, pl.ds(r0, q), :] = (gated * inv * gn_ref[...]).astype(o_ref.dtype)
        return 0

    lax.fori_loop(0, x_ref.shape[1] // q, chunk, 0)


def ssd_core(proj3, conv_w, conv_b, dt_gt, acs_gt, d_skip_exp, gnorm):
    b, l, _ = proj3.shape
    gw = SSD_GROUP_WIDTH
    n = SSD_STATE
    inner = SSD_GROUPS * gw
    x_blk0 = inner // gw
    bm_blk0 = 2 * inner // n
    cm_blk0 = bm_blk0 + SSD_GROUPS
    wb_blk0 = inner // n
    wc_blk0 = wb_blk0 + SSD_GROUPS
    hpg = SSD_GROUP_HEADS
    conv_b2 = conv_b.reshape(1, -1)
    return pl.pallas_call(
        _ssd_body,
        grid=(b, SSD_GROUPS),
        in_specs=[
            pl.BlockSpec((1, l, gw), lambda i, g: (i, 0, g)),
            pl.BlockSpec((1, l, gw), lambda i, g: (i, 0, x_blk0 + g)),
            pl.BlockSpec((1, l, n), lambda i, g: (i, 0, bm_blk0 + g)),
            pl.BlockSpec((1, l, n), lambda i, g: (i, 0, cm_blk0 + g)),
            pl.BlockSpec((SSD_CONV, gw), lambda i, g: (0, g)),
            pl.BlockSpec((SSD_CONV, n), lambda i, g: (0, wb_blk0 + g)),
            pl.BlockSpec((SSD_CONV, n), lambda i, g: (0, wc_blk0 + g)),
            pl.BlockSpec((1, gw), lambda i, g: (0, g)),
            pl.BlockSpec((1, n), lambda i, g: (0, wb_blk0 + g)),
            pl.BlockSpec((1, n), lambda i, g: (0, wc_blk0 + g)),
            pl.BlockSpec((1, hpg, l), lambda i, g: (g, 0, i)),
            pl.BlockSpec((1, hpg, l), lambda i, g: (g, 0, i)),
            pl.BlockSpec((1, gw), lambda i, g: (0, g)),
            pl.BlockSpec((1, gw), lambda i, g: (0, g)),
        ],
        out_specs=pl.BlockSpec((1, l, gw), lambda i, g: (i, 0, g)),
        out_shape=jax.ShapeDtypeStruct((b, l, inner), BF16),
        scratch_shapes=[pltpu.VMEM((l, gw), F32), pltpu.VMEM((l, n), BF16),
                        pltpu.VMEM((l, n), BF16), pltpu.VMEM((n, gw), F32)],
        compiler_params=_params(("arbitrary", "arbitrary"), 48),
        name="ssd_core",
    )(proj3, proj3, proj3, proj3, conv_w, conv_w, conv_w, conv_b2, conv_b2,
      conv_b2, dt_gt, acs_gt, d_skip_exp, gnorm.reshape(1, inner))


def _first_argmax(vals, nrows):
    row = lax.broadcasted_iota(I32, vals.shape, 0)
    top = jnp.max(vals, axis=0, keepdims=True)
    idx = jnp.min(jnp.where(vals == top, row, nrows), axis=0, keepdims=True)
    return top, idx, row


def _pack_bf16_halves(y):
    half = y.shape[1] // 2

    def bits(v):
        return lax.bitcast_convert_type(v.astype(BF16).astype(F32), U32)

    return bits(y[:, half:]) | (bits(y[:, :half]) >> 16)


def _unpack_bf16_halves(w):
    lo = lax.bitcast_convert_type(w << 16, F32)
    hi = lax.bitcast_convert_type(w & jnp.uint32(0xFFFF0000), F32)
    return lo, hi


def _router_body(x_ref, g_ref, rt_ref, rb_ref, eid_ref, wt_ref, rank_ref,
                 cnt_ref, hp_ref, carry_ref):
    step = pl.program_id(0)

    @pl.when(step == 0)
    def _():
        carry_ref[...] = jnp.zeros_like(carry_ref)

    x = x_ref[...]
    tm = x.shape[0]
    h = x * lax.rsqrt(jnp.mean(x * x, axis=-1, keepdims=True) + EPS) * g_ref[...]
    hp_ref[...] = _pack_bf16_halves(h)
    def split(v):
        hi = v.astype(BF16)
        return hi, (v - hi.astype(F32)).astype(BF16)

    def dot_nt(a, b):
        return lax.dot_general(a, b, (((1,), (1,)), ((), ())),
                               preferred_element_type=F32)

    h_hi, h_lo = split(h)
    rt_hi, rt_lo = split(rt_ref[...])
    logits = (dot_nt(rt_hi, h_hi) + dot_nt(rt_hi, h_lo) + dot_nt(rt_lo, h_hi)
              + rb_ref[...])
    e_logits = logits[:MOE_EXPERTS]
    g_logits = logits[MOE_EXPERTS:MOE_EXPERTS + MOE_GROUPS]
    g_max, g_sel, _ = _first_argmax(g_logits, MOE_GROUPS)
    g_w = 1.0 / jnp.sum(jnp.exp(g_logits - g_max), axis=0, keepdims=True)
    sel = jnp.zeros((MOE_PER_GROUP, tm), F32)
    for grp in range(MOE_GROUPS):
        sel = jnp.where(g_sel == grp,
                        e_logits[grp * MOE_PER_GROUP:(grp + 1) * MOE_PER_GROUP], sel)
    m1, i1, row8 = _first_argmax(sel, MOE_PER_GROUP)
    rest = jnp.where(row8 == i1, -jnp.inf, sel)
    m2, i2, _ = _first_argmax(rest, MOE_PER_GROUP)
    p2 = jnp.exp(m2 - m1)
    w1 = g_w / (1.0 + p2)
    w2 = g_w * p2 / (1.0 + p2)
    e1 = g_sel * MOE_PER_GROUP + i1
    e2 = g_sel * MOE_PER_GROUP + i2

    r = lax.broadcasted_iota(I32, (tm, tm), 0)
    c = lax.broadcasted_iota(I32, (tm, tm), 1)
    upper = jnp.where(r <= c, 1.0, 0.0).astype(BF16)
    row32 = lax.broadcasted_iota(I32, (MOE_EXPERTS, tm), 0)
    hit1 = row32 == e1
    hit2 = row32 == e2
    cum1 = jnp.dot(jnp.where(hit1, 1.0, 0.0).astype(BF16), upper,
                   preferred_element_type=F32)
    cum2 = jnp.dot(jnp.where(hit2, 1.0, 0.0).astype(BF16), upper,
                   preferred_element_type=F32)
    carry = carry_ref[...]
    tot1 = cum1[:, tm - 1:tm]
    tot2 = cum2[:, tm - 1:tm]
    rank1 = jnp.sum(jnp.where(hit1, carry + cum1 - 1.0, 0.0), axis=0, keepdims=True)
    rank2 = jnp.sum(jnp.where(hit2, carry + tot1 + cum2 - 1.0, 0.0), axis=0,
                    keepdims=True)
    new_carry = carry + tot1 + tot2
    carry_ref[...] = new_carry

    eid_ref[...] = jnp.concatenate([e1, e2], axis=0)
    wt_ref[...] = jnp.concatenate([w1, w2], axis=0)
    rank_ref[...] = jnp.concatenate([rank1, rank2], axis=0).astype(I32)
    cnt_ref[...] = jnp.broadcast_to(new_carry, cnt_ref.shape).astype(I32)


def moe_router(x2d, g, router_g, router_g_b, router_e, router_e_b, tm=512):
    t, d = x2d.shape
    pad = 8 - MOE_GROUPS
    rt = jnp.concatenate([router_e.T, router_g.T, jnp.zeros((pad, d), F32)], axis=0)
    rb = jnp.concatenate([router_e_b, router_g_b, jnp.zeros((pad,), F32)]).reshape(-1, 1)
    nr = rt.shape[0]
    tok = pl.BlockSpec((2, tm), lambda i: (0, i))
    return pl.pallas_call(
        _router_body,
        grid=(t // tm,),
        in_specs=[pl.BlockSpec((tm, d), lambda i: (i, 0)),
                  pl.BlockSpec((1, d), lambda i: (0, 0)),
                  pl.BlockSpec((nr, d), lambda i: (0, 0)),
                  pl.BlockSpec((nr, 1), lambda i: (0, 0))],
        out_specs=[tok, tok, tok,
                   pl.BlockSpec((MOE_EXPERTS, LANES), lambda i: (0, 0)),
                   pl.BlockSpec((tm, d // 2), lambda i: (i, 0))],
        out_shape=[jax.ShapeDtypeStruct((2, t), I32),
                   jax.ShapeDtypeStruct((2, t), F32),
                   jax.ShapeDtypeStruct((2, t), I32),
                   jax.ShapeDtypeStruct((MOE_EXPERTS, LANES), I32),
                   jax.ShapeDtypeStruct((t, d // 2), U32)],
        scratch_shapes=[pltpu.VMEM((MOE_EXPERTS, 1), F32)],
        compiler_params=_params(("arbitrary",), 40),
        name="moe_router",
    )(x2d, g.reshape(1, d), rt, rb)


def _experts_body(blk_e_ref, n_used_ref, next_e_ref, next2_e_ref, grp_ref, dest_ref,
                  hp_hbm, wg_hbm, wu_hbm, wd_hbm, o_ref,
                  xbuf_ref, xsem, wg_st, wu_st, wd_st, wsem,
                  wgb_ref, wub_ref, wdb_ref, tok_ref, *, layer):
    i = pl.program_id(0)
    n_used = n_used_ref[0]
    used = i < n_used
    e_cur = blk_e_ref[i]
    prev = blk_e_ref[jnp.maximum(i - 1, 0)]
    fresh = jnp.logical_and(used, jnp.logical_or(i == 0, e_cur != prev))
    rows = xbuf_ref.shape[1]

    def weight_copies(e, slot):
        return (pltpu.make_async_copy(wg_hbm.at[layer, e], wg_st.at[slot],
                                      wsem.at[0, slot]),
                pltpu.make_async_copy(wu_hbm.at[layer, e], wu_st.at[slot],
                                      wsem.at[1, slot]),
                pltpu.make_async_copy(wd_hbm.at[layer, e], wd_st.at[slot],
                                      wsem.at[2, slot]))

    def row_copy(blk, slot, r):
        tok = tok_ref[blk * rows + r]
        return pltpu.make_async_copy(hp_hbm.at[pl.ds(tok, 1)],
                                     xbuf_ref.at[slot, pl.ds(r, 1)],
                                     xsem.at[slot])

    @pl.when(i == 0)
    def _():
        for cp in weight_copies(e_cur, 0):
            cp.start(priority=WEIGHT_DMA_PRIORITY)

        @pl.when(next_e_ref[0] >= 0)
        def _():
            for cp in weight_copies(next_e_ref[0], 1):
                cp.start(priority=WEIGHT_DMA_PRIORITY)

        n_tok = dest_ref.shape[0] // 2

        def clear(r, _):
            tok_ref[r] = 0
            return 0
        lax.fori_loop(0, tok_ref.shape[0], clear, 0, unroll=8)

        def fill(tok, _):
            tok_ref[dest_ref[tok]] = tok
            tok_ref[dest_ref[n_tok + tok]] = tok
            return 0
        lax.fori_loop(0, n_tok, fill, 0, unroll=8)

        def first(r, _):
            row_copy(0, 0, r).start()
            return 0
        lax.fori_loop(0, rows, first, 0, unroll=8)

        @pl.when(n_used > 1)
        def _():
            def second(r, _):
                row_copy(1, 1, r).start()
                return 0
            lax.fori_loop(0, rows, second, 0, unroll=8)

    wslot = grp_ref[i] % WEIGHT_SLOTS
    ahead_e = next2_e_ref[i]

    @pl.when(jnp.logical_and(fresh, ahead_e >= 0))
    def _():
        for cp in weight_copies(ahead_e, (grp_ref[i] + 2) % WEIGHT_SLOTS):
            cp.start(priority=WEIGHT_DMA_PRIORITY)

    @pl.when(fresh)
    def _():
        for cp in weight_copies(e_cur, wslot):
            cp.wait()
        wgb_ref[...] = wg_st[wslot].astype(BF16)
        wub_ref[...] = wu_st[wslot].astype(BF16)
        wdb_ref[...] = wd_st[wslot].astype(BF16)

    def compute(gather_ahead):
        slot = i % ROW_SLOTS
        pltpu.make_async_copy(hp_hbm.at[pl.ds(0, rows)], xbuf_ref.at[slot],
                              xsem.at[slot]).wait()
        if gather_ahead:
            ahead = i + ROW_SLOTS - 1
            for r in range(rows):
                row_copy(ahead, ahead % ROW_SLOTS, r).start(priority=r % 2)
        lo, hi = _unpack_bf16_halves(xbuf_ref[slot])
        h = jnp.concatenate([lo.astype(BF16), hi.astype(BF16)], axis=1)
        gate = jnp.dot(h, wgb_ref[...], preferred_element_type=F32)
        up = jnp.dot(h, wub_ref[...], preferred_element_type=F32)
        act = (_silu(gate) * up).astype(BF16)
        y = jnp.dot(act, wdb_ref[...], preferred_element_type=F32)
        o_ref[...] = _pack_bf16_halves(y)

    has_ahead = i + ROW_SLOTS - 1 < n_used

    @pl.when(has_ahead)
    def _():
        compute(True)

    @pl.when(jnp.logical_and(used, jnp.logical_not(has_ahead)))
    def _():
        compute(False)

    @pl.when(jnp.logical_not(used))
    def _():
        o_ref[...] = jnp.zeros_like(o_ref)


def moe_experts(blk_e, n_used, next_e, next2_e, grp, dest, n_rows, hp, w_gate, w_up,
                w_down, layer):
    half = hp.shape[1]
    d = 2 * half
    ff = w_gate.shape[3]
    nblk = n_rows // MOE_BLOCK
    hbm = pl.BlockSpec(memory_space=pl.ANY)
    return pl.pallas_call(
        functools.partial(_experts_body, layer=layer),
        grid_spec=pltpu.PrefetchScalarGridSpec(
            num_scalar_prefetch=6,
            grid=(nblk,),
            in_specs=[hbm, hbm, hbm, hbm],
            out_specs=pl.BlockSpec((MOE_BLOCK, half), lambda i, *_: (i, 0)),
            scratch_shapes=[pltpu.VMEM((ROW_SLOTS, MOE_BLOCK, half), U32),
                            pltpu.SemaphoreType.DMA((ROW_SLOTS,)),
                            pltpu.VMEM((WEIGHT_SLOTS, d, ff), F32),
                            pltpu.VMEM((WEIGHT_SLOTS, d, ff), F32),
                            pltpu.VMEM((WEIGHT_SLOTS, ff, d), F32),
                            pltpu.SemaphoreType.DMA((3, WEIGHT_SLOTS)),
                            pltpu.VMEM((d, ff), BF16), pltpu.VMEM((d, ff), BF16),
                            pltpu.VMEM((ff, d), BF16),
                            pltpu.SMEM((n_rows,), I32)],
        ),
        out_shape=jax.ShapeDtypeStruct((n_rows, half), U32),
        compiler_params=_params(("arbitrary",), 60),
        name="moe_experts",
    )(blk_e, n_used, next_e, next2_e, grp, dest, hp, w_gate, w_up, w_down)


def _combine_body(dest_ref, x_ref, wt_ref, g_ref, ys_ref, *rest, final_norm):
    if final_norm:
        o_ref, buf_ref, sem = rest
        h_ref = None
    else:
        o_ref, h_ref, buf_ref, sem = rest
    i = pl.program_id(0)
    n = pl.num_programs(0)
    tm = x_ref.shape[0]
    t = tm * n

    def issue(tile, slot):
        def body(r, _):
            for k in range(2):
                d = dest_ref[k * t + tile * tm + r]
                pltpu.make_async_copy(ys_ref.at[pl.ds(d, 1)],
                                      buf_ref.at[slot, k, pl.ds(r, 1)],
                                      sem.at[slot]).start(priority=k)
            return 0
        lax.fori_loop(0, tm, body, 0, unroll=8)

    @pl.when(i == 0)
    def _():
        issue(0, 0)

    @pl.when(i + 1 < n)
    def _():
        issue(i + 1, (i + 1) % 2)

    slot = i % 2
    for k in range(2):
        pltpu.make_async_copy(ys_ref.at[pl.ds(0, tm)], buf_ref.at[slot, k],
                              sem.at[slot]).wait()
    w = wt_ref[...]
    lo1, hi1 = _unpack_bf16_halves(buf_ref[slot, 0])
    lo2, hi2 = _unpack_bf16_halves(buf_ref[slot, 1])
    half = lo1.shape[1]
    x = x_ref[...]
    y_lo = x[:, :half] + w[:, 0:1] * lo1 + w[:, 1:2] * lo2
    y_hi = x[:, half:] + w[:, 0:1] * hi1 + w[:, 1:2] * hi2
    ssq = (jnp.sum(y_lo * y_lo, axis=-1, keepdims=True)
           + jnp.sum(y_hi * y_hi, axis=-1, keepdims=True))
    inv = lax.rsqrt(ssq / (2 * half) + EPS)
    g = g_ref[...]
    n_lo = y_lo * inv * g[:, :half]
    n_hi = y_hi * inv * g[:, half:]
    if final_norm:
        o_ref[:, :half] = n_lo
        o_ref[:, half:] = n_hi
    else:
        o_ref[:, :half] = y_lo
        o_ref[:, half:] = y_hi
        h_ref[:, :half] = n_lo.astype(h_ref.dtype)
        h_ref[:, half:] = n_hi.astype(h_ref.dtype)


def moe_combine(dest_flat, x2d, wt_t, ys, g_norm, final_norm, tm=256):
    t, d = x2d.shape
    row_spec = pl.BlockSpec((tm, d), lambda i, ds: (i, 0))
    if final_norm:
        out_specs = row_spec
        out_shape = jax.ShapeDtypeStruct((t, d), F32)
    else:
        out_specs = [row_spec, row_spec]
        out_shape = [jax.ShapeDtypeStruct((t, d), F32),
                     jax.ShapeDtypeStruct((t, d), BF16)]
    return pl.pallas_call(
        functools.partial(_combine_body, final_norm=final_norm),
        grid_spec=pltpu.PrefetchScalarGridSpec(
            num_scalar_prefetch=1,
            grid=(t // tm,),
            in_specs=[row_spec,
                      pl.BlockSpec((tm, 2), lambda i, ds: (i, 0)),
                      pl.BlockSpec((1, d), lambda i, ds: (0, 0)),
                      pl.BlockSpec(memory_space=pl.ANY)],
            out_specs=out_specs,
            scratch_shapes=[pltpu.VMEM((2, 2, tm, d // 2), U32),
                            pltpu.SemaphoreType.DMA((2,))],
        ),
        out_shape=out_shape,
        compiler_params=_params(("arbitrary",), 32),
        name="moe_combine",
    )(dest_flat, x2d, wt_t, g_norm.reshape(1, d), ys)


def hier_moe(x2d, g_ffn, router_g, router_g_b, router_e, router_e_b,
             w_gate, w_up, w_down, layer, g_norm, final_norm):
    t, d = x2d.shape
    eid, wts, rank, cnt, hp = moe_router(x2d, g_ffn, router_g, router_g_b,
                                         router_e, router_e_b)
    counts = cnt[:, 0]
    padded = (counts + MOE_BLOCK - 1) // MOE_BLOCK * MOE_BLOCK
    pend = jnp.cumsum(padded)
    pstart = pend - padded
    n_rows = 2 * t + MOE_EXPERTS * MOE_BLOCK
    nblk = n_rows // MOE_BLOCK
    expert_ids = jnp.arange(MOE_EXPERTS, dtype=I32)
    start_of = jnp.sum(jnp.where(eid[..., None] == expert_ids, pstart, 0), axis=-1)
    dest = (start_of + rank).reshape(-1)
    blk_start = jnp.arange(nblk, dtype=I32) * MOE_BLOCK
    blk_e = jnp.sum(blk_start[:, None] >= pend[None, :], axis=1).astype(I32)
    blk_e = jnp.minimum(blk_e, MOE_EXPERTS - 1)
    n_used = (pend[-1:] // MOE_BLOCK).astype(I32)
    last_e = blk_e[jnp.maximum(n_used[0] - 1, 0)]
    blk_e = jnp.where(jnp.arange(nblk) < n_used[0], blk_e, last_e)
    changed = jnp.concatenate([jnp.ones((1,), I32),
                               (blk_e[1:] != blk_e[:-1]).astype(I32)])
    grp = jnp.cumsum(changed) - 1
    grp_end = jnp.sum(jnp.where(blk_e[:, None] == expert_ids, pend, 0), axis=-1)
    nxt_blk = grp_end // MOE_BLOCK
    nxt_e = jnp.sum(jnp.where(nxt_blk[:, None] == jnp.arange(nblk), blk_e, 0), axis=-1)
    next_e = jnp.where(nxt_blk < n_used[0], nxt_e, -1).astype(I32)
    nxt2_e = jnp.sum(jnp.where(nxt_blk[:, None] == jnp.arange(nblk), next_e, 0), axis=-1)
    next2_e = jnp.where(nxt_blk < n_used[0], nxt2_e, -1).astype(I32)
    ys = moe_experts(blk_e, n_used, next_e, next2_e, grp.astype(I32), dest, n_rows,
                     hp, w_gate, w_up, w_down, layer)
    return moe_combine(dest, x2d, wts.T, ys, g_norm, final_norm)


def _pad_cols(w, n):
    return jnp.pad(w, ((0, 0), (0, n - w.shape[1])))


def pool_fox_layer(x2d, h, b, l, j, w_in, b_forget, pool_w, pool_scale, w_out):
    main = POOL_WIDTH + 3 * FOX_WIDTH
    w_in_t = jnp.swapaxes(w_in, 1, 2)
    q_cols = (POOL_WIDTH, POOL_WIDTH + FOX_WIDTH, FOX_Q_SCALE)
    proj = matmul([h], w_in_t, j, main, w_transposed=True, scaled_cols=q_cols,
                  name="pf_in_proj")
    f = matmul([h], _pad_cols(w_in[j, :, main:], LANES)[None], 0, LANES,
               out_dtype=F32, name="pf_forget_proj")
    f_t = f[:, :FOX_HEADS].reshape(b, l, FOX_HEADS).transpose(0, 2, 1)
    c = forget_cumsum(f_t, b_forget)
    proj3 = proj.reshape(b, l, main)
    y_pool = pool_mixer(proj3, pool_w, pool_scale)
    y_att = fox_attention(proj3, c, c.transpose(0, 2, 1))
    acts = [y_pool.reshape(b * l, POOL_WIDTH), y_att.reshape(b * l, FOX_WIDTH)]
    return matmul(acts, w_out, j, w_out.shape[2], res=x2d, out_dtype=F32,
                  vmem_mib=56, name="pf_out_proj")


def ssd_layer(x2d, h, b, l, j, w_in, conv_w, conv_b, dt_bias, a_log, d_skip,
              gnorm, w_out):
    heads = dt_bias.shape[0]
    inner = heads * SSD_HEAD_DIM
    main = 2 * inner + 2 * SSD_GROUPS * SSD_STATE
    w_in_t = jnp.swapaxes(w_in, 1, 2)
    proj = matmul([h], w_in_t, j, main, w_transposed=True, name="ssd_in_proj")
    raw = matmul([h], _pad_cols(w_in[j, :, main:], LANES)[None], 0, LANES,
                 out_dtype=F32, name="ssd_dt_proj")
    dt_t, acs_t = ssd_dt(raw, jnp.pad(dt_bias, (0, LANES - heads)),
                         jnp.pad(a_log, (0, LANES - heads)))
    hpg = SSD_GROUP_HEADS
    dt_gt = dt_t[:heads].reshape(SSD_GROUPS, hpg, b * l)
    acs_gt = acs_t[:heads].reshape(SSD_GROUPS, hpg, b * l)
    d_skip_exp = jnp.repeat(d_skip, SSD_HEAD_DIM).reshape(1, inner)
    y = ssd_core(proj.reshape(b, l, main), conv_w, conv_b, dt_gt, acs_gt,
                 d_skip_exp, gnorm)
    return matmul([y.reshape(b * l, inner)], w_out, j, w_out.shape[2], res=x2d,
                  out_dtype=F32, tn=512, vmem_mib=56, name="ssd_out_proj")


def kernel(x, norm_mix, norm_ffn, norm_final, pf_w_in, pf_b_forget, pf_pool_w, pf_pool_scale, pf_w_out, ssd_w_in, ssd_conv_w, ssd_conv_b, ssd_dt_bias, ssd_a_log, ssd_d_skip, ssd_gnorm, ssd_w_out, moe_router_g, moe_router_g_b, moe_router_e, moe_router_e_b, moe_w_gate, moe_w_up, moe_w_down):
    b, l, d = x.shape
    depth = norm_mix.shape[0]
    x2d = x.reshape(b * l, d)
    h = rmsnorm(x2d, norm_mix[0], BF16)
    for i in range(depth):
        j = i // 2
        if i % 2 == 0:
            x2d = pool_fox_layer(x2d, h, b, l, j, pf_w_in, pf_b_forget[j],
                                 pf_pool_w[j], pf_pool_scale[j], pf_w_out)
        else:
            x2d = ssd_layer(x2d, h, b, l, j, ssd_w_in, ssd_conv_w[j],
                            ssd_conv_b[j], ssd_dt_bias[j], ssd_a_log[j],
                            ssd_d_skip[j], ssd_gnorm[j], ssd_w_out)
        last = i == depth - 1
        g_norm = norm_final if last else norm_mix[i + 1]
        out = hier_moe(x2d, norm_ffn[i], moe_router_g[i], moe_router_g_b[i],
                       moe_router_e[i], moe_router_e_b[i], moe_w_gate,
                       moe_w_up, moe_w_down, i, g_norm, last)
        if last:
            x2d = out
        else:
            x2d, h = out
    return x2d.reshape(b, l, d)
```

```python
import functools

import jax
import jax.numpy as jnp
from jax import lax
from jax.experimental import pallas as pl
from jax.experimental.pallas import tpu as pltpu

F32 = jnp.float32
BF16 = jnp.bfloat16
I32 = jnp.int32
U32 = jnp.uint32
EPS = 1e-6

POOL_WINDOWS = (2, 4, 8, 16)
POOL_GROUP = 256
POOL_WIDTH = 1024
FOX_HEADS = 8
FOX_HEAD_DIM = 128
FOX_WIDTH = 1024
SSD_HEAD_DIM = 64
SSD_STATE = 128
SSD_GROUPS = 8
SSD_GROUP_HEADS = 8
SSD_GROUP_WIDTH = SSD_GROUP_HEADS * SSD_HEAD_DIM
SSD_CONV = 4
SSD_CHUNK = 128
MOE_GROUPS = 4
MOE_PER_GROUP = 8
MOE_EXPERTS = 32
MOE_BLOCK = 256
WEIGHT_DMA_PRIORITY = 1
ROW_SLOTS = 3
WEIGHT_SLOTS = 3
LANES = 128
SUBLANES = 8
MIB = 1 << 20


def _params(semantics, vmem_mib):
    return pltpu.CompilerParams(dimension_semantics=semantics,
                                vmem_limit_bytes=vmem_mib * MIB)


def _rmsnorm_body(x_ref, g_ref, o_ref):
    x = x_ref[...]
    inv = lax.rsqrt(jnp.mean(x * x, axis=-1, keepdims=True) + EPS)
    o_ref[...] = (x * inv * g_ref[...]).astype(o_ref.dtype)


def rmsnorm(x2d, g, out_dtype, tm=512):
    t, d = x2d.shape
    return pl.pallas_call(
        _rmsnorm_body,
        grid=(t // tm,),
        in_specs=[pl.BlockSpec((tm, d), lambda i: (i, 0)),
                  pl.BlockSpec((1, d), lambda i: (0, 0))],
        out_specs=pl.BlockSpec((tm, d), lambda i: (i, 0)),
        out_shape=jax.ShapeDtypeStruct((t, d), out_dtype),
        compiler_params=_params(("arbitrary",), 40),
        name="rmsnorm",
    )(x2d, g.reshape(1, d))


def _matmul_body(*refs, n_act, has_res, w_transposed, scaled_cols):
    refs = list(refs)
    a_refs = refs[:n_act]
    w_ref = refs[n_act]
    o_ref, wb_ref = refs[-2:]
    r_ref = refs[-3] if has_res else None
    out_axis = 0 if w_transposed else 1

    @pl.when(pl.program_id(1) == 0)
    def _():
        w = w_ref[0]
        if scaled_cols is not None:
            lo, hi, value = scaled_cols
            col = (lax.broadcasted_iota(I32, w.shape, out_axis)
                   + pl.program_id(0) * w.shape[out_axis])
            w = jnp.where(jnp.logical_and(col >= lo, col < hi), w * value, w)
        wb_ref[...] = w.astype(BF16)

    acc = None
    k0 = 0
    for a_ref in a_refs:
        kw = a_ref.shape[1]
        if w_transposed:
            part = lax.dot_general(a_ref[...], wb_ref[:, k0:k0 + kw],
                                   (((1,), (1,)), ((), ())),
                                   preferred_element_type=F32)
        else:
            part = jnp.dot(a_ref[...], wb_ref[k0:k0 + kw, :],
                           preferred_element_type=F32)
        acc = part if acc is None else acc + part
        k0 += kw
    if has_res:
        acc = acc + r_ref[...]
    o_ref[...] = acc.astype(o_ref.dtype)


def matmul(acts, w, layer, n_out, *, w_transposed=False, res=None,
           scaled_cols=None, out_dtype=BF16, tm=1024, tn=1024, vmem_mib=48,
           name="matmul"):
    t = acts[0].shape[0]
    k = sum(a.shape[1] for a in acts)
    assert w.shape[2 if w_transposed else 1] == k
    tn = min(tn, n_out)
    tm = min(tm, t)
    assert t % tm == 0 and n_out % tn == 0
    in_specs = [pl.BlockSpec((tm, a.shape[1]), lambda j, i: (i, 0)) for a in acts]
    if w_transposed:
        in_specs.append(pl.BlockSpec((1, tn, k), lambda j, i: (layer, j, 0)))
        wb_shape = (tn, k)
    else:
        in_specs.append(pl.BlockSpec((1, k, tn), lambda j, i: (layer, 0, j)))
        wb_shape = (k, tn)
    args = list(acts) + [w]
    if res is not None:
        in_specs.append(pl.BlockSpec((tm, tn), lambda j, i: (i, j)))
        args.append(res)
    return pl.pallas_call(
        functools.partial(_matmul_body, n_act=len(acts), has_res=res is not None,
                          w_transposed=w_transposed, scaled_cols=scaled_cols),
        grid=(n_out // tn, t // tm),
        in_specs=in_specs,
        out_specs=pl.BlockSpec((tm, tn), lambda j, i: (i, j)),
        out_shape=jax.ShapeDtypeStruct((t, n_out), out_dtype),
        scratch_shapes=[pltpu.VMEM(wb_shape, BF16)],
        compiler_params=_params(("arbitrary", "arbitrary"), vmem_mib),
        name=name,
    )(*args)


def _forget_cumsum_body(f_ref, b_ref, c_ref):
    z = f_ref[0] + b_ref[...]
    x = jnp.minimum(z, 0.0) - jnp.log1p(jnp.exp(-jnp.abs(z)))
    n = x.shape[1]
    lane = lax.broadcasted_iota(I32, x.shape, 1)
    shift = 1
    while shift < n:
        x = x + jnp.where(lane >= shift, pltpu.roll(x, shift, axis=1), 0.0)
        shift *= 2
    c_ref[0] = x


def forget_cumsum(f_t, b_forget):
    b, h, l = f_t.shape
    return pl.pallas_call(
        _forget_cumsum_body,
        grid=(b,),
        in_specs=[pl.BlockSpec((1, h, l), lambda i: (i, 0, 0)),
                  pl.BlockSpec((h, 1), lambda i: (0, 0))],
        out_specs=pl.BlockSpec((1, h, l), lambda i: (i, 0, 0)),
        out_shape=jax.ShapeDtypeStruct((b, h, l), F32),
        compiler_params=_params(("arbitrary",), 16),
        name="forget_cumsum",
    )(f_t, b_forget.reshape(h, 1))


def _pool_body(u_ref, w_ref, s_ref, o_ref):
    g = pl.program_id(1)
    u = u_ref[0].astype(F32)
    row = lax.broadcasted_iota(I32, u.shape, 0)
    acc = u
    sums = []
    for shift in (1, 2, 4, 8):
        acc = acc + jnp.where(row >= shift, pltpu.roll(acc, shift, axis=0), 0.0)
        sums.append(acc)
    win_sum = jnp.where(g == 0, sums[0],
                        jnp.where(g == 1, sums[1],
                                  jnp.where(g == 2, sums[2], sums[3])))
    window = jnp.left_shift(jnp.int32(2), g)
    count = jnp.minimum(row + 1, window).astype(F32)
    mixed = win_sum / count - u
    y = jnp.dot(mixed.astype(BF16), w_ref[0].astype(BF16),
                preferred_element_type=F32)
    o_ref[0] = (y * s_ref[...]).astype(o_ref.dtype)


def pool_mixer(proj3, pool_w, pool_scale):
    b, l, _ = proj3.shape
    ng = len(POOL_WINDOWS)
    return pl.pallas_call(
        _pool_body,
        grid=(b, ng),
        in_specs=[pl.BlockSpec((1, l, POOL_GROUP), lambda i, g: (i, 0, g)),
                  pl.BlockSpec((1, POOL_GROUP, POOL_GROUP), lambda i, g: (g, 0, 0)),
                  pl.BlockSpec((1, POOL_GROUP), lambda i, g: (0, g))],
        out_specs=pl.BlockSpec((1, l, POOL_GROUP), lambda i, g: (i, 0, g)),
        out_shape=jax.ShapeDtypeStruct((b, l, POOL_WIDTH), BF16),
        compiler_params=_params(("arbitrary", "arbitrary"), 40),
        name="pool_mixer",
    )(proj3, pool_w, pool_scale.reshape(1, POOL_WIDTH))


LOG2E = 1.4426950408889634
FOX_Q_SCALE = FOX_HEAD_DIM ** -0.5 * LOG2E


FOX_HEADS_PER_STEP = 2


def _weighted_values(p, v):
    d = v.shape[1]
    v_aug = jnp.concatenate([v, jnp.ones_like(v)], axis=1)
    out = jnp.dot(p.astype(BF16), v_aug, preferred_element_type=F32)
    return out[:, :d], out[:, d:d + 1]


def _fox_body(q_ref, k_ref, v_ref, cq_ref, ck_ref, o_ref, m_ref, l_ref, acc_ref,
              *, tile):
    hp = pl.program_id(1)
    qi = pl.program_id(2)
    hd = FOX_HEAD_DIM
    cq8 = cq_ref[0]
    head_lane = lax.broadcasted_iota(I32, cq8.shape, 1)
    heads = []
    for hh in range(FOX_HEADS_PER_STEP):
        head = hp * FOX_HEADS_PER_STEP + hh
        cq = jnp.sum(jnp.where(head_lane == head, cq8, 0.0), axis=1,
                     keepdims=True) * LOG2E
        heads.append((hh, head, slice(hh * hd, (hh + 1) * hd), cq))

    def logits(j, head, cols):
        start = pl.multiple_of(j * tile, tile)
        k = k_ref[0, pl.ds(start, tile), cols]
        ck = ck_ref[0, pl.ds(head, 1), pl.ds(start, tile)] * LOG2E
        s = lax.dot_general(q_ref[0, :, cols], k, (((1,), (1,)), ((), ())),
                            preferred_element_type=F32)
        return s - ck, start

    for hh, head, cols, cq in heads:
        s, start = logits(qi, head, cols)
        r = lax.broadcasted_iota(I32, s.shape, 0)
        c = lax.broadcasted_iota(I32, s.shape, 1)
        s = jnp.where(c <= r, s, -jnp.inf)
        m0 = jnp.max(s, axis=1, keepdims=True) + cq
        p = jnp.exp2(s + (cq - m0))
        m_ref[hh] = m0
        pv, row_sum = _weighted_values(p, v_ref[0, pl.ds(start, tile), cols])
        l_ref[hh] = row_sum
        acc_ref[hh] = pv

    def step(j, _):
        for hh, head, cols, cq in heads:
            s, start = logits(j, head, cols)
            m_old = m_ref[hh]
            m_new = jnp.maximum(m_old, jnp.max(s, axis=1, keepdims=True) + cq)
            alpha = jnp.exp2(m_old - m_new)
            p = jnp.exp2(s + (cq - m_new))
            pv, row_sum = _weighted_values(p, v_ref[0, pl.ds(start, tile), cols])
            l_ref[hh] = alpha * l_ref[hh] + row_sum
            acc_ref[hh] = alpha * acc_ref[hh] + pv
            m_ref[hh] = m_new
        return 0

    lax.fori_loop(0, qi, step, 0)
    for hh, head, cols, cq in heads:
        o_ref[0, :, cols] = (acc_ref[hh] / l_ref[hh]).astype(o_ref.dtype)


def fox_attention(proj3, c_row, c_col, tile=512):
    b, l, _ = proj3.shape
    hps = FOX_HEADS_PER_STEP
    w = hps * FOX_HEAD_DIM
    q0 = POOL_WIDTH // w
    k0 = q0 + FOX_HEADS // hps
    v0 = k0 + FOX_HEADS // hps
    return pl.pallas_call(
        functools.partial(_fox_body, tile=tile),
        grid=(b, FOX_HEADS // hps, l // tile),
        in_specs=[
            pl.BlockSpec((1, tile, w), lambda i, h, q: (i, q, q0 + h)),
            pl.BlockSpec((1, l, w), lambda i, h, q: (i, 0, k0 + h)),
            pl.BlockSpec((1, l, w), lambda i, h, q: (i, 0, v0 + h)),
            pl.BlockSpec((1, tile, FOX_HEADS), lambda i, h, q: (i, q, 0)),
            pl.BlockSpec((1, FOX_HEADS, l), lambda i, h, q: (i, 0, 0)),
        ],
        out_specs=pl.BlockSpec((1, tile, w), lambda i, h, q: (i, q, h)),
        out_shape=jax.ShapeDtypeStruct((b, l, FOX_WIDTH), BF16),
        scratch_shapes=[pltpu.VMEM((hps, tile, 1), F32),
                        pltpu.VMEM((hps, tile, 1), F32),
                        pltpu.VMEM((hps, tile, FOX_HEAD_DIM), F32)],
        compiler_params=_params(("arbitrary", "arbitrary", "arbitrary"), 32),
        name="fox_attention",
    )(proj3, proj3, proj3, c_col, c_row)


def _ssd_dt_body(raw_ref, bias_ref, alog_ref, dt_ref, acs_ref):
    z = raw_ref[...] + bias_ref[...]
    dt = jnp.maximum(z, 0.0) + jnp.log1p(jnp.exp(-jnp.abs(z)))
    a_dt = dt * (-jnp.exp(alog_ref[...]))
    n = z.shape[0]
    r = lax.broadcasted_iota(I32, (n, n), 0)
    c = lax.broadcasted_iota(I32, (n, n), 1)
    tri = (c <= r).astype(F32)
    acs = jnp.dot(tri, a_dt, preferred_element_type=F32,
                  precision=lax.Precision.HIGHEST)
    dt_ref[...] = dt.T
    acs_ref[...] = acs.T


def ssd_dt(raw, bias_pad, alog_pad):
    t, n = raw.shape
    assert n == SSD_CHUNK
    spec = pl.BlockSpec((SSD_CHUNK, n), lambda i: (i, 0))
    spec_t = pl.BlockSpec((n, SSD_CHUNK), lambda i: (0, i))
    vec = pl.BlockSpec((1, n), lambda i: (0, 0))
    return pl.pallas_call(
        _ssd_dt_body,
        grid=(t // SSD_CHUNK,),
        in_specs=[spec, vec, vec],
        out_specs=[spec_t, spec_t],
        out_shape=[jax.ShapeDtypeStruct((n, t), F32)] * 2,
        compiler_params=_params(("arbitrary",), 16),
        name="ssd_dt",
    )(raw, bias_pad.reshape(1, n), alog_pad.reshape(1, n))


def _silu(x):
    return x / (1.0 + jnp.exp(-x))


def _causal_conv_silu(u_ref, w_ref, b_ref):
    u = u_ref[0].astype(F32)
    w = w_ref[...]

    def conv(v, causal_rows):
        out = b_ref[...] + v * w[SSD_CONV - 1:SSD_CONV, :]
        for shift in range(1, SSD_CONV):
            prev = pltpu.roll(v, shift, axis=0)
            if causal_rows is not None:
                prev = jnp.where(causal_rows >= shift, prev, 0.0)
            out = out + prev * w[SSD_CONV - 1 - shift:SSD_CONV - shift, :]
        return out

    head = u[:SUBLANES]
    head_rows = lax.broadcasted_iota(I32, head.shape, 0)
    out = jnp.concatenate([conv(head, head_rows), conv(u, None)[SUBLANES:]], axis=0)
    return _silu(out)


def _pair_lanes(cols, j):
    rows = cols.shape[0]
    lane = lax.broadcasted_iota(I32, (rows, LANES), 1)
    lo = jnp.broadcast_to(cols[:, 2 * j:2 * j + 1], (rows, LANES))
    hi = jnp.broadcast_to(cols[:, 2 * j + 1:2 * j + 2], (rows, LANES))
    return jnp.where(lane < SSD_HEAD_DIM, lo, hi)


def _ssd_body(z_ref, x_ref, b_ref, c_ref, wx_ref, wb_ref, wc_ref,
              bx_ref, bb_ref, bc_ref, dtt_ref, acst_ref,
              dskip_ref, gn_ref, o_ref, xs_ref, bs_ref, cs_ref, st_ref):
    q = SSD_CHUNK
    xs_ref[...] = _causal_conv_silu(x_ref, wx_ref, bx_ref)
    bs_ref[...] = _causal_conv_silu(b_ref, wb_ref, bb_ref).astype(BF16)
    cs_ref[...] = _causal_conv_silu(c_ref, wc_ref, bc_ref).astype(BF16)
    st_ref[...] = jnp.zeros_like(st_ref)

    r = lax.broadcasted_iota(I32, (q, q), 0)
    c = lax.broadcasted_iota(I32, (q, q), 1)
    causal = c <= r
    lane = lax.broadcasted_iota(I32, (q, LANES), 1)
    first_head = lane < SSD_HEAD_DIM

    def chunk(ci, _):
        r0 = pl.multiple_of(ci * q, q)
        xc = xs_ref[pl.ds(r0, q), :]
        bm = bs_ref[pl.ds(r0, q), :]
        cm = cs_ref[pl.ds(r0, q), :]
        zc = z_ref[0, pl.ds(r0, q), :].astype(F32)
        dt_t = dtt_ref[0, :, pl.ds(r0, q)]
        acs_t = acst_ref[0, :, pl.ds(r0, q)]
        acs = acs_t.T
        exp_acs = jnp.exp(acs)
        chunk_decay = jnp.exp(acs[q - 1:q, :])
        w_diag_t = dt_t
        w_end_t = dt_t * jnp.exp(acs_t[:, q - 1:q] - acs_t)
        cb = lax.dot_general(cm, bm, (((1,), (1,)), ((), ())),
                             preferred_element_type=F32)
        bm_t = bm.astype(F32).T
        gated = []
        for j in range(SSD_GROUP_HEADS // 2):
            lanes = slice(j * LANES, (j + 1) * LANES)
            xp = xc[:, lanes]
            xp_b = xp.astype(BF16)
            y_heads = []
            s_heads = []
            for hh in range(2):
                hd = 2 * j + hh
                seg = jnp.broadcast_to(acs[:, hd:hd + 1], (q, q)) - acs_t[hd:hd + 1, :]
                decay = jnp.exp(jnp.where(causal, seg, -jnp.inf))
                mix = cb * decay * w_diag_t[hd:hd + 1, :]
                y_heads.append(jnp.dot(mix.astype(BF16), xp_b,
                                       preferred_element_type=F32))
                b_end = (bm_t * w_end_t[hd:hd + 1, :]).astype(BF16)
                s_heads.append(jnp.dot(b_end, xp_b, preferred_element_type=F32))
            y = jnp.where(first_head, y_heads[0], y_heads[1])
            state = st_ref[:, lanes]
            y = y + jnp.dot(cm, state.astype(BF16),
                            preferred_element_type=F32) * _pair_lanes(exp_acs, j)
            y = y + xp * dskip_ref[:, lanes]
            st_ref[:, lanes] = state * _pair_lanes(chunk_decay, j) + jnp.where(
                first_head, s_heads[0], s_heads[1])
            gated.append(y * _silu(zc[:, lanes]))
        gated = jnp.concatenate(gated, axis=1)
        inv = lax.rsqrt(jnp.mean(gated * gated, axis=-1, keepdims=True) + EPS)
        o_ref[0, pl.ds(r0, q), :] = (gated * inv * gn_ref[...]).astype(o_ref.dtype)
        return 0

    lax.fori_loop(0, x_ref.shape[1] // q, chunk, 0)


def ssd_core(proj3, conv_w, conv_b, dt_gt, acs_gt, d_skip_exp, gnorm):
    b, l, _ = proj3.shape
    gw = SSD_GROUP_WIDTH
    n = SSD_STATE
    inner = SSD_GROUPS * gw
    x_blk0 = inner // gw
    bm_blk0 = 2 * inner // n
    cm_blk0 = bm_blk0 + SSD_GROUPS
    wb_blk0 = inner // n
    wc_blk0 = wb_blk0 + SSD_GROUPS
    hpg = SSD_GROUP_HEADS
    conv_b2 = conv_b.reshape(1, -1)
    return pl.pallas_call(
        _ssd_body,
        grid=(b, SSD_GROUPS),
        in_specs=[
            pl.BlockSpec((1, l, gw), lambda i, g: (i, 0, g)),
            pl.BlockSpec((1, l, gw), lambda i, g: (i, 0, x_blk0 + g)),
            pl.BlockSpec((1, l, n), lambda i, g: (i, 0, bm_blk0 + g)),
            pl.BlockSpec((1, l, n), lambda i, g: (i, 0, cm_blk0 + g)),
            pl.BlockSpec((SSD_CONV, gw), lambda i, g: (0, g)),
            pl.BlockSpec((SSD_CONV, n), lambda i, g: (0, wb_blk0 + g)),
            pl.BlockSpec((SSD_CONV, n), lambda i, g: (0, wc_blk0 + g)),
            pl.BlockSpec((1, gw), lambda i, g: (0, g)),
            pl.BlockSpec((1, n), lambda i, g: (0, wb_blk0 + g)),
            pl.BlockSpec((1, n), lambda i, g: (0, wc_blk0 + g)),
            pl.BlockSpec((1, hpg, l), lambda i, g: (g, 0, i)),
            pl.BlockSpec((1, hpg, l), lambda i, g: (g, 0, i)),
            pl.BlockSpec((1, gw), lambda i, g: (0, g)),
            pl.BlockSpec((1, gw), lambda i, g: (0, g)),
        ],
        out_specs=pl.BlockSpec((1, l, gw), lambda i, g: (i, 0, g)),
        out_shape=jax.ShapeDtypeStruct((b, l, inner), BF16),
        scratch_shapes=[pltpu.VMEM((l, gw), F32), pltpu.VMEM((l, n), BF16),
                        pltpu.VMEM((l, n), BF16), pltpu.VMEM((n, gw), F32)],
        compiler_params=_params(("arbitrary", "arbitrary"), 48),
        name="ssd_core",
    )(proj3, proj3, proj3, proj3, conv_w, conv_w, conv_w, conv_b2, conv_b2,
      conv_b2, dt_gt, acs_gt, d_skip_exp, gnorm.reshape(1, inner))


def _first_argmax(vals, nrows):
    row = lax.broadcasted_iota(I32, vals.shape, 0)
    top = jnp.max(vals, axis=0, keepdims=True)
    idx = jnp.min(jnp.where(vals == top, row, nrows), axis=0, keepdims=True)
    return top, idx, row


def _pack_bf16_halves(y):
    half = y.shape[1] // 2

    def bits(v):
        return lax.bitcast_convert_type(v.astype(BF16).astype(F32), U32)

    return bits(y[:, half:]) | (bits(y[:, :half]) >> 16)


def _unpack_bf16_halves(w):
    lo = lax.bitcast_convert_type(w << 16, F32)
    hi = lax.bitcast_convert_type(w & jnp.uint32(0xFFFF0000), F32)
    return lo, hi


def _router_body(x_ref, g_ref, rt_ref, rb_ref, eid_ref, wt_ref, rank_ref,
                 cnt_ref, hp_ref, carry_ref):
    step = pl.program_id(0)

    @pl.when(step == 0)
    def _():
        carry_ref[...] = jnp.zeros_like(carry_ref)

    x = x_ref[...]
    tm = x.shape[0]
    h = x * lax.rsqrt(jnp.mean(x * x, axis=-1, keepdims=True) + EPS) * g_ref[...]
    hp_ref[...] = _pack_bf16_halves(h)
    def split(v):
        hi = v.astype(BF16)
        return hi, (v - hi.astype(F32)).astype(BF16)

    def dot_nt(a, b):
        return lax.dot_general(a, b, (((1,), (1,)), ((), ())),
                               preferred_element_type=F32)

    h_hi, h_lo = split(h)
    rt_hi, rt_lo = split(rt_ref[...])
    logits = (dot_nt(rt_hi, h_hi) + dot_nt(rt_hi, h_lo) + dot_nt(rt_lo, h_hi)
              + rb_ref[...])
    e_logits = logits[:MOE_EXPERTS]
    g_logits = logits[MOE_EXPERTS:MOE_EXPERTS + MOE_GROUPS]
    g_max, g_sel, _ = _first_argmax(g_logits, MOE_GROUPS)
    g_w = 1.0 / jnp.sum(jnp.exp(g_logits - g_max), axis=0, keepdims=True)
    sel = jnp.zeros((MOE_PER_GROUP, tm), F32)
    for grp in range(MOE_GROUPS):
        sel = jnp.where(g_sel == grp,
                        e_logits[grp * MOE_PER_GROUP:(grp + 1) * MOE_PER_GROUP], sel)
    m1, i1, row8 = _first_argmax(sel, MOE_PER_GROUP)
    rest = jnp.where(row8 == i1, -jnp.inf, sel)
    m2, i2, _ = _first_argmax(rest, MOE_PER_GROUP)
    p2 = jnp.exp(m2 - m1)
    w1 = g_w / (1.0 + p2)
    w2 = g_w * p2 / (1.0 + p2)
    e1 = g_sel * MOE_PER_GROUP + i1
    e2 = g_sel * MOE_PER_GROUP + i2

    r = lax.broadcasted_iota(I32, (tm, tm), 0)
    c = lax.broadcasted_iota(I32, (tm, tm), 1)
    upper = jnp.where(r <= c, 1.0, 0.0).astype(BF16)
    row32 = lax.broadcasted_iota(I32, (MOE_EXPERTS, tm), 0)
    hit1 = row32 == e1
    hit2 = row32 == e2
    cum1 = jnp.dot(jnp.where(hit1, 1.0, 0.0).astype(BF16), upper,
                   preferred_element_type=F32)
    cum2 = jnp.dot(jnp.where(hit2, 1.0, 0.0).astype(BF16), upper,
                   preferred_element_type=F32)
    carry = carry_ref[...]
    tot1 = cum1[:, tm - 1:tm]
    tot2 = cum2[:, tm - 1:tm]
    rank1 = jnp.sum(jnp.where(hit1, carry + cum1 - 1.0, 0.0), axis=0, keepdims=True)
    rank2 = jnp.sum(jnp.where(hit2, carry + tot1 + cum2 - 1.0, 0.0), axis=0,
                    keepdims=True)
    new_carry = carry + tot1 + tot2
    carry_ref[...] = new_carry

    eid_ref[...] = jnp.concatenate([e1, e2], axis=0)
    wt_ref[...] = jnp.concatenate([w1, w2], axis=0)
    rank_ref[...] = jnp.concatenate([rank1, rank2], axis=0).astype(I32)
    cnt_ref[...] = jnp.broadcast_to(new_carry, cnt_ref.shape).astype(I32)


def moe_router(x2d, g, router_g, router_g_b, router_e, router_e_b, tm=512):
    t, d = x2d.shape
    pad = 8 - MOE_GROUPS
    rt = jnp.concatenate([router_e.T, router_g.T, jnp.zeros((pad, d), F32)], axis=0)
    rb = jnp.concatenate([router_e_b, router_g_b, jnp.zeros((pad,), F32)]).reshape(-1, 1)
    nr = rt.shape[0]
    tok = pl.BlockSpec((2, tm), lambda i: (0, i))
    return pl.pallas_call(
        _router_body,
        grid=(t // tm,),
        in_specs=[pl.BlockSpec((tm, d), lambda i: (i, 0)),
                  pl.BlockSpec((1, d), lambda i: (0, 0)),
                  pl.BlockSpec((nr, d), lambda i: (0, 0)),
                  pl.BlockSpec((nr, 1), lambda i: (0, 0))],
        out_specs=[tok, tok, tok,
                   pl.BlockSpec((MOE_EXPERTS, LANES), lambda i: (0, 0)),
                   pl.BlockSpec((tm, d // 2), lambda i: (i, 0))],
        out_shape=[jax.ShapeDtypeStruct((2, t), I32),
                   jax.ShapeDtypeStruct((2, t), F32),
                   jax.ShapeDtypeStruct((2, t), I32),
                   jax.ShapeDtypeStruct((MOE_EXPERTS, LANES), I32),
                   jax.ShapeDtypeStruct((t, d // 2), U32)],
        scratch_shapes=[pltpu.VMEM((MOE_EXPERTS, 1), F32)],
        compiler_params=_params(("arbitrary",), 40),
        name="moe_router",
    )(x2d, g.reshape(1, d), rt, rb)


def _experts_body(blk_e_ref, n_used_ref, next_e_ref, next2_e_ref, grp_ref, dest_ref,
                  hp_hbm, wg_hbm, wu_hbm, wd_hbm, o_ref,
                  xbuf_ref, xsem, wg_st, wu_st, wd_st, wsem,
                  wgb_ref, wub_ref, wdb_ref, tok_ref, *, layer):
    i = pl.program_id(0)
    n_used = n_used_ref[0]
    used = i < n_used
    e_cur = blk_e_ref[i]
    prev = blk_e_ref[jnp.maximum(i - 1, 0)]
    fresh = jnp.logical_and(used, jnp.logical_or(i == 0, e_cur != prev))
    rows = xbuf_ref.shape[1]

    def weight_copies(e, slot):
        return (pltpu.make_async_copy(wg_hbm.at[layer, e], wg_st.at[slot],
                                      wsem.at[0, slot]),
                pltpu.make_async_copy(wu_hbm.at[layer, e], wu_st.at[slot],
                                      wsem.at[1, slot]),
                pltpu.make_async_copy(wd_hbm.at[layer, e], wd_st.at[slot],
                                      wsem.at[2, slot]))

    def row_copy(blk, slot, r):
        tok = tok_ref[blk * rows + r]
        return pltpu.make_async_copy(hp_hbm.at[pl.ds(tok, 1)],
                                     xbuf_ref.at[slot, pl.ds(r, 1)],
                                     xsem.at[slot])

    @pl.when(i == 0)
    def _():
        for cp in weight_copies(e_cur, 0):
            cp.start(priority=WEIGHT_DMA_PRIORITY)

        @pl.when(next_e_ref[0] >= 0)
        def _():
            for cp in weight_copies(next_e_ref[0], 1):
                cp.start(priority=WEIGHT_DMA_PRIORITY)

        n_tok = dest_ref.shape[0] // 2

        def clear(r, _):
            tok_ref[r] = 0
            return 0
        lax.fori_loop(0, tok_ref.shape[0], clear, 0, unroll=8)

        def fill(tok, _):
            tok_ref[dest_ref[tok]] = tok
            tok_ref[dest_ref[n_tok + tok]] = tok
            return 0
        lax.fori_loop(0, n_tok, fill, 0, unroll=8)

        def first(r, _):
            row_copy(0, 0, r).start()
            return 0
        lax.fori_loop(0, rows, first, 0, unroll=8)

        @pl.when(n_used > 1)
        def _():
            def second(r, _):
                row_copy(1, 1, r).start()
                return 0
            lax.fori_loop(0, rows, second, 0, unroll=8)

    wslot = grp_ref[i] % WEIGHT_SLOTS
    ahead_e = next2_e_ref[i]

    @pl.when(jnp.logical_and(fresh, ahead_e >= 0))
    def _():
        for cp in weight_copies(ahead_e, (grp_ref[i] + 2) % WEIGHT_SLOTS):
            cp.start(priority=WEIGHT_DMA_PRIORITY)

    @pl.when(fresh)
    def _():
        for cp in weight_copies(e_cur, wslot):
            cp.wait()
        wgb_ref[...] = wg_st[wslot].astype(BF16)
        wub_ref[...] = wu_st[wslot].astype(BF16)
        wdb_ref[...] = wd_st[wslot].astype(BF16)

    def compute(gather_ahead):
        slot = i % ROW_SLOTS
        pltpu.make_async_copy(hp_hbm.at[pl.ds(0, rows)], xbuf_ref.at[slot],
                              xsem.at[slot]).wait()
        if gather_ahead:
            ahead = i + ROW_SLOTS - 1
            for r in range(rows):
                row_copy(ahead, ahead % ROW_SLOTS, r).start()
        lo, hi = _unpack_bf16_halves(xbuf_ref[slot])
        h = jnp.concatenate([lo.astype(BF16), hi.astype(BF16)], axis=1)
        gate = jnp.dot(h, wgb_ref[...], preferred_element_type=F32)
        up = jnp.dot(h, wub_ref[...], preferred_element_type=F32)
        act = (_silu(gate) * up).astype(BF16)
        y = jnp.dot(act, wdb_ref[...], preferred_element_type=F32)
        o_ref[...] = _pack_bf16_halves(y)

    has_ahead = i + ROW_SLOTS - 1 < n_used

    @pl.when(has_ahead)
    def _():
        compute(True)

    @pl.when(jnp.logical_and(used, jnp.logical_not(has_ahead)))
    def _():
        compute(False)

    @pl.when(jnp.logical_not(used))
    def _():
        o_ref[...] = jnp.zeros_like(o_ref)


def moe_experts(blk_e, n_used, next_e, next2_e, grp, dest, n_rows, hp, w_gate, w_up,
                w_down, layer):
    half = hp.shape[1]
    d = 2 * half
    ff = w_gate.shape[3]
    nblk = n_rows // MOE_BLOCK
    hbm = pl.BlockSpec(memory_space=pl.ANY)
    return pl.pallas_call(
        functools.partial(_experts_body, layer=layer),
        grid_spec=pltpu.PrefetchScalarGridSpec(
            num_scalar_prefetch=6,
            grid=(nblk,),
            in_specs=[hbm, hbm, hbm, hbm],
            out_specs=pl.BlockSpec((MOE_BLOCK, half), lambda i, *_: (i, 0)),
            scratch_shapes=[pltpu.VMEM((ROW_SLOTS, MOE_BLOCK, half), U32),
                            pltpu.SemaphoreType.DMA((ROW_SLOTS,)),
                            pltpu.VMEM((WEIGHT_SLOTS, d, ff), F32),
                            pltpu.VMEM((WEIGHT_SLOTS, d, ff), F32),
                            pltpu.VMEM((WEIGHT_SLOTS, ff, d), F32),
                            pltpu.SemaphoreType.DMA((3, WEIGHT_SLOTS)),
                            pltpu.VMEM((d, ff), BF16), pltpu.VMEM((d, ff), BF16),
                            pltpu.VMEM((ff, d), BF16),
                            pltpu.SMEM((n_rows,), I32)],
        ),
        out_shape=jax.ShapeDtypeStruct((n_rows, half), U32),
        compiler_params=_params(("arbitrary",), 60),
        name="moe_experts",
    )(blk_e, n_used, next_e, next2_e, grp, dest, hp, w_gate, w_up, w_down)


def _combine_body(dest_ref, x_ref, wt_ref, g_ref, ys_ref, *rest, final_norm):
    if final_norm:
        o_ref, buf_ref, sem = rest
        h_ref = None
    else:
        o_ref, h_ref, buf_ref, sem = rest
    i = pl.program_id(0)
    n = pl.num_programs(0)
    tm = x_ref.shape[0]
    t = tm * n

    def issue(tile, slot):
        def body(r, _):
            for k in range(2):
                d = dest_ref[k * t + tile * tm + r]
                pltpu.make_async_copy(ys_ref.at[pl.ds(d, 1)],
                                      buf_ref.at[slot, k, pl.ds(r, 1)],
                                      sem.at[slot]).start(priority=k)
            return 0
        lax.fori_loop(0, tm, body, 0, unroll=8)

    @pl.when(i == 0)
    def _():
        issue(0, 0)

    @pl.when(i + 1 < n)
    def _():
        issue(i + 1, (i + 1) % 2)

    slot = i % 2
    for k in range(2):
        pltpu.make_async_copy(ys_ref.at[pl.ds(0, tm)], buf_ref.at[slot, k],
                              sem.at[slot]).wait()
    w = wt_ref[...]
    lo1, hi1 = _unpack_bf16_halves(buf_ref[slot, 0])
    lo2, hi2 = _unpack_bf16_halves(buf_ref[slot, 1])
    half = lo1.shape[1]
    x = x_ref[...]
    y_lo = x[:, :half] + w[:, 0:1] * lo1 + w[:, 1:2] * lo2
    y_hi = x[:, half:] + w[:, 0:1] * hi1 + w[:, 1:2] * hi2
    ssq = (jnp.sum(y_lo * y_lo, axis=-1, keepdims=True)
           + jnp.sum(y_hi * y_hi, axis=-1, keepdims=True))
    inv = lax.rsqrt(ssq / (2 * half) + EPS)
    g = g_ref[...]
    n_lo = y_lo * inv * g[:, :half]
    n_hi = y_hi * inv * g[:, half:]
    if final_norm:
        o_ref[:, :half] = n_lo
        o_ref[:, half:] = n_hi
    else:
        o_ref[:, :half] = y_lo
        o_ref[:, half:] = y_hi
        h_ref[:, :half] = n_lo.astype(h_ref.dtype)
        h_ref[:, half:] = n_hi.astype(h_ref.dtype)


def moe_combine(dest_flat, x2d, wt_t, ys, g_norm, final_norm, tm=256):
    t, d = x2d.shape
    row_spec = pl.BlockSpec((tm, d), lambda i, ds: (i, 0))
    if final_norm:
        out_specs = row_spec
        out_shape = jax.ShapeDtypeStruct((t, d), F32)
    else:
        out_specs = [row_spec, row_spec]
        out_shape = [jax.ShapeDtypeStruct((t, d), F32),
                     jax.ShapeDtypeStruct((t, d), BF16)]
    return pl.pallas_call(
        functools.partial(_combine_body, final_norm=final_norm),
        grid_spec=pltpu.PrefetchScalarGridSpec(
            num_scalar_prefetch=1,
            grid=(t // tm,),
            in_specs=[row_spec,
                      pl.BlockSpec((tm, 2), lambda i, ds: (i, 0)),
                      pl.BlockSpec((1, d), lambda i, ds: (0, 0)),
                      pl.BlockSpec(memory_space=pl.ANY)],
            out_specs=out_specs,
            scratch_shapes=[pltpu.VMEM((2, 2, tm, d // 2), U32),
                            pltpu.SemaphoreType.DMA((2,))],
        ),
        out_shape=out_shape,
        compiler_params=_params(("arbitrary",), 32),
        name="moe_combine",
    )(dest_flat, x2d, wt_t, g_norm.reshape(1, d), ys)


def hier_moe(x2d, g_ffn, router_g, router_g_b, router_e, router_e_b,
             w_gate, w_up, w_down, layer, g_norm, final_norm):
    t, d = x2d.shape
    eid, wts, rank, cnt, hp = moe_router(x2d, g_ffn, router_g, router_g_b,
                                         router_e, router_e_b)
    counts = cnt[:, 0]
    padded = (counts + MOE_BLOCK - 1) // MOE_BLOCK * MOE_BLOCK
    pend = jnp.cumsum(padded)
    pstart = pend - padded
    n_rows = 2 * t + MOE_EXPERTS * MOE_BLOCK
    nblk = n_rows // MOE_BLOCK
    expert_ids = jnp.arange(MOE_EXPERTS, dtype=I32)
    start_of = jnp.sum(jnp.where(eid[..., None] == expert_ids, pstart, 0), axis=-1)
    dest = (start_of + rank).reshape(-1)
    blk_start = jnp.arange(nblk, dtype=I32) * MOE_BLOCK
    blk_e = jnp.sum(blk_start[:, None] >= pend[None, :], axis=1).astype(I32)
    blk_e = jnp.minimum(blk_e, MOE_EXPERTS - 1)
    n_used = (pend[-1:] // MOE_BLOCK).astype(I32)
    last_e = blk_e[jnp.maximum(n_used[0] - 1, 0)]
    blk_e = jnp.where(jnp.arange(nblk) < n_used[0], blk_e, last_e)
    changed = jnp.concatenate([jnp.ones((1,), I32),
                               (blk_e[1:] != blk_e[:-1]).astype(I32)])
    grp = jnp.cumsum(changed) - 1
    grp_end = jnp.sum(jnp.where(blk_e[:, None] == expert_ids, pend, 0), axis=-1)
    nxt_blk = grp_end // MOE_BLOCK
    nxt_e = jnp.sum(jnp.where(nxt_blk[:, None] == jnp.arange(nblk), blk_e, 0), axis=-1)
    next_e = jnp.where(nxt_blk < n_used[0], nxt_e, -1).astype(I32)
    nxt2_e = jnp.sum(jnp.where(nxt_blk[:, None] == jnp.arange(nblk), next_e, 0), axis=-1)
    next2_e = jnp.where(nxt_blk < n_used[0], nxt2_e, -1).astype(I32)
    ys = moe_experts(blk_e, n_used, next_e, next2_e, grp.astype(I32), dest, n_rows,
                     hp, w_gate, w_up, w_down, layer)
    return moe_combine(dest, x2d, wts.T, ys, g_norm, final_norm)


def _pad_cols(w, n):
    return jnp.pad(w, ((0, 0), (0, n - w.shape[1])))


def pool_fox_layer(x2d, h, b, l, j, w_in, b_forget, pool_w, pool_scale, w_out):
    main = POOL_WIDTH + 3 * FOX_WIDTH
    w_in_t = jnp.swapaxes(w_in, 1, 2)
    q_cols = (POOL_WIDTH, POOL_WIDTH + FOX_WIDTH, FOX_Q_SCALE)
    proj = matmul([h], w_in_t, j, main, w_transposed=True, scaled_cols=q_cols,
                  name="pf_in_proj")
    f = matmul([h], _pad_cols(w_in[j, :, main:], LANES)[None], 0, LANES,
               out_dtype=F32, name="pf_forget_proj")
    f_t = f[:, :FOX_HEADS].reshape(b, l, FOX_HEADS).transpose(0, 2, 1)
    c = forget_cumsum(f_t, b_forget)
    proj3 = proj.reshape(b, l, main)
    y_pool = pool_mixer(proj3, pool_w, pool_scale)
    y_att = fox_attention(proj3, c, c.transpose(0, 2, 1))
    acts = [y_pool.reshape(b * l, POOL_WIDTH), y_att.reshape(b * l, FOX_WIDTH)]
    return matmul(acts, w_out, j, w_out.shape[2], res=x2d, out_dtype=F32,
                  vmem_mib=56, name="pf_out_proj")


def ssd_layer(x2d, h, b, l, j, w_in, conv_w, conv_b, dt_bias, a_log, d_skip,
              gnorm, w_out):
    heads = dt_bias.shape[0]
    inner = heads * SSD_HEAD_DIM
    main = 2 * inner + 2 * SSD_GROUPS * SSD_STATE
    w_in_t = jnp.swapaxes(w_in, 1, 2)
    proj = matmul([h], w_in_t, j, main, w_transposed=True, name="ssd_in_proj")
    raw = matmul([h], _pad_cols(w_in[j, :, main:], LANES)[None], 0, LANES,
                 out_dtype=F32, name="ssd_dt_proj")
    dt_t, acs_t = ssd_dt(raw, jnp.pad(dt_bias, (0, LANES - heads)),
                         jnp.pad(a_log, (0, LANES - heads)))
    hpg = SSD_GROUP_HEADS
    dt_gt = dt_t[:heads].reshape(SSD_GROUPS, hpg, b * l)
    acs_gt = acs_t[:heads].reshape(SSD_GROUPS, hpg, b * l)
    d_skip_exp = jnp.repeat(d_skip, SSD_HEAD_DIM).reshape(1, inner)
    y = ssd_core(proj.reshape(b, l, main), conv_w, conv_b, dt_gt, acs_gt,
                 d_skip_exp, gnorm)
    return matmul([y.reshape(b * l, inner)], w_out, j, w_out.shape[2], res=x2d,
                  out_dtype=F32, tn=512, vmem_mib=56, name="ssd_out_proj")


def kernel(x, norm_mix, norm_ffn, norm_final, pf_w_in, pf_b_forget, pf_pool_w, pf_pool_scale, pf_w_out, ssd_w_in, ssd_conv_w, ssd_conv_b, ssd_dt_bias, ssd_a_log, ssd_d_skip, ssd_gnorm, ssd_w_out, moe_router_g, moe_router_g_b, moe_router_e, moe_router_e_b, moe_w_gate, moe_w_up, moe_w_down):
    b, l, d = x.shape
    depth = norm_mix.shape[0]
    x2d = x.reshape(b * l, d)
    h = rmsnorm(x2d, norm_mix[0], BF16)
    for i in range(depth):
        j = i // 2
        if i % 2 == 0:
            x2d = pool_fox_layer(x2d, h, b, l, j, pf_w_in, pf_b_forget[j],
                                 pf_pool_w[j], pf_pool_scale[j], pf_w_out)
        else:
            x2d = ssd_layer(x2d, h, b, l, j, ssd_w_in, ssd_conv_w[j],
                            ssd_conv_b[j], ssd_dt_bias[j], ssd_a_log[j],
                            ssd_d_skip[j], ssd_gnorm[j], ssd_w_out)
        last = i == depth - 1
        g_norm = norm_final if last else norm_mix[i + 1]
        out = hier_moe(x2d, norm_ffn[i], moe_router_g[i], moe_router_g_b[i],
                       moe_router_e[i], moe_router_e_b[i], moe_w_gate,
                       moe_w_up, moe_w_down, i, g_norm, last)
        if last:
            x2d = out
        else:
            x2d, h = out
    return x2d.reshape(b, l, d)
```

```python
import functools

import jax
import jax.numpy as jnp
from jax import lax
from jax.experimental import pallas as pl
from jax.experimental.pallas import tpu as pltpu

F32 = jnp.float32
BF16 = jnp.bfloat16
I32 = jnp.int32
U32 = jnp.uint32
EPS = 1e-6

POOL_WINDOWS = (2, 4, 8, 16)
POOL_GROUP = 256
POOL_WIDTH = 1024
FOX_HEADS = 8
FOX_HEAD_DIM = 128
FOX_WIDTH = 1024
SSD_HEAD_DIM = 64
SSD_STATE = 128
SSD_GROUPS = 8
SSD_GROUP_HEADS = 8
SSD_GROUP_WIDTH = SSD_GROUP_HEADS * SSD_HEAD_DIM
SSD_CONV = 4
SSD_CHUNK = 128
MOE_GROUPS = 4
MOE_PER_GROUP = 8
MOE_EXPERTS = 32
MOE_BLOCK = 256
WEIGHT_DMA_PRIORITY = 1
ROW_SLOTS = 3
WEIGHT_SLOTS = 3
LANES = 128
SUBLANES = 8
MIB = 1 << 20


def _params(semantics, vmem_mib):
    return pltpu.CompilerParams(dimension_semantics=semantics,
                                vmem_limit_bytes=vmem_mib * MIB)


def _rmsnorm_body(x_ref, g_ref, o_ref):
    x = x_ref[...]
    inv = lax.rsqrt(jnp.mean(x * x, axis=-1, keepdims=True) + EPS)
    o_ref[...] = (x * inv * g_ref[...]).astype(o_ref.dtype)


def rmsnorm(x2d, g, out_dtype, tm=512):
    t, d = x2d.shape
    return pl.pallas_call(
        _rmsnorm_body,
        grid=(t // tm,),
        in_specs=[pl.BlockSpec((tm, d), lambda i: (i, 0)),
                  pl.BlockSpec((1, d), lambda i: (0, 0))],
        out_specs=pl.BlockSpec((tm, d), lambda i: (i, 0)),
        out_shape=jax.ShapeDtypeStruct((t, d), out_dtype),
        compiler_params=_params(("arbitrary",), 40),
        name="rmsnorm",
    )(x2d, g.reshape(1, d))


def _matmul_body(*refs, n_act, has_res, w_transposed, scaled_cols):
    refs = list(refs)
    a_refs = refs[:n_act]
    w_ref = refs[n_act]
    o_ref, wb_ref = refs[-2:]
    r_ref = refs[-3] if has_res else None
    out_axis = 0 if w_transposed else 1

    @pl.when(pl.program_id(1) == 0)
    def _():
        w = w_ref[0]
        if scaled_cols is not None:
            lo, hi, value = scaled_cols
            col = (lax.broadcasted_iota(I32, w.shape, out_axis)
                   + pl.program_id(0) * w.shape[out_axis])
            w = jnp.where(jnp.logical_and(col >= lo, col < hi), w * value, w)
        wb_ref[...] = w.astype(BF16)

    acc = None
    k0 = 0
    for a_ref in a_refs:
        kw = a_ref.shape[1]
        if w_transposed:
            part = lax.dot_general(a_ref[...], wb_ref[:, k0:k0 + kw],
                                   (((1,), (1,)), ((), ())),
                                   preferred_element_type=F32)
        else:
            part = jnp.dot(a_ref[...], wb_ref[k0:k0 + kw, :],
                           preferred_element_type=F32)
        acc = part if acc is None else acc + part
        k0 += kw
    if has_res:
        acc = acc + r_ref[...]
    o_ref[...] = acc.astype(o_ref.dtype)


def matmul(acts, w, layer, n_out, *, w_transposed=False, res=None,
           scaled_cols=None, out_dtype=BF16, tm=1024, tn=1024, vmem_mib=48,
           name="matmul"):
    t = acts[0].shape[0]
    k = sum(a.shape[1] for a in acts)
    assert w.shape[2 if w_transposed else 1] == k
    tn = min(tn, n_out)
    tm = min(tm, t)
    assert t % tm == 0 and n_out % tn == 0
    in_specs = [pl.BlockSpec((tm, a.shape[1]), lambda j, i: (i, 0)) for a in acts]
    if w_transposed:
        in_specs.append(pl.BlockSpec((1, tn, k), lambda j, i: (layer, j, 0)))
        wb_shape = (tn, k)
    else:
        in_specs.append(pl.BlockSpec((1, k, tn), lambda j, i: (layer, 0, j)))
        wb_shape = (k, tn)
    args = list(acts) + [w]
    if res is not None:
        in_specs.append(pl.BlockSpec((tm, tn), lambda j, i: (i, j)))
        args.append(res)
    return pl.pallas_call(
        functools.partial(_matmul_body, n_act=len(acts), has_res=res is not None,
                          w_transposed=w_transposed, scaled_cols=scaled_cols),
        grid=(n_out // tn, t // tm),
        in_specs=in_specs,
        out_specs=pl.BlockSpec((tm, tn), lambda j, i: (i, j)),
        out_shape=jax.ShapeDtypeStruct((t, n_out), out_dtype),
        scratch_shapes=[pltpu.VMEM(wb_shape, BF16)],
        compiler_params=_params(("arbitrary", "arbitrary"), vmem_mib),
        name=name,
    )(*args)


def _forget_cumsum_body(f_ref, b_ref, c_ref):
    z = f_ref[0] + b_ref[...]
    x = jnp.minimum(z, 0.0) - jnp.log1p(jnp.exp(-jnp.abs(z)))
    n = x.shape[1]
    lane = lax.broadcasted_iota(I32, x.shape, 1)
    shift = 1
    while shift < n:
        x = x + jnp.where(lane >= shift, pltpu.roll(x, shift, axis=1), 0.0)
        shift *= 2
    c_ref[0] = x


def forget_cumsum(f_t, b_forget):
    b, h, l = f_t.shape
    return pl.pallas_call(
        _forget_cumsum_body,
        grid=(b,),
        in_specs=[pl.BlockSpec((1, h, l), lambda i: (i, 0, 0)),
                  pl.BlockSpec((h, 1), lambda i: (0, 0))],
        out_specs=pl.BlockSpec((1, h, l), lambda i: (i, 0, 0)),
        out_shape=jax.ShapeDtypeStruct((b, h, l), F32),
        compiler_params=_params(("arbitrary",), 16),
        name="forget_cumsum",
    )(f_t, b_forget.reshape(h, 1))


def _pool_body(u_ref, w_ref, s_ref, o_ref):
    g = pl.program_id(1)
    u = u_ref[0].astype(F32)
    row = lax.broadcasted_iota(I32, u.shape, 0)
    acc = u
    sums = []
    for shift in (1, 2, 4, 8):
        acc = acc + jnp.where(row >= shift, pltpu.roll(acc, shift, axis=0), 0.0)
        sums.append(acc)
    win_sum = jnp.where(g == 0, sums[0],
                        jnp.where(g == 1, sums[1],
                                  jnp.where(g == 2, sums[2], sums[3])))
    window = jnp.left_shift(jnp.int32(2), g)
    count = jnp.minimum(row + 1, window).astype(F32)
    mixed = win_sum / count - u
    y = jnp.dot(mixed.astype(BF16), w_ref[0].astype(BF16),
                preferred_element_type=F32)
    o_ref[0] = (y * s_ref[...]).astype(o_ref.dtype)


def pool_mixer(proj3, pool_w, pool_scale):
    b, l, _ = proj3.shape
    ng = len(POOL_WINDOWS)
    return pl.pallas_call(
        _pool_body,
        grid=(b, ng),
        in_specs=[pl.BlockSpec((1, l, POOL_GROUP), lambda i, g: (i, 0, g)),
                  pl.BlockSpec((1, POOL_GROUP, POOL_GROUP), lambda i, g: (g, 0, 0)),
                  pl.BlockSpec((1, POOL_GROUP), lambda i, g: (0, g))],
        out_specs=pl.BlockSpec((1, l, POOL_GROUP), lambda i, g: (i, 0, g)),
        out_shape=jax.ShapeDtypeStruct((b, l, POOL_WIDTH), BF16),
        compiler_params=_params(("arbitrary", "arbitrary"), 40),
        name="pool_mixer",
    )(proj3, pool_w, pool_scale.reshape(1, POOL_WIDTH))


LOG2E = 1.4426950408889634
FOX_Q_SCALE = FOX_HEAD_DIM ** -0.5 * LOG2E


FOX_HEADS_PER_STEP = 2


def _weighted_values(p, v):
    d = v.shape[1]
    v_aug = jnp.concatenate([v, jnp.ones_like(v)], axis=1)
    out = jnp.dot(p.astype(BF16), v_aug, preferred_element_type=F32)
    return out[:, :d], out[:, d:d + 1]


def _fox_body(q_ref, k_ref, v_ref, cq_ref, ck_ref, o_ref, m_ref, l_ref, acc_ref,
              *, tile):
    hp = pl.program_id(1)
    qi = pl.program_id(2)
    hd = FOX_HEAD_DIM
    cq8 = cq_ref[0]
    head_lane = lax.broadcasted_iota(I32, cq8.shape, 1)
    heads = []
    for hh in range(FOX_HEADS_PER_STEP):
        head = hp * FOX_HEADS_PER_STEP + hh
        cq = jnp.sum(jnp.where(head_lane == head, cq8, 0.0), axis=1,
                     keepdims=True) * LOG2E
        heads.append((hh, head, slice(hh * hd, (hh + 1) * hd), cq))

    def logits(j, head, cols):
        start = pl.multiple_of(j * tile, tile)
        k = k_ref[0, pl.ds(start, tile), cols]
        ck = ck_ref[0, pl.ds(head, 1), pl.ds(start, tile)] * LOG2E
        s = lax.dot_general(q_ref[0, :, cols], k, (((1,), (1,)), ((), ())),
                            preferred_element_type=F32)
        return s - ck, start

    for hh, head, cols, cq in heads:
        s, start = logits(qi, head, cols)
        r = lax.broadcasted_iota(I32, s.shape, 0)
        c = lax.broadcasted_iota(I32, s.shape, 1)
        s = jnp.where(c <= r, s, -jnp.inf)
        m0 = jnp.max(s, axis=1, keepdims=True) + cq
        p = jnp.exp2(s + (cq - m0))
        m_ref[hh] = m0
        pv, row_sum = _weighted_values(p, v_ref[0, pl.ds(start, tile), cols])
        l_ref[hh] = row_sum
        acc_ref[hh] = pv

    def step(j, _):
        for hh, head, cols, cq in heads:
            s, start = logits(j, head, cols)
            m_old = m_ref[hh]
            m_new = jnp.maximum(m_old, jnp.max(s, axis=1, keepdims=True) + cq)
            alpha = jnp.exp2(m_old - m_new)
            p = jnp.exp2(s + (cq - m_new))
            pv, row_sum = _weighted_values(p, v_ref[0, pl.ds(start, tile), cols])
            l_ref[hh] = alpha * l_ref[hh] + row_sum
            acc_ref[hh] = alpha * acc_ref[hh] + pv
            m_ref[hh] = m_new
        return 0

    lax.fori_loop(0, qi, step, 0)
    for hh, head, cols, cq in heads:
        o_ref[0, :, cols] = (acc_ref[hh] / l_ref[hh]).astype(o_ref.dtype)


def fox_attention(proj3, c_row, c_col, tile=512):
    b, l, _ = proj3.shape
    hps = FOX_HEADS_PER_STEP
    w = hps * FOX_HEAD_DIM
    q0 = POOL_WIDTH // w
    k0 = q0 + FOX_HEADS // hps
    v0 = k0 + FOX_HEADS // hps
    return pl.pallas_call(
        functools.partial(_fox_body, tile=tile),
        grid=(b, FOX_HEADS // hps, l // tile),
        in_specs=[
            pl.BlockSpec((1, tile, w), lambda i, h, q: (i, q, q0 + h)),
            pl.BlockSpec((1, l, w), lambda i, h, q: (i, 0, k0 + h)),
            pl.BlockSpec((1, l, w), lambda i, h, q: (i, 0, v0 + h)),
            pl.BlockSpec((1, tile, FOX_HEADS), lambda i, h, q: (i, q, 0)),
            pl.BlockSpec((1, FOX_HEADS, l), lambda i, h, q: (i, 0, 0)),
        ],
        out_specs=pl.BlockSpec((1, tile, w), lambda i, h, q: (i, q, h)),
        out_shape=jax.ShapeDtypeStruct((b, l, FOX_WIDTH), BF16),
        scratch_shapes=[pltpu.VMEM((hps, tile, 1), F32),
                        pltpu.VMEM((hps, tile, 1), F32),
                        pltpu.VMEM((hps, tile, FOX_HEAD_DIM), F32)],
        compiler_params=_params(("arbitrary", "arbitrary", "arbitrary"), 32),
        name="fox_attention",
    )(proj3, proj3, proj3, c_col, c_row)


def _ssd_dt_body(raw_ref, bias_ref, alog_ref, dt_ref, acs_ref):
    z = raw_ref[...] + bias_ref[...]
    dt = jnp.maximum(z, 0.0) + jnp.log1p(jnp.exp(-jnp.abs(z)))
    a_dt = dt * (-jnp.exp(alog_ref[...]))
    n = z.shape[0]
    r = lax.broadcasted_iota(I32, (n, n), 0)
    c = lax.broadcasted_iota(I32, (n, n), 1)
    tri = (c <= r).astype(F32)
    acs = jnp.dot(tri, a_dt, preferred_element_type=F32,
                  precision=lax.Precision.HIGHEST)
    dt_ref[...] = dt.T
    acs_ref[...] = acs.T


def ssd_dt(raw, bias_pad, alog_pad):
    t, n = raw.shape
    assert n == SSD_CHUNK
    spec = pl.BlockSpec((SSD_CHUNK, n), lambda i: (i, 0))
    spec_t = pl.BlockSpec((n, SSD_CHUNK), lambda i: (0, i))
    vec = pl.BlockSpec((1, n), lambda i: (0, 0))
    return pl.pallas_call(
        _ssd_dt_body,
        grid=(t // SSD_CHUNK,),
        in_specs=[spec, vec, vec],
        out_specs=[spec_t, spec_t],
        out_shape=[jax.ShapeDtypeStruct((n, t), F32)] * 2,
        compiler_params=_params(("arbitrary",), 16),
        name="ssd_dt",
    )(raw, bias_pad.reshape(1, n), alog_pad.reshape(1, n))


def _silu(x):
    return x / (1.0 + jnp.exp(-x))


def _causal_conv_silu(u_ref, w_ref, b_ref):
    u = u_ref[0].astype(F32)
    w = w_ref[...]

    def conv(v, causal_rows):
        out = b_ref[...] + v * w[SSD_CONV - 1:SSD_CONV, :]
        for shift in range(1, SSD_CONV):
            prev = pltpu.roll(v, shift, axis=0)
            if causal_rows is not None:
                prev = jnp.where(causal_rows >= shift, prev, 0.0)
            out = out + prev * w[SSD_CONV - 1 - shift:SSD_CONV - shift, :]
        return out

    head = u[:SUBLANES]
    head_rows = lax.broadcasted_iota(I32, head.shape, 0)
    out = jnp.concatenate([conv(head, head_rows), conv(u, None)[SUBLANES:]], axis=0)
    return _silu(out)


def _pair_lanes(cols, j):
    rows = cols.shape[0]
    lane = lax.broadcasted_iota(I32, (rows, LANES), 1)
    lo = jnp.broadcast_to(cols[:, 2 * j:2 * j + 1], (rows, LANES))
    hi = jnp.broadcast_to(cols[:, 2 * j + 1:2 * j + 2], (rows, LANES))
    return jnp.where(lane < SSD_HEAD_DIM, lo, hi)


def _ssd_body(z_ref, x_ref, b_ref, c_ref, wx_ref, wb_ref, wc_ref,
              bx_ref, bb_ref, bc_ref, dtt_ref, acst_ref,
              dskip_ref, gn_ref, o_ref, xs_ref, bs_ref, cs_ref, st_ref):
    q = SSD_CHUNK
    xs_ref[...] = _causal_conv_silu(x_ref, wx_ref, bx_ref)
    bs_ref[...] = _causal_conv_silu(b_ref, wb_ref, bb_ref).astype(BF16)
    cs_ref[...] = _causal_conv_silu(c_ref, wc_ref, bc_ref).astype(BF16)
    st_ref[...] = jnp.zeros_like(st_ref)

    r = lax.broadcasted_iota(I32, (q, q), 0)
    c = lax.broadcasted_iota(I32, (q, q), 1)
    causal = c <= r
    lane = lax.broadcasted_iota(I32, (q, LANES), 1)
    first_head = lane < SSD_HEAD_DIM

    def chunk(ci, _):
        r0 = pl.multiple_of(ci * q, q)
        xc = xs_ref[pl.ds(r0, q), :]
        bm = bs_ref[pl.ds(r0, q), :]
        cm = cs_ref[pl.ds(r0, q), :]
        zc = z_ref[0, pl.ds(r0, q), :].astype(F32)
        dt_t = dtt_ref[0, :, pl.ds(r0, q)]
        acs_t = acst_ref[0, :, pl.ds(r0, q)]
        acs = acs_t.T
        exp_acs = jnp.exp(acs)
        chunk_decay = jnp.exp(acs[q - 1:q, :])
        w_diag_t = dt_t
        w_end_t = dt_t * jnp.exp(acs_t[:, q - 1:q] - acs_t)
        cb = lax.dot_general(cm, bm, (((1,), (1,)), ((), ())),
                             preferred_element_type=F32)
        bm_t = bm.astype(F32).T
        gated = []
        for j in range(SSD_GROUP_HEADS // 2):
            lanes = slice(j * LANES, (j + 1) * LANES)
            xp = xc[:, lanes]
            xp_b = xp.astype(BF16)
            y_heads = []
            s_heads = []
            for hh in range(2):
                hd = 2 * j + hh
                seg = jnp.broadcast_to(acs[:, hd:hd + 1], (q, q)) - acs_t[hd:hd + 1, :]
                decay = jnp.exp(jnp.where(causal, seg, -jnp.inf))
                mix = cb * decay * w_diag_t[hd:hd + 1, :]
                y_heads.append(jnp.dot(mix.astype(BF16), xp_b,
                                       preferred_element_type=F32))
                b_end = (bm_t * w_end_t[hd:hd + 1, :]).astype(BF16)
                s_heads.append(jnp.dot(b_end, xp_b, preferred_element_type=F32))
            y = jnp.where(first_head, y_heads[0], y_heads[1])
            state = st_ref[:, lanes]
            y = y + jnp.dot(cm, state.astype(BF16),
                            preferred_element_type=F32) * _pair_lanes(exp_acs, j)
            y = y + xp * dskip_ref[:, lanes]
            st_ref[:, lanes] = state * _pair_lanes(chunk_decay, j) + jnp.where(
                first_head, s_heads[0], s_heads[1])
            gated.append(y * _silu(zc[:, lanes]))
        gated = jnp.concatenate(gated, axis=1)
        inv = lax.rsqrt(jnp.mean(gated * gated, axis=-1, keepdims=True) + EPS)
        o_ref[0, pl.ds(r0, q), :] = (gated * inv * gn_ref[...]).astype(o_ref.dtype)
        return 0

    lax.fori_loop(0, x_ref.shape[1] // q, chunk, 0)


def ssd_core(proj3, conv_w, conv_b, dt_gt, acs_gt, d_skip_exp, gnorm):
    b, l, _ = proj3.shape
    gw = SSD_GROUP_WIDTH
    n = SSD_STATE
    inner = SSD_GROUPS * gw
    x_blk0 = inner // gw
    bm_blk0 = 2 * inner // n
    cm_blk0 = bm_blk0 + SSD_GROUPS
    wb_blk0 = inner // n
    wc_blk0 = wb_blk0 + SSD_GROUPS
    hpg = SSD_GROUP_HEADS
    conv_b2 = conv_b.reshape(1, -1)
    return pl.pallas_call(
        _ssd_body,
        grid=(b, SSD_GROUPS),
        in_specs=[
            pl.BlockSpec((1, l, gw), lambda i, g: (i, 0, g)),
            pl.BlockSpec((1, l, gw), lambda i, g: (i, 0, x_blk0 + g)),
            pl.BlockSpec((1, l, n), lambda i, g: (i, 0, bm_blk0 + g)),
            pl.BlockSpec((1, l, n), lambda i, g: (i, 0, cm_blk0 + g)),
            pl.BlockSpec((SSD_CONV, gw), lambda i, g: (0, g)),
            pl.BlockSpec((SSD_CONV, n), lambda i, g: (0, wb_blk0 + g)),
            pl.BlockSpec((SSD_CONV, n), lambda i, g: (0, wc_blk0 + g)),
            pl.BlockSpec((1, gw), lambda i, g: (0, g)),
            pl.BlockSpec((1, n), lambda i, g: (0, wb_blk0 + g)),
            pl.BlockSpec((1, n), lambda i, g: (0, wc_blk0 + g)),
            pl.BlockSpec((1, hpg, l), lambda i, g: (g, 0, i)),
            pl.BlockSpec((1, hpg, l), lambda i, g: (g, 0, i)),
            pl.BlockSpec((1, gw), lambda i, g: (0, g)),
            pl.BlockSpec((1, gw), lambda i, g: (0, g)),
        ],
        out_specs=pl.BlockSpec((1, l, gw), lambda i, g: (i, 0, g)),
        out_shape=jax.ShapeDtypeStruct((b, l, inner), BF16),
        scratch_shapes=[pltpu.VMEM((l, gw), F32), pltpu.VMEM((l, n), BF16),
                        pltpu.VMEM((l, n), BF16), pltpu.VMEM((n, gw), F32)],
        compiler_params=_params(("arbitrary", "arbitrary"), 48),
        name="ssd_core",
    )(proj3, proj3, proj3, proj3, conv_w, conv_w, conv_w, conv_b2, conv_b2,
      conv_b2, dt_gt, acs_gt, d_skip_exp, gnorm.reshape(1, inner))


def _first_argmax(vals, nrows):
    row = lax.broadcasted_iota(I32, vals.shape, 0)
    top = jnp.max(vals, axis=0, keepdims=True)
    idx = jnp.min(jnp.where(vals == top, row, nrows), axis=0, keepdims=True)
    return top, idx, row


def _pack_bf16_halves(y):
    half = y.shape[1] // 2

    def bits(v):
        return lax.bitcast_convert_type(v.astype(BF16).astype(F32), U32)

    return bits(y[:, half:]) | (bits(y[:, :half]) >> 16)


def _unpack_bf16_halves(w):
    lo = lax.bitcast_convert_type(w << 16, F32)
    hi = lax.bitcast_convert_type(w & jnp.uint32(0xFFFF0000), F32)
    return lo, hi


def _router_body(x_ref, g_ref, rt_ref, rb_ref, eid_ref, wt_ref, rank_ref,
                 cnt_ref, hp_ref, carry_ref):
    step = pl.program_id(0)

    @pl.when(step == 0)
    def _():
        carry_ref[...] = jnp.zeros_like(carry_ref)

    x = x_ref[...]
    tm = x.shape[0]
    h = x * lax.rsqrt(jnp.mean(x * x, axis=-1, keepdims=True) + EPS) * g_ref[...]
    hp_ref[...] = _pack_bf16_halves(h)
    def split(v):
        hi = v.astype(BF16)
        return hi, (v - hi.astype(F32)).astype(BF16)

    def dot_nt(a, b):
        return lax.dot_general(a, b, (((1,), (1,)), ((), ())),
                               preferred_element_type=F32)

    h_hi, h_lo = split(h)
    rt_hi, rt_lo = split(rt_ref[...])
    logits = (dot_nt(rt_hi, h_hi) + dot_nt(rt_hi, h_lo) + dot_nt(rt_lo, h_hi)
              + rb_ref[...])
    e_logits = logits[:MOE_EXPERTS]
    g_logits = logits[MOE_EXPERTS:MOE_EXPERTS + MOE_GROUPS]
    g_max, g_sel, _ = _first_argmax(g_logits, MOE_GROUPS)
    g_w = 1.0 / jnp.sum(jnp.exp(g_logits - g_max), axis=0, keepdims=True)
    sel = jnp.zeros((MOE_PER_GROUP, tm), F32)
    for grp in range(MOE_GROUPS):
        sel = jnp.where(g_sel == grp,
                        e_logits[grp * MOE_PER_GROUP:(grp + 1) * MOE_PER_GROUP], sel)
    m1, i1, row8 = _first_argmax(sel, MOE_PER_GROUP)
    rest = jnp.where(row8 == i1, -jnp.inf, sel)
    m2, i2, _ = _first_argmax(rest, MOE_PER_GROUP)
    p2 = jnp.exp(m2 - m1)
    w1 = g_w / (1.0 + p2)
    w2 = g_w * p2 / (1.0 + p2)
    e1 = g_sel * MOE_PER_GROUP + i1
    e2 = g_sel * MOE_PER_GROUP + i2

    r = lax.broadcasted_iota(I32, (tm, tm), 0)
    c = lax.broadcasted_iota(I32, (tm, tm), 1)
    upper = jnp.where(r <= c, 1.0, 0.0).astype(BF16)
    row32 = lax.broadcasted_iota(I32, (MOE_EXPERTS, tm), 0)
    hit1 = row32 == e1
    hit2 = row32 == e2
    cum1 = jnp.dot(jnp.where(hit1, 1.0, 0.0).astype(BF16), upper,
                   preferred_element_type=F32)
    cum2 = jnp.dot(jnp.where(hit2, 1.0, 0.0).astype(BF16), upper,
                   preferred_element_type=F32)
    carry = carry_ref[...]
    tot1 = cum1[:, tm - 1:tm]
    tot2 = cum2[:, tm - 1:tm]
    rank1 = jnp.sum(jnp.where(hit1, carry + cum1 - 1.0, 0.0), axis=0, keepdims=True)
    rank2 = jnp.sum(jnp.where(hit2, carry + tot1 + cum2 - 1.0, 0.0), axis=0,
                    keepdims=True)
    new_carry = carry + tot1 + tot2
    carry_ref[...] = new_carry

    eid_ref[...] = jnp.concatenate([e1, e2], axis=0)
    wt_ref[...] = jnp.concatenate([w1, w2], axis=0)
    rank_ref[...] = jnp.concatenate([rank1, rank2], axis=0).astype(I32)
    cnt_ref[...] = jnp.broadcast_to(new_carry, cnt_ref.shape).astype(I32)


def moe_router(x2d, g, router_g, router_g_b, router_e, router_e_b, tm=512):
    t, d = x2d.shape
    pad = 8 - MOE_GROUPS
    rt = jnp.concatenate([router_e.T, router_g.T, jnp.zeros((pad, d), F32)], axis=0)
    rb = jnp.concatenate([router_e_b, router_g_b, jnp.zeros((pad,), F32)]).reshape(-1, 1)
    nr = rt.shape[0]
    tok = pl.BlockSpec((2, tm), lambda i: (0, i))
    return pl.pallas_call(
        _router_body,
        grid=(t // tm,),
        in_specs=[pl.BlockSpec((tm, d), lambda i: (i, 0)),
                  pl.BlockSpec((1, d), lambda i: (0, 0)),
                  pl.BlockSpec((nr, d), lambda i: (0, 0)),
                  pl.BlockSpec((nr, 1), lambda i: (0, 0))],
        out_specs=[tok, tok, tok,
                   pl.BlockSpec((MOE_EXPERTS, LANES), lambda i: (0, 0)),
                   pl.BlockSpec((tm, d // 2), lambda i: (i, 0))],
        out_shape=[jax.ShapeDtypeStruct((2, t), I32),
                   jax.ShapeDtypeStruct((2, t), F32),
                   jax.ShapeDtypeStruct((2, t), I32),
                   jax.ShapeDtypeStruct((MOE_EXPERTS, LANES), I32),
                   jax.ShapeDtypeStruct((t, d // 2), U32)],
        scratch_shapes=[pltpu.VMEM((MOE_EXPERTS, 1), F32)],
        compiler_params=_params(("arbitrary",), 40),
        name="moe_router",
    )(x2d, g.reshape(1, d), rt, rb)


def _experts_body(blk_e_ref, n_used_ref, next_e_ref, next2_e_ref, grp_ref, dest_ref,
                  hp_hbm, wg_hbm, wu_hbm, wd_hbm, o_ref,
                  xbuf_ref, xsem, wg_st, wu_st, wd_st, wsem,
                  wgb_ref, wub_ref, wdb_ref, tok_ref, *, layer):
    i = pl.program_id(0)
    n_used = n_used_ref[0]
    used = i < n_used
    e_cur = blk_e_ref[i]
    prev = blk_e_ref[jnp.maximum(i - 1, 0)]
    fresh = jnp.logical_and(used, jnp.logical_or(i == 0, e_cur != prev))
    rows = xbuf_ref.shape[1]

    def weight_copies(e, slot):
        return (pltpu.make_async_copy(wg_hbm.at[layer, e], wg_st.at[slot],
                                      wsem.at[0, slot]),
                pltpu.make_async_copy(wu_hbm.at[layer, e], wu_st.at[slot],
                                      wsem.at[1, slot]),
                pltpu.make_async_copy(wd_hbm.at[layer, e], wd_st.at[slot],
                                      wsem.at[2, slot]))

    def row_copy(blk, slot, r):
        tok = tok_ref[blk * rows + r]
        return pltpu.make_async_copy(hp_hbm.at[pl.ds(tok, 1)],
                                     xbuf_ref.at[slot, pl.ds(r, 1)],
                                     xsem.at[slot])

    @pl.when(i == 0)
    def _():
        for cp in weight_copies(e_cur, 0):
            cp.start(priority=WEIGHT_DMA_PRIORITY)

        @pl.when(next_e_ref[0] >= 0)
        def _():
            for cp in weight_copies(next_e_ref[0], 1):
                cp.start(priority=WEIGHT_DMA_PRIORITY)

        n_tok = dest_ref.shape[0] // 2

        def clear(r, _):
            tok_ref[r] = 0
            return 0
        lax.fori_loop(0, tok_ref.shape[0], clear, 0, unroll=8)

        def fill(tok, _):
            tok_ref[dest_ref[tok]] = tok
            tok_ref[dest_ref[n_tok + tok]] = tok
            return 0
        lax.fori_loop(0, n_tok, fill, 0, unroll=8)

        def first(r, _):
            row_copy(0, 0, r).start()
            return 0
        lax.fori_loop(0, rows, first, 0, unroll=8)

        @pl.when(n_used > 1)
        def _():
            def second(r, _):
                row_copy(1, 1, r).start()
                return 0
            lax.fori_loop(0, rows, second, 0, unroll=8)

    wslot = grp_ref[i] % WEIGHT_SLOTS
    ahead_e = next2_e_ref[i]

    @pl.when(jnp.logical_and(fresh, ahead_e >= 0))
    def _():
        for cp in weight_copies(ahead_e, (grp_ref[i] + 2) % WEIGHT_SLOTS):
            cp.start(priority=WEIGHT_DMA_PRIORITY)

    @pl.when(fresh)
    def _():
        for cp in weight_copies(e_cur, wslot):
            cp.wait()
        wgb_ref[...] = wg_st[wslot].astype(BF16)
        wub_ref[...] = wu_st[wslot].astype(BF16)
        wdb_ref[...] = wd_st[wslot].astype(BF16)

    def compute(gather_ahead):
        slot = i % ROW_SLOTS
        pltpu.make_async_copy(hp_hbm.at[pl.ds(0, rows)], xbuf_ref.at[slot],
                              xsem.at[slot]).wait()
        if gather_ahead:
            ahead = i + ROW_SLOTS - 1
            for r in range(rows):
                row_copy(ahead, ahead % ROW_SLOTS, r).start()
        lo, hi = _unpack_bf16_halves(xbuf_ref[slot])
        h = jnp.concatenate([lo.astype(BF16), hi.astype(BF16)], axis=1)
        gate = jnp.dot(h, wgb_ref[...], preferred_element_type=F32)
        up = jnp.dot(h, wub_ref[...], preferred_element_type=F32)
        act = (_silu(gate) * up).astype(BF16)
        y = jnp.dot(act, wdb_ref[...], preferred_element_type=F32)
        o_ref[...] = _pack_bf16_halves(y)

    has_ahead = i + ROW_SLOTS - 1 < n_used

    @pl.when(has_ahead)
    def _():
        compute(True)

    @pl.when(jnp.logical_and(used, jnp.logical_not(has_ahead)))
    def _():
        compute(False)

    @pl.when(jnp.logical_not(used))
    def _():
        o_ref[...] = jnp.zeros_like(o_ref)


def moe_experts(blk_e, n_used, next_e, next2_e, grp, dest, n_rows, hp, w_gate, w_up,
                w_down, layer):
    half = hp.shape[1]
    d = 2 * half
    ff = w_gate.shape[3]
    nblk = n_rows // MOE_BLOCK
    hbm = pl.BlockSpec(memory_space=pl.ANY)
    return pl.pallas_call(
        functools.partial(_experts_body, layer=layer),
        grid_spec=pltpu.PrefetchScalarGridSpec(
            num_scalar_prefetch=6,
            grid=(nblk,),
            in_specs=[hbm, hbm, hbm, hbm],
            out_specs=pl.BlockSpec((MOE_BLOCK, half), lambda i, *_: (i, 0)),
            scratch_shapes=[pltpu.VMEM((ROW_SLOTS, MOE_BLOCK, half), U32),
                            pltpu.SemaphoreType.DMA((ROW_SLOTS,)),
                            pltpu.VMEM((WEIGHT_SLOTS, d, ff), F32),
                            pltpu.VMEM((WEIGHT_SLOTS, d, ff), F32),
                            pltpu.VMEM((WEIGHT_SLOTS, ff, d), F32),
                            pltpu.SemaphoreType.DMA((3, WEIGHT_SLOTS)),
                            pltpu.VMEM((d, ff), BF16), pltpu.VMEM((d, ff), BF16),
                            pltpu.VMEM((ff, d), BF16),
                            pltpu.SMEM((n_rows,), I32)],
        ),
        out_shape=jax.ShapeDtypeStruct((n_rows, half), U32),
        compiler_params=_params(("arbitrary",), 60),
        name="moe_experts",
    )(blk_e, n_used, next_e, next2_e, grp, dest, hp, w_gate, w_up, w_down)


def _combine_body(dest_ref, x_ref, wt_ref, g_ref, ys_ref, *rest, final_norm):
    if final_norm:
        o_ref, buf_ref, sem = rest
        h_ref = None
    else:
        o_ref, h_ref, buf_ref, sem = rest
    i = pl.program_id(0)
    n = pl.num_programs(0)
    tm = x_ref.shape[0]
    t = tm * n

    def issue(tile, slot):
        def body(r, _):
            for k in range(2):
                d = dest_ref[k * t + tile * tm + r]
                pltpu.make_async_copy(ys_ref.at[pl.ds(d, 1)],
                                      buf_ref.at[slot, k, pl.ds(r, 1)],
                                      sem.at[slot]).start(priority=k)
            return 0
        lax.fori_loop(0, tm, body, 0, unroll=8)

    @pl.when(i == 0)
    def _():
        issue(0, 0)

    @pl.when(i + 1 < n)
    def _():
        issue(i + 1, (i + 1) % 2)

    slot = i % 2
    for k in range(2):
        pltpu.make_async_copy(ys_ref.at[pl.ds(0, tm)], buf_ref.at[slot, k],
                              sem.at[slot]).wait()
    w = wt_ref[...]
    lo1, hi1 = _unpack_bf16_halves(buf_ref[slot, 0])
    lo2, hi2 = _unpack_bf16_halves(buf_ref[slot, 1])
    half = lo1.shape[1]
    x = x_ref[...]
    y_lo = x[:, :half] + w[:, 0:1] * lo1 + w[:, 1:2] * lo2
    y_hi = x[:, half:] + w[:, 0:1] * hi1 + w[:, 1:2] * hi2
    ssq = (jnp.sum(y_lo * y_lo, axis=-1, keepdims=True)
           + jnp.sum(y_hi * y_hi, axis=-1, keepdims=True))
    inv = lax.rsqrt(ssq / (2 * half) + EPS)
    g = g_ref[...]
    n_lo = y_lo * inv * g[:, :half]
    n_hi = y_hi * inv * g[:, half:]
    if final_norm:
        o_ref[:, :half] = n_lo
        o_ref[:, half:] = n_hi
    else:
        o_ref[:, :half] = y_lo
        o_ref[:, half:] = y_hi
        h_ref[:, :half] = n_lo.astype(h_ref.dtype)
        h_ref[:, half:] = n_hi.astype(h_ref.dtype)


def moe_combine(dest_flat, x2d, wt_t, ys, g_norm, final_norm, tm=256):
    t, d = x2d.shape
    row_spec = pl.BlockSpec((tm, d), lambda i, ds: (i, 0))
    if final_norm:
        out_specs = row_spec
        out_shape = jax.ShapeDtypeStruct((t, d), F32)
    else:
        out_specs = [row_spec, row_spec]
        out_shape = [jax.ShapeDtypeStruct((t, d), F32),
                     jax.ShapeDtypeStruct((t, d), BF16)]
    return pl.pallas_call(
        functools.partial(_combine_body, final_norm=final_norm),
        grid_spec=pltpu.PrefetchScalarGridSpec(
            num_scalar_prefetch=1,
            grid=(t // tm,),
            in_specs=[row_spec,
                      pl.BlockSpec((tm, 2), lambda i, ds: (i, 0)),
                      pl.BlockSpec((1, d), lambda i, ds: (0, 0)),
                      pl.BlockSpec(memory_space=pl.ANY)],
            out_specs=out_specs,
            scratch_shapes=[pltpu.VMEM((2, 2, tm, d // 2), U32),
                            pltpu.SemaphoreType.DMA((2,))],
        ),
        out_shape=out_shape,
        compiler_params=_params(("arbitrary",), 32),
        name="moe_combine",
    )(dest_flat, x2d, wt_t, g_norm.reshape(1, d), ys)


def hier_moe(x2d, g_ffn, router_g, router_g_b, router_e, router_e_b,
             w_gate, w_up, w_down, layer, g_norm, final_norm):
    t, d = x2d.shape
    eid, wts, rank, cnt, hp = moe_router(x2d, g_ffn, router_g, router_g_b,
                                         router_e, router_e_b)
    counts = cnt[:, 0]
    padded = (counts + MOE_BLOCK - 1) // MOE_BLOCK * MOE_BLOCK
    pend = jnp.cumsum(padded)
    pstart = pend - padded
    n_rows = 2 * t + MOE_EXPERTS * MOE_BLOCK
    nblk = n_rows // MOE_BLOCK
    expert_ids = jnp.arange(MOE_EXPERTS, dtype=I32)
    start_of = jnp.sum(jnp.where(eid[..., None] == expert_ids, pstart, 0), axis=-1)
    dest = (start_of + rank).reshape(-1)
    blk_start = jnp.arange(nblk, dtype=I32) * MOE_BLOCK
    blk_e = jnp.sum(blk_start[:, None] >= pend[None, :], axis=1).astype(I32)
    blk_e = jnp.minimum(blk_e, MOE_EXPERTS - 1)
    n_used = (pend[-1:] // MOE_BLOCK).astype(I32)
    last_e = blk_e[jnp.maximum(n_used[0] - 1, 0)]
    blk_e = jnp.where(jnp.arange(nblk) < n_used[0], blk_e, last_e)
    changed = jnp.concatenate([jnp.ones((1,), I32),
                               (blk_e[1:] != blk_e[:-1]).astype(I32)])
    grp = jnp.cumsum(changed) - 1
    grp_end = jnp.sum(jnp.where(blk_e[:, None] == expert_ids, pend, 0), axis=-1)
    nxt_blk = grp_end // MOE_BLOCK
    nxt_e = jnp.sum(jnp.where(nxt_blk[:, None] == jnp.arange(nblk), blk_e, 0), axis=-1)
    next_e = jnp.where(nxt_blk < n_used[0], nxt_e, -1).astype(I32)
    nxt2_e = jnp.sum(jnp.where(nxt_blk[:, None] == jnp.arange(nblk), next_e, 0), axis=-1)
    next2_e = jnp.where(nxt_blk < n_used[0], nxt2_e, -1).astype(I32)
    ys = moe_experts(blk_e, n_used, next_e, next2_e, grp.astype(I32), dest, n_rows,
                     hp, w_gate, w_up, w_down, layer)
    return moe_combine(dest, x2d, wts.T, ys, g_norm, final_norm)


def _pad_cols(w, n):
    return jnp.pad(w, ((0, 0), (0, n - w.shape[1])))


def pool_fox_layer(x2d, h, b, l, j, w_in, b_forget, pool_w, pool_scale, w_out):
    main = POOL_WIDTH + 3 * FOX_WIDTH
    w_in_t = jnp.swapaxes(w_in, 1, 2)
    q_cols = (POOL_WIDTH, POOL_WIDTH + FOX_WIDTH, FOX_Q_SCALE)
    proj = matmul([h], w_in_t, j, main, w_transposed=True, scaled_cols=q_cols,
                  tm=2048, vmem_mib=56, name="pf_in_proj")
    f = matmul([h], _pad_cols(w_in[j, :, main:], LANES)[None], 0, LANES,
               out_dtype=F32, name="pf_forget_proj")
    f_t = f[:, :FOX_HEADS].reshape(b, l, FOX_HEADS).transpose(0, 2, 1)
    c = forget_cumsum(f_t, b_forget)
    proj3 = proj.reshape(b, l, main)
    y_pool = pool_mixer(proj3, pool_w, pool_scale)
    y_att = fox_attention(proj3, c, c.transpose(0, 2, 1))
    acts = [y_pool.reshape(b * l, POOL_WIDTH), y_att.reshape(b * l, FOX_WIDTH)]
    return matmul(acts, w_out, j, w_out.shape[2], res=x2d, out_dtype=F32,
                  vmem_mib=56, name="pf_out_proj")


def ssd_layer(x2d, h, b, l, j, w_in, conv_w, conv_b, dt_bias, a_log, d_skip,
              gnorm, w_out):
    heads = dt_bias.shape[0]
    inner = heads * SSD_HEAD_DIM
    main = 2 * inner + 2 * SSD_GROUPS * SSD_STATE
    w_in_t = jnp.swapaxes(w_in, 1, 2)
    proj = matmul([h], w_in_t, j, main, w_transposed=True, tm=2048, vmem_mib=56,
                  name="ssd_in_proj")
    raw = matmul([h], _pad_cols(w_in[j, :, main:], LANES)[None], 0, LANES,
                 out_dtype=F32, name="ssd_dt_proj")
    dt_t, acs_t = ssd_dt(raw, jnp.pad(dt_bias, (0, LANES - heads)),
                         jnp.pad(a_log, (0, LANES - heads)))
    hpg = SSD_GROUP_HEADS
    dt_gt = dt_t[:heads].reshape(SSD_GROUPS, hpg, b * l)
    acs_gt = acs_t[:heads].reshape(SSD_GROUPS, hpg, b * l)
    d_skip_exp = jnp.repeat(d_skip, SSD_HEAD_DIM).reshape(1, inner)
    y = ssd_core(proj.reshape(b, l, main), conv_w, conv_b, dt_gt, acs_gt,
                 d_skip_exp, gnorm)
    return matmul([y.reshape(b * l, inner)], w_out, j, w_out.shape[2], res=x2d,
                  out_dtype=F32, tn=512, vmem_mib=56, name="ssd_out_proj")


def kernel(x, norm_mix, norm_ffn, norm_final, pf_w_in, pf_b_forget, pf_pool_w, pf_pool_scale, pf_w_out, ssd_w_in, ssd_conv_w, ssd_conv_b, ssd_dt_bias, ssd_a_log, ssd_d_skip, ssd_gnorm, ssd_w_out, moe_router_g, moe_router_g_b, moe_router_e, moe_router_e_b, moe_w_gate, moe_w_up, moe_w_down):
    b, l, d = x.shape
    depth = norm_mix.shape[0]
    x2d = x.reshape(b * l, d)
    h = rmsnorm(x2d, norm_mix[0], BF16)
    for i in range(depth):
        j = i // 2
        if i % 2 == 0:
            x2d = pool_fox_layer(x2d, h, b, l, j, pf_w_in, pf_b_forget[j],
                                 pf_pool_w[j], pf_pool_scale[j], pf_w_out)
        else:
            x2d = ssd_layer(x2d, h, b, l, j, ssd_w_in, ssd_conv_w[j],
                            ssd_conv_b[j], ssd_dt_bias[j], ssd_a_log[j],
                            ssd_d_skip[j], ssd_gnorm[j], ssd_w_out)
        last = i == depth - 1
        g_norm = norm_final if last else norm_mix[i + 1]
        out = hier_moe(x2d, norm_ffn[i], moe_router_g[i], moe_router_g_b[i],
                       moe_router_e[i], moe_router_e_b[i], moe_w_gate,
                       moe_w_up, moe_w_down, i, g_norm, last)
        if last:
            x2d = out
        else:
            x2d, h = out
    return x2d.reshape(b, l, d)
```
